```python
import jax
import jax.numpy as jnp
from jax import lax
import numpy as np

D_MODEL = 1024
BATCH = 4
SEQ = 4096
DEPTH = 2
DEC_BATCH = 32
DEC_SEQ = 4
PAST_LEN = 16384
PAGE_SIZE = 128

N_AB_LAYERS = (DEPTH + 1) // 2
N_C_LAYERS = DEPTH // 2

A_HEADS = 8
A_HEAD_DIM = 64
A_WIDTH = A_HEADS * A_HEAD_DIM
Q_BLOCK = 128
B_WIDTH = D_MODEL // 2
B_CONV = 3
AB_IN_WIDTH = 3 * A_WIDTH + A_HEADS + 3 * B_WIDTH
AB_OUT_WIDTH = A_WIDTH + B_WIDTH
C_HEADS = 8
C_KEY_DIM = 128
C_VAL_DIM = 128
C_QK_WIDTH = C_HEADS * C_KEY_DIM
C_V_WIDTH = C_HEADS * C_VAL_DIM
C_CONV = 4
C_CONV_WIDTH = 2 * C_QK_WIDTH + C_V_WIDTH
C_IN_WIDTH = C_CONV_WIDTH + C_V_WIDTH + 2 * C_HEADS
C_CHUNK = 64
D_FF = 2816
NORM_EPS = 1e-6
DT_MIN = 1e-3
DT_MAX = 1e-1

kernel_name = 'fox_shortconv_gdn_macaron_decode_step'


def rmsnorm(x, g):
    xf = x.astype(jnp.float32)
    y = xf * lax.rsqrt(jnp.mean(xf * xf, axis=-1, keepdims=True) + NORM_EPS)
    return (y * g.astype(jnp.float32)).astype(x.dtype)


def l2norm(x):
    xf = x.astype(jnp.float32)
    return xf * lax.rsqrt(jnp.sum(xf * xf, axis=-1, keepdims=True) + NORM_EPS)


def swiglu(x, w_gate, w_up, w_down):
    return (jax.nn.silu(x @ w_gate) * (x @ w_up)) @ w_down


def causal_depthwise_conv(u, buf, w):
    width = w.shape[0]
    length = u.shape[1]
    ext = jnp.concatenate([buf.astype(u.dtype), u], axis=1)
    out = ext[:, 0:length] * w[0]
    for j in range(1, width):
        out = out + ext[:, j:j + length] * w[j]
    return out, ext[:, length:]


def gather_pages(pool, layer, page_table):
    g = pool[layer, page_table]
    return g.reshape((g.shape[0], g.shape[1] * g.shape[2]) + g.shape[3:])


def fox_attention(q, k, v, logf, q_offset):
    bsz, lq, h, d = q.shape
    lk = k.shape[1]
    c_k = jnp.cumsum(logf.astype(jnp.float32), axis=1).transpose(0, 2, 1)
    qb = min(Q_BLOCK, lq)
    nb = lq // qb
    scale = d ** -0.5
    q_blocks = q.reshape(bsz, nb, qb, h, d).transpose(1, 0, 2, 3, 4)
    pos_blocks = (q_offset + jnp.arange(lq)).reshape(nb, qb)
    key_pos = jnp.arange(lk)

    def block(args):
        q_blk, q_pos = args
        s = jnp.einsum('bqhd,bkhd->bhqk', q_blk, k).astype(jnp.float32) * scale
        c_q = jnp.take(c_k, q_pos, axis=2)
        s = s + c_q[..., :, None] - c_k[..., None, :]
        s = jnp.where(key_pos[None, :] <= q_pos[:, None], s, -jnp.inf)
        p = jax.nn.softmax(s, axis=-1).astype(v.dtype)
        return jnp.einsum('bhqk,bkhd->bqhd', p, v)

    o = lax.map(block, (q_blocks, pos_blocks))
    return o.transpose(1, 0, 2, 3, 4).reshape(bsz, lq, h, d)


def gated_delta_rule(q, k, v, g, beta, s0):
    bsz, length, h, dk = q.shape
    dv = v.shape[-1]
    cs = min(C_CHUNK, length)
    pad = (-length) % cs
    if pad:
        pw = ((0, 0), (0, pad), (0, 0), (0, 0))
        q = jnp.pad(q, pw)
        k = jnp.pad(k, pw)
        v = jnp.pad(v, pw)
        g = jnp.pad(g, pw[:3])
        beta = jnp.pad(beta, pw[:3])
    n = (length + pad) // cs

    def chunks(t):
        t = t.reshape((bsz, n, cs) + t.shape[2:])
        return jnp.moveaxis(t, 3, 2).swapaxes(0, 1)

    qc, kc, vc, gc, bc = chunks(q), chunks(k), chunks(v), chunks(g), chunks(beta)
    G = jnp.cumsum(gc, axis=-1)
    incl = jnp.tril(jnp.ones((cs, cs), dtype=bool))
    strict = jnp.tril(jnp.ones((cs, cs), dtype=bool), -1)
    diff = G[..., :, None] - G[..., None, :]
    decay = jnp.where(incl, jnp.exp(jnp.where(incl, diff, 0.0)), 0.0)
    kk = jnp.einsum('nbhtd,nbhsd->nbhts', kc, kc)
    a_mat = jnp.where(strict, bc[..., :, None] * kk * decay, 0.0) + jnp.eye(cs, dtype=jnp.float32)
    w_v = lax.linalg.triangular_solve(a_mat, bc[..., None] * vc, left_side=True, lower=True, unit_diagonal=True)
    w_k = lax.linalg.triangular_solve(a_mat, (bc * jnp.exp(G))[..., None] * kc, left_side=True, lower=True, unit_diagonal=True)
    p = jnp.einsum('nbhtd,nbhsd->nbhts', qc, kc) * decay
    q_dec = jnp.exp(G)[..., None] * qc
    k_dec = jnp.exp(G[..., -1:] - G)[..., None] * kc
    g_last = jnp.exp(G[..., -1])

    def step(s, xs):
        w_v_c, w_k_c, p_c, q_dec_c, k_dec_c, gl = xs
        u = w_v_c - jnp.einsum('bhtk,bhvk->bhtv', w_k_c, s)
        o = jnp.einsum('bhtk,bhvk->bhtv', q_dec_c, s) + jnp.einsum('bhts,bhsv->bhtv', p_c, u)
        s = gl[..., None, None] * s + jnp.einsum('bhtv,bhtk->bhvk', u, k_dec_c)
        return s, o

    s_final, o = lax.scan(step, s0, (w_v, w_k, p, q_dec, k_dec, g_last))
    o = jnp.moveaxis(o.swapaxes(0, 1), 2, 3).reshape(bsz, n * cs, h, dv)[:, :length]
    return o, s_final


def mixer_ab(h, w_in, b_f, conv_w, w_out, past_kvf, conv_buf):
    bsz, length, _ = h.shape
    proj = h @ w_in
    splits = [A_WIDTH, 2 * A_WIDTH, 3 * A_WIDTH, 3 * A_WIDTH + A_HEADS,
              3 * A_WIDTH + A_HEADS + B_WIDTH, 3 * A_WIDTH + A_HEADS + 2 * B_WIDTH]
    q, k, v, f_pre, gate_b, gate_c, u = jnp.split(proj, splits, axis=-1)
    q = q.reshape(bsz, length, A_HEADS, A_HEAD_DIM)
    k = k.reshape(bsz, length, A_HEADS, A_HEAD_DIM)
    v = v.reshape(bsz, length, A_HEADS, A_HEAD_DIM)
    logf = jax.nn.log_sigmoid(f_pre.astype(jnp.float32) + b_f.astype(jnp.float32))
    if past_kvf is None:
        k_all, v_all, logf_all, q_offset = k, v, logf, 0
    else:
        k_past, v_past, logf_past = past_kvf
        k_all = jnp.concatenate([k_past, k], axis=1)
        v_all = jnp.concatenate([v_past, v], axis=1)
        logf_all = jnp.concatenate([logf_past.astype(jnp.float32), logf], axis=1)
        q_offset = k_past.shape[1]
    o_a = fox_attention(q, k_all, v_all, logf_all, q_offset).reshape(bsz, length, A_WIDTH)
    y_conv, new_buf = causal_depthwise_conv(gate_c * u, conv_buf, conv_w)
    o_b = gate_b * y_conv
    out = jnp.concatenate([o_a, o_b.astype(o_a.dtype)], axis=-1) @ w_out
    return out, (k, v, logf, new_buf)


def mixer_c(h, w_in, conv_w, a_log, dt_bias, norm_w, w_out, conv_buf, s0):
    bsz, length, _ = h.shape
    proj = h @ w_in
    qkv, z, b_pre, a_pre = jnp.split(proj, [C_CONV_WIDTH, C_CONV_WIDTH + C_V_WIDTH, C_CONV_WIDTH + C_V_WIDTH + C_HEADS], axis=-1)
    qkv_c, new_buf = causal_depthwise_conv(qkv, conv_buf, conv_w)
    qkv_c = jax.nn.silu(qkv_c)
    q, k, v = jnp.split(qkv_c, [C_QK_WIDTH, 2 * C_QK_WIDTH], axis=-1)
    q = l2norm(q.reshape(bsz, length, C_HEADS, C_KEY_DIM)) * (C_KEY_DIM ** -0.5)
    k = l2norm(k.reshape(bsz, length, C_HEADS, C_KEY_DIM))
    v = v.reshape(bsz, length, C_HEADS, C_VAL_DIM).astype(jnp.float32)
    beta = jax.nn.sigmoid(b_pre.astype(jnp.float32))
    g = -jnp.exp(a_log.astype(jnp.float32)) * jax.nn.softplus(a_pre.astype(jnp.float32) + dt_bias.astype(jnp.float32))
    o, s_new = gated_delta_rule(q, k, v, g, beta, s0.astype(jnp.float32))
    o = rmsnorm(o, norm_w) * jax.nn.silu(z.reshape(bsz, length, C_HEADS, C_VAL_DIM).astype(jnp.float32))
    out = o.reshape(bsz, length, C_V_WIDTH).astype(h.dtype) @ w_out
    return out, (new_buf, s_new)


def run_trunk(x, past, norm_g, ffn_w_gate, ffn_w_up, ffn_w_down,
              ab_w_in, ab_b_f, ab_conv_w, ab_w_out,
              c_w_in, c_conv_w, c_a_log, c_dt_bias, c_norm_w, c_w_out):
    bsz = x.shape[0]
    ab_states, c_states = [], []
    for layer in range(DEPTH):
        gains = norm_g[layer]
        hdn = rmsnorm(x, gains[0])
        x = x + 0.5 * rmsnorm(swiglu(hdn, ffn_w_gate[layer, 0], ffn_w_up[layer, 0], ffn_w_down[layer, 0]), gains[1])
        hdn = rmsnorm(x, gains[2])
        i = layer // 2
        if layer % 2 == 0:
            if past is None:
                kvf = None
                buf = jnp.zeros((bsz, B_CONV - 1, B_WIDTH), x.dtype)
            else:
                cache_a_k, cache_a_v, cache_a_logf, page_table, state_b_conv, _, _ = past
                kvf = (gather_pages(cache_a_k, i, page_table),
                       gather_pages(cache_a_v, i, page_table),
                       gather_pages(cache_a_logf, i, page_table))
                buf = state_b_conv[i]
            mix, st = mixer_ab(hdn, ab_w_in[i], ab_b_f[i], ab_conv_w[i], ab_w_out[i], kvf, buf)
            ab_states.append(st)
        else:
            if past is None:
                buf = jnp.zeros((bsz, C_CONV - 1, C_CONV_WIDTH), x.dtype)
                s0 = jnp.zeros((bsz, C_HEADS, C_VAL_DIM, C_KEY_DIM), jnp.float32)
            else:
                _, _, _, _, _, state_c_conv, state_c_S = past
                buf = state_c_conv[i]
                s0 = state_c_S[i]
            mix, st = mixer_c(hdn, c_w_in[i], c_conv_w[i], c_a_log[i], c_dt_bias[i], c_norm_w[i], c_w_out[i], buf, s0)
            c_states.append(st)
        x = x + rmsnorm(mix, gains[3])
        hdn = rmsnorm(x, gains[4])
        x = x + 0.5 * rmsnorm(swiglu(hdn, ffn_w_gate[layer, 1], ffn_w_up[layer, 1], ffn_w_down[layer, 1]), gains[5])
    ab_new = [jnp.stack(t) for t in zip(*ab_states)]
    c_new = [jnp.stack(t) for t in zip(*c_states)]
    return x, ab_new, c_new


def setup_inputs(seed: int = 0) -> dict:
    key = jax.random.key(seed)
    ks = jax.random.split(key, 24)
    f32 = jnp.float32
    n_pages = PAST_LEN // PAGE_SIZE
    n_used = DEC_BATCH * n_pages
    n_phys = n_used + max(1, n_used // 4)

    def nrm(k, shape, scale):
        return jax.random.normal(k, shape, f32) * scale

    logf_shape = (N_AB_LAYERS, n_phys, PAGE_SIZE, A_HEADS)
    dt = jnp.exp(jax.random.uniform(ks[21], (N_C_LAYERS, C_HEADS), f32, float(np.log(DT_MIN)), float(np.log(DT_MAX))))
    return {
        'x_prompt': nrm(ks[0], (BATCH, SEQ, D_MODEL), 1.0),
        'x_sample': nrm(ks[1], (DEC_BATCH, DEC_SEQ, D_MODEL), 1.0),
        'cache_a_k': nrm(ks[2], (N_AB_LAYERS, n_phys, PAGE_SIZE, A_HEADS, A_HEAD_DIM), 1.0),
        'cache_a_v': nrm(ks[3], (N_AB_LAYERS, n_phys, PAGE_SIZE, A_HEADS, A_HEAD_DIM), 1.0),
        'cache_a_logf': jax.nn.log_sigmoid(jax.random.uniform(ks[4], logf_shape, f32, 1.0, 5.0) + nrm(ks[5], logf_shape, 0.5)),
        'page_table': jax.random.permutation(ks[6], n_phys)[:n_used].reshape(DEC_BATCH, n_pages).astype(jnp.int32),
        'state_b_conv': nrm(ks[7], (N_AB_LAYERS, DEC_BATCH, B_CONV - 1, B_WIDTH), 1.0),
        'state_c_conv': nrm(ks[8], (N_C_LAYERS, DEC_BATCH, C_CONV - 1, C_CONV_WIDTH), 1.0),
        'state_c_S': nrm(ks[9], (N_C_LAYERS, DEC_BATCH, C_HEADS, C_VAL_DIM, C_KEY_DIM), 0.1),
        'norm_g': 1.0 + nrm(ks[10], (DEPTH, 6, D_MODEL), 0.05),
        'ffn_w_gate': nrm(ks[11], (DEPTH, 2, D_MODEL, D_FF), D_MODEL ** -0.5),
        'ffn_w_up': nrm(ks[12], (DEPTH, 2, D_MODEL, D_FF), D_MODEL ** -0.5),
        'ffn_w_down': nrm(ks[13], (DEPTH, 2, D_FF, D_MODEL), D_FF ** -0.5),
        'ab_w_in': nrm(ks[14], (N_AB_LAYERS, D_MODEL, AB_IN_WIDTH), D_MODEL ** -0.5),
        'ab_b_f': jax.random.uniform(ks[15], (N_AB_LAYERS, A_HEADS), f32, 1.0, 5.0),
        'ab_conv_w': nrm(ks[16], (N_AB_LAYERS, B_CONV, B_WIDTH), B_CONV ** -0.5),
        'ab_w_out': nrm(ks[17], (N_AB_LAYERS, AB_OUT_WIDTH, D_MODEL), AB_OUT_WIDTH ** -0.5),
        'c_w_in': nrm(ks[18], (N_C_LAYERS, D_MODEL, C_IN_WIDTH), D_MODEL ** -0.5),
        'c_conv_w': nrm(ks[19], (N_C_LAYERS, C_CONV, C_CONV_WIDTH), C_CONV ** -0.5),
        'c_a_log': jnp.log(jax.random.uniform(ks[20], (N_C_LAYERS, C_HEADS), f32, 1.0, 16.0)),
        'c_dt_bias': dt + jnp.log(-jnp.expm1(-dt)),
        'c_norm_w': 1.0 + nrm(ks[22], (N_C_LAYERS, C_VAL_DIM), 0.05),
        'c_w_out': nrm(ks[23], (N_C_LAYERS, C_V_WIDTH, D_MODEL), C_V_WIDTH ** -0.5),
    }


def reference(x_prompt, x_sample, cache_a_k, cache_a_v, cache_a_logf, page_table,
              state_b_conv, state_c_conv, state_c_S,
              norm_g, ffn_w_gate, ffn_w_up, ffn_w_down,
              ab_w_in, ab_b_f, ab_conv_w, ab_w_out,
              c_w_in, c_conv_w, c_a_log, c_dt_bias, c_norm_w, c_w_out):
    y_prompt, ab_p, c_p = run_trunk(x_prompt, None, norm_g, ffn_w_gate, ffn_w_up, ffn_w_down,
                                    ab_w_in, ab_b_f, ab_conv_w, ab_w_out,
                                    c_w_in, c_conv_w, c_a_log, c_dt_bias, c_norm_w, c_w_out)
    past = (cache_a_k, cache_a_v, cache_a_logf, page_table, state_b_conv, state_c_conv, state_c_S)
    y_sample, ab_s, c_s = run_trunk(x_sample, past, norm_g, ffn_w_gate, ffn_w_up, ffn_w_down,
                                    ab_w_in, ab_b_f, ab_conv_w, ab_w_out,
                                    c_w_in, c_conv_w, c_a_log, c_dt_bias, c_norm_w, c_w_out)
    a_k_prompt, a_v_prompt, a_logf_prompt, b_conv_prompt = ab_p
    a_k_sample, a_v_sample, a_logf_sample, b_conv_sample = ab_s
    c_conv_prompt, c_S_prompt = c_p
    c_conv_sample, c_S_sample = c_s
    return (y_prompt, y_sample,
            a_k_prompt, a_v_prompt, a_logf_prompt,
            a_k_sample, a_v_sample, a_logf_sample,
            b_conv_prompt, b_conv_sample,
            c_conv_prompt, c_conv_sample,
            c_S_prompt, c_S_sample)
```

```python
import functools

import jax
import jax.numpy as jnp
from jax import lax
from jax.experimental import pallas as pl
from jax.experimental.pallas import tpu as pltpu

F32 = jnp.float32
BF16 = jnp.bfloat16
NORM_EPS = 1e-6

LANES = 128
SUBLANES = 8
VMEM_LIMIT_BYTES = 56 * 1024 * 1024
DELTA_CHUNK = 64
PAGES_PER_STEP = 8


def _params(*semantics):
    return pltpu.CompilerParams(dimension_semantics=semantics, vmem_limit_bytes=VMEM_LIMIT_BYTES)


def _resident(shape, index_map):
    return pl.BlockSpec(shape, index_map, pipeline_mode=pl.Buffered(1))


def _const_spec(a):
    return _resident(a.shape, lambda *_: (0,) * a.ndim)


def _rms(x, g):
    return x * lax.rsqrt(jnp.mean(x * x, axis=-1, keepdims=True) + NORM_EPS) * g


def _softplus(x):
    return jnp.maximum(x, 0.0) + jnp.log1p(jnp.exp(-jnp.abs(x)))


def _silu(x):
    return x * jax.nn.sigmoid(x)


def _dot(a, b):
    return jnp.dot(a, b, preferred_element_type=F32)


def _dot_nt(a, b):
    return lax.dot_general(a, b, (((1,), (1,)), ((), ())), preferred_element_type=F32)


def _dot_tn(a, b):
    return lax.dot_general(a, b, (((0,), (0,)), ((), ())), preferred_element_type=F32)


def _dot_f32(a, b):
    return jnp.dot(a, b, preferred_element_type=F32, precision=lax.Precision.HIGHEST)


def _ffn_kernel(x_ref, gpre_ref, gpost_ref, wg_ref, wu_ref, wd_ref, o_ref, *, ff_chunk):
    x = x_ref[...]
    h = _rms(x, gpre_ref[...]).astype(BF16)
    d_ff = wg_ref.shape[1]
    y = None
    for c0 in range(0, d_ff, ff_chunk):
        gate = _dot(h, wg_ref[:, c0:c0 + ff_chunk])
        up = _dot(h, wu_ref[:, c0:c0 + ff_chunk])
        act = (_silu(gate) * up).astype(BF16)
        part = _dot(act, wd_ref[c0:c0 + ff_chunk, :])
        y = part if y is None else y + part
    o_ref[...] = x + 0.5 * _rms(y, gpost_ref[...])


def _ffn_chunk(d_ff):
    half = d_ff // 2
    return half if d_ff % 2 == 0 and half % LANES == 0 else d_ff


def _ffn_block(x, g_pre, g_post, wg_all, wu_all, wd_all, layer, j, *, tm):
    t, d = x.shape
    d_ff = wg_all.shape[-1]
    row = pl.BlockSpec((tm, d), lambda i: (i, 0))
    gain = _resident((1, d), lambda i: (0, 0))
    w_in = _resident((None, None, d, d_ff), lambda i: (layer, j, 0, 0))
    w_out = _resident((None, None, d_ff, d), lambda i: (layer, j, 0, 0))
    return pl.pallas_call(
        functools.partial(_ffn_kernel, ff_chunk=_ffn_chunk(d_ff)),
        grid=(t // tm,),
        in_specs=[row, gain, gain, w_in, w_in, w_out],
        out_specs=row,
        out_shape=jax.ShapeDtypeStruct((t, d), F32),
        compiler_params=_params("arbitrary"),
        name="ffn_block",
    )(x, g_pre, g_post, wg_all, wu_all, wd_all)


def _causal_conv(u, taps_ref, ext_ref, overrides, seq_len):
    tm = u.shape[0]
    n_taps = taps_ref.shape[0]
    ext_ref[pl.ds(SUBLANES, tm), :] = u
    out = u * taps_ref[n_taps - 1:n_taps, :]
    if overrides is not None:
        pos = lax.broadcasted_iota(jnp.int32, u.shape, 0) % seq_len
    for j in range(1, n_taps):
        prev = ext_ref[pl.ds(SUBLANES - j, tm), :]
        if overrides is not None:
            prev = jnp.where(pos >= j, prev, overrides[j - 1][...])
        out = out + prev * taps_ref[n_taps - 1 - j:n_taps - j, :]
    return out


def _conv_overrides(state, seq_len):
    b, wm1, c = state.shape
    assert seq_len >= wm1
    outs = []
    for j in range(1, wm1 + 1):
        rows = jnp.pad(state[:, wm1 - j:, :], ((0, 0), (0, seq_len - j), (0, 0)))
        outs.append(rows.reshape(b * seq_len, c))
    return outs


def _conv_prologue(ext_ref, first, width):
    @pl.when(first)
    def _():
        ext_ref[pl.ds(0, SUBLANES), :] = jnp.zeros((SUBLANES, width), F32)


def _log_sigmoid(z):
    return -_softplus(-z)


def _ab_proj_prompt_kernel(x_ref, g_ref, wq_ref, wkvt_ref, wft_ref, bf_ref, wb_ref, taps_ref,
                           kt_ref, vt_ref, ktb_ref, vtb_ref, lft_ref, qb_ref, ob_ref, tail_ref, ext_ref,
                           *, head_dim, tiles_per_seq):
    a_width = wq_ref.shape[1]
    b_width = wb_ref.shape[1] // 3
    h = _rms(x_ref[...], g_ref[...]).astype(BF16)
    qb_ref[...] = (_dot(h, wq_ref[...]) * head_dim ** -0.5).astype(BF16)
    kvt = _dot_nt(wkvt_ref[...], h)
    kt = kvt[:a_width]
    vt = kvt[a_width:]
    kt_ref[...] = kt
    vt_ref[...] = vt
    ktb_ref[...] = kt.astype(BF16)
    vtb_ref[...] = vt.astype(BF16)
    lft_ref[...] = _log_sigmoid(_dot_nt(wft_ref[...], h) + bf_ref[...])

    gbu = _dot(h, wb_ref[...])
    gate_b = gbu[:, :b_width]
    gcu = gbu[:, b_width:2 * b_width] * gbu[:, 2 * b_width:]
    _conv_prologue(ext_ref, pl.program_id(0) % tiles_per_seq == 0, b_width)
    y = _causal_conv(gcu, taps_ref, ext_ref, None, None)
    ob_ref[...] = (gate_b * y).astype(BF16)
    tail = gcu[gcu.shape[0] - SUBLANES:, :]
    tail_ref[...] = tail
    ext_ref[pl.ds(0, SUBLANES), :] = tail


def _ab_proj_prompt(x, gain, wq, wkvt, wft, bf_col, wb, taps, *, batch, seq_len, head_dim, tm):
    t, d = x.shape
    a_width = wq.shape[1]
    n_heads = wft.shape[0]
    b_width = wb.shape[1] // 3
    tiles_per_seq = seq_len // tm
    row = lambda c: pl.BlockSpec((tm, c), lambda i: (i, 0))
    feat = lambda c: pl.BlockSpec((None, c, tm), lambda i: (i // tiles_per_seq, 0, i % tiles_per_seq))
    feat_shape = lambda c, dt: jax.ShapeDtypeStruct((batch, c, seq_len), dt)
    return pl.pallas_call(
        functools.partial(_ab_proj_prompt_kernel, head_dim=head_dim, tiles_per_seq=tiles_per_seq),
        grid=(t // tm,),
        in_specs=[row(d)] + [_const_spec(a) for a in (gain, wq, wkvt, wft, bf_col, wb, taps)],
        out_specs=[feat(a_width), feat(a_width), feat(a_width), feat(a_width), feat(n_heads),
                   row(a_width), row(b_width),
                   pl.BlockSpec((None, SUBLANES, b_width), lambda i: (i, 0, 0))],
        out_shape=[feat_shape(a_width, F32), feat_shape(a_width, F32), feat_shape(a_width, BF16),
                   feat_shape(a_width, BF16), feat_shape(n_heads, F32),
                   jax.ShapeDtypeStruct((t, a_width), BF16), jax.ShapeDtypeStruct((t, b_width), BF16),
                   jax.ShapeDtypeStruct((t // tm, SUBLANES, b_width), F32)],
        scratch_shapes=[pltpu.VMEM((SUBLANES + tm, b_width), F32)],
        compiler_params=_params("arbitrary"),
        name="ab_proj_prompt",
    )(x, gain, wq, wkvt, wft, bf_col, wb, taps)


def _ab_proj_sample_kernel(*refs, n_heads, head_dim, seq_len, n_over):
    x_ref, g_ref, wqkv_ref, wf_ref, bf_ref, wb_ref, taps_ref = refs[:7]
    over = list(refs[7:7 + n_over])
    q_ref, k_ref, v_ref, logf_ref, ob_ref, u_ref, ext_ref = refs[7 + n_over:]
    a_width = n_heads * head_dim
    b_width = wb_ref.shape[1] // 3
    h = _rms(x_ref[...], g_ref[...]).astype(BF16)
    qkv = _dot(h, wqkv_ref[...])
    q_ref[...] = qkv[:, :a_width] * head_dim ** -0.5
    k_ref[...] = qkv[:, a_width:2 * a_width]
    v_ref[...] = qkv[:, 2 * a_width:]
    logf_ref[...] = _log_sigmoid(_dot(h, wf_ref[...]) + bf_ref[...])[:, :n_heads]

    gbu = _dot(h, wb_ref[...])
    gate_b = gbu[:, :b_width]
    gcu = gbu[:, b_width:2 * b_width] * gbu[:, 2 * b_width:]
    _conv_prologue(ext_ref, pl.program_id(0) == 0, b_width)
    y = _causal_conv(gcu, taps_ref, ext_ref, over, seq_len)
    ob_ref[...] = (gate_b * y).astype(BF16)
    u_ref[...] = gcu


def _ab_proj_sample(x, gain, wqkv, wf, bf, wb, taps, overrides, *, n_heads, head_dim, seq_len):
    t, d = x.shape
    a_width = n_heads * head_dim
    b_width = wb.shape[1] // 3
    full = lambda c: pl.BlockSpec((t, c), lambda i: (0, 0))
    f32 = lambda c: jax.ShapeDtypeStruct((t, c), F32)
    return pl.pallas_call(
        functools.partial(_ab_proj_sample_kernel, n_heads=n_heads, head_dim=head_dim, seq_len=seq_len,
                          n_over=len(overrides)),
        grid=(1,),
        in_specs=[full(d)] + [_const_spec(a) for a in (gain, wqkv, wf, bf, wb, taps)]
        + [full(b_width)] * len(overrides),
        out_specs=[full(a_width), full(a_width), full(a_width), full(n_heads), full(b_width), full(b_width)],
        out_shape=[f32(a_width), f32(a_width), f32(a_width), f32(n_heads),
                   jax.ShapeDtypeStruct((t, b_width), BF16), f32(b_width)],
        scratch_shapes=[pltpu.VMEM((SUBLANES + t, b_width), F32)],
        compiler_params=_params("arbitrary"),
        name="ab_proj_sample",
    )(x, gain, wqkv, wf, bf, wb, taps, *overrides)


def _scan_kernel(x_ref, o_ref, *, seg):
    x = x_ref[...]
    pos = lax.broadcasted_iota(jnp.int32, x.shape, 1) % seg
    d = 1
    while d < seg:
        x = x + jnp.where(pos >= d, pltpu.roll(x, d, axis=1), 0.0)
        d *= 2
    o_ref[...] = x


def _segment_cumsum(x, seg):
    b, h, length = x.shape
    spec = pl.BlockSpec((None, h, length), lambda i: (i, 0, 0))
    return pl.pallas_call(
        functools.partial(_scan_kernel, seg=seg),
        grid=(b,),
        in_specs=[spec],
        out_specs=spec,
        out_shape=jax.ShapeDtypeStruct(x.shape, F32),
        compiler_params=_params("arbitrary"),
        name="segment_cumsum",
    )(x)


def _fox_kernel(q_ref, kt_ref, vt_ref, ct_ref, o_ref, *, tile, head_dim):
    hp = pl.program_id(1)
    qi = pl.program_id(2)
    heads = q_ref.shape[1] // head_dim
    row = lax.broadcasted_iota(jnp.int32, (tile, tile), 0)
    col = lax.broadcasted_iota(jnp.int32, (tile, tile), 1)
    outs = []
    for j in range(heads):
        head = hp * heads + j
        feats = slice(j * head_dim, (j + 1) * head_dim)
        q = q_ref[:, feats]

        def block(ki, carry, masked, feats=feats, head=head, q=q):
            m, l, acc = carry
            off = pl.multiple_of(ki * tile, tile)
            s = _dot(q, kt_ref[feats, pl.ds(off, tile)])
            s = s - ct_ref[pl.ds(head, 1), pl.ds(off, tile)]
            if masked:
                s = jnp.where(col <= row, s, -jnp.inf)
            m_new = jnp.maximum(m, jnp.max(s, axis=1, keepdims=True))
            alpha = jnp.exp(m - m_new)
            p = jnp.exp(s - m_new)
            l = alpha * l + jnp.sum(p, axis=1, keepdims=True)
            acc = alpha * acc + _dot_nt(p.astype(BF16), vt_ref[feats, pl.ds(off, tile)])
            return m_new, l, acc

        init = (jnp.full((tile, 1), -jnp.inf, F32), jnp.zeros((tile, 1), F32),
                jnp.zeros((tile, head_dim), F32))
        carry = lax.fori_loop(0, qi, functools.partial(block, masked=False), init)
        _, l, acc = block(qi, carry, True)
        outs.append(acc / l)
    o_ref[...] = jnp.concatenate(outs, axis=1).astype(BF16)


def _fox_attention(qb, ktb, vtb, ct, *, head_dim, tile):
    t, a_width = qb.shape
    batch, n_heads, seq_len = ct.shape
    nq = seq_len // tile
    q_spec = pl.BlockSpec((tile, LANES), lambda b, hp, qi: (b * nq + qi, hp))
    kv_spec = pl.BlockSpec((None, LANES, seq_len), lambda b, hp, qi: (b, hp, 0))
    ct_spec = pl.BlockSpec((None, n_heads, seq_len), lambda b, hp, qi: (b, 0, 0))
    return pl.pallas_call(
        functools.partial(_fox_kernel, tile=tile, head_dim=head_dim),
        grid=(batch, a_width // LANES, nq),
        in_specs=[q_spec, kv_spec, kv_spec, ct_spec],
        out_specs=q_spec,
        out_shape=jax.ShapeDtypeStruct((t, a_width), BF16),
        compiler_params=_params("arbitrary", "arbitrary", "arbitrary"),
        name="fox_attention",
    )(qb, ktb, vtb, ct)


def _split3(x):
    hi = x.astype(BF16)
    r = x - hi.astype(F32)
    mid = r.astype(BF16)
    lo = (r - mid.astype(F32)).astype(BF16)
    return hi, mid, lo


def _paged_attn_kernel(pt_ref, q_ref, knew_ref, vnew_ref, lfnew_ref, tri_ref, *refs,
                       n_pages, n_heads, head_dim):
    del pt_ref
    k_refs = refs[:n_pages]
    v_refs = refs[n_pages:2 * n_pages]
    lf_refs = refs[2 * n_pages:3 * n_pages]
    o_ref, qbd_ref, m_ref, l_ref, acc_ref, pref_ref = refs[3 * n_pages:]
    step = pl.program_id(1)
    nq, width = q_ref.shape
    rows = nq * n_heads
    page = k_refs[0].shape[1]
    head_of_lane = lax.broadcasted_iota(jnp.int32, (n_heads, width), 1) // head_dim
    own = head_of_lane == lax.broadcasted_iota(jnp.int32, (n_heads, width), 0)

    @pl.when(step == 0)
    def _():
        q = q_ref[...]
        for qq in range(nq):
            blk = jnp.where(own, jnp.broadcast_to(q[qq:qq + 1, :], (n_heads, width)), 0.0)
            qbd_ref[pl.ds(qq * n_heads, n_heads), :] = blk
        m_ref[...] = jnp.full(m_ref.shape, -jnp.inf, F32)
        l_ref[...] = jnp.zeros(l_ref.shape, F32)
        acc_ref[...] = jnp.zeros(acc_ref.shape, F32)
        pref_ref[...] = jnp.zeros(pref_ref.shape, F32)

    def attend(k_list, v_list, lf_list, causal):
        n = len(k_list)
        ys = [lf[...] for lf in lf_list]
        y = jnp.concatenate(ys, axis=0) if n > 1 else ys[0]
        pieces = jnp.concatenate(_split3(y), axis=0)
        cw = _dot(pieces, tri_ref[...])
        nh = n * n_heads
        cw = cw[:nh] + cw[nh:2 * nh] + cw[2 * nh:]
        qbd = qbd_ref[...].astype(BF16)
        pref = pref_ref[...]
        s_list = []
        for g in range(n):
            cg = cw[g * n_heads:(g + 1) * n_heads]
            c = cg + pref
            pref = pref + cg[:, page - 1:page]
            s = _dot(qbd, k_list[g][...].astype(BF16))
            s = s - jnp.concatenate([c] * nq, axis=0)
            if causal:
                key = lax.broadcasted_iota(jnp.int32, (rows, page), 1)
                qpos = lax.broadcasted_iota(jnp.int32, (rows, page), 0) // n_heads
                s = jnp.where(key <= qpos, s, -jnp.inf)
            s_list.append(s)
        pref_ref[...] = pref
        s_all = jnp.concatenate(s_list, axis=1) if n > 1 else s_list[0]
        m = m_ref[...]
        m_new = jnp.maximum(m, jnp.max(s_all, axis=1, keepdims=True))
        alpha = jnp.exp(m - m_new)
        p = jnp.exp(s_all - m_new)
        l_ref[...] = alpha * l_ref[...] + jnp.sum(p, axis=1, keepdims=True)
        pv = None
        for g in range(n):
            part = _dot_nt(p[:, g * page:(g + 1) * page].astype(BF16), v_list[g][...].astype(BF16))
            pv = part if pv is None else pv + part
        acc_ref[...] = alpha * acc_ref[...] + pv
        m_ref[...] = m_new

    attend(k_refs, v_refs, lf_refs, causal=False)

    @pl.when(step == pl.num_programs(1) - 1)
    def _():
        attend([knew_ref], [vnew_ref], [lfnew_ref], causal=True)
        o = acc_ref[...] / l_ref[...]
        out_rows = []
        for qq in range(nq):
            blk = jnp.where(own, o[qq * n_heads:(qq + 1) * n_heads, :], 0.0)
            out_rows.append(jnp.sum(blk, axis=0, keepdims=True))
        o_ref[...] = jnp.concatenate(out_rows, axis=0).astype(o_ref.dtype)


def _paged_attention(q, kt_new, vt_new, lf_new, kt_pool, vt_pool, lf_pool, page_table, *, head_dim):
    batch, nq, width = q.shape
    page = kt_pool.shape[2]
    n_heads = lf_pool.shape[1]
    n_pages = PAGES_PER_STEP
    steps = page_table.shape[1] // n_pages
    rows = nq * n_heads
    tri = (jnp.arange(page)[:, None] <= jnp.arange(page)[None, :]).astype(BF16)

    per_batch = lambda shape: pl.BlockSpec((None,) + shape, lambda b, s, pt: (b, 0, 0))

    def paged(shape, g):
        return pl.BlockSpec((None,) + shape, lambda b, s, pt: (pt[b, s * n_pages + g], 0, 0))

    in_specs = [per_batch((nq, width)), per_batch((width, page)), per_batch((width, page)),
                per_batch((n_heads, page)), pl.BlockSpec(tri.shape, lambda b, s, pt: (0, 0))]
    in_specs += [paged((width, page), g) for g in range(n_pages)]
    in_specs += [paged((width, page), g) for g in range(n_pages)]
    in_specs += [paged((n_heads, page), g) for g in range(n_pages)]
    grid_spec = pltpu.PrefetchScalarGridSpec(
        num_scalar_prefetch=1,
        grid=(batch, steps),
        in_specs=in_specs,
        out_specs=per_batch((nq, width)),
        scratch_shapes=[pltpu.VMEM((rows, width), F32), pltpu.VMEM((rows, 1), F32),
                        pltpu.VMEM((rows, 1), F32), pltpu.VMEM((rows, width), F32),
                        pltpu.VMEM((n_heads, 1), F32)],
    )
    return pl.pallas_call(
        functools.partial(_paged_attn_kernel, n_pages=n_pages, n_heads=n_heads, head_dim=head_dim),
        grid_spec=grid_spec,
        out_shape=jax.ShapeDtypeStruct((batch, nq, width), BF16),
        compiler_params=_params("arbitrary", "arbitrary"),
        name="paged_attention",
    )(page_table, q, kt_new, vt_new, lf_new, tri,
      *([kt_pool] * n_pages), *([vt_pool] * n_pages), *([lf_pool] * n_pages))


def _out_proj_kernel(*refs, n_parts):
    parts = refs[:n_parts]
    w_ref, x_ref, g_ref, o_ref = refs[n_parts:]
    mix = None
    k0 = 0
    for a_ref in parts:
        kw = a_ref.shape[1]
        part = _dot(a_ref[...], w_ref[k0:k0 + kw, :])
        mix = part if mix is None else mix + part
        k0 += kw
    o_ref[...] = x_ref[...] + _rms(mix, g_ref[...])


def _out_proj(parts, w, x, gain, *, tm):
    t, d = x.shape
    row = lambda c: pl.BlockSpec((tm, c), lambda i: (i, 0))
    return pl.pallas_call(
        functools.partial(_out_proj_kernel, n_parts=len(parts)),
        grid=(t // tm,),
        in_specs=[row(p.shape[1]) for p in parts] + [_const_spec(w), row(d), _const_spec(gain)],
        out_specs=row(d),
        out_shape=jax.ShapeDtypeStruct((t, d), F32),
        compiler_params=_params("arbitrary"),
        name="out_proj",
    )(*parts, w, x, gain)


def _c_proj_kernel(*refs, n_heads, key_dim, seq_len, tiles_per_seq, n_over):
    x_ref, g_ref, wqkv_ref, wz_ref, wba_ref, taps_ref, alog_ref, dtb_ref = refs[:8]
    over = list(refs[8:8 + n_over]) if n_over else None
    (q_ref, k_ref, v_ref, z_ref, beta_ref, gdec_ref, u_ref, ext_ref) = refs[8 + n_over:]
    qk_width = n_heads * key_dim
    conv_width = wqkv_ref.shape[1]

    h = _rms(x_ref[...], g_ref[...]).astype(BF16)
    qkv = _dot(h, wqkv_ref[...])
    z_ref[...] = _dot(h, wz_ref[...]).astype(BF16)
    ba = _dot(h, wba_ref[...])
    beta_ref[...] = jax.nn.sigmoid(ba)[:, :n_heads]
    gdec = -jnp.exp(alog_ref[...]) * _softplus(ba + dtb_ref[...])
    gdec_ref[...] = gdec[:, n_heads:2 * n_heads]

    first = pl.program_id(0) % tiles_per_seq == 0 if over is None else pl.program_id(0) == 0
    _conv_prologue(ext_ref, first, conv_width)
    y = _silu(_causal_conv(qkv, taps_ref, ext_ref, over, seq_len))
    tm = qkv.shape[0]
    if over is None:
        tail = qkv[tm - SUBLANES:, :]
        u_ref[...] = tail
        ext_ref[pl.ds(0, SUBLANES), :] = tail
    else:
        u_ref[...] = qkv

    for hh in range(n_heads):
        lanes = slice(hh * key_dim, (hh + 1) * key_dim)
        qh = y[:, hh * key_dim:(hh + 1) * key_dim]
        kh = y[:, qk_width + hh * key_dim:qk_width + (hh + 1) * key_dim]
        qn = qh * lax.rsqrt(jnp.sum(qh * qh, axis=-1, keepdims=True) + NORM_EPS) * key_dim ** -0.5
        kn = kh * lax.rsqrt(jnp.sum(kh * kh, axis=-1, keepdims=True) + NORM_EPS)
        q_ref[:, lanes] = qn.astype(BF16)
        k_ref[:, lanes] = kn.astype(BF16)
    v_ref[...] = y[:, 2 * qk_width:].astype(BF16)


def _c_proj(x, gain, wqkv, wz, wba, taps, alog, dtb, overrides, *, n_heads, key_dim, seq_len, tm):
    t, d = x.shape
    conv_width = wqkv.shape[1]
    qk_width = n_heads * key_dim
    v_width = conv_width - 2 * qk_width
    n_tiles = t // tm
    n_over = 0 if overrides is None else len(overrides)
    row = lambda c: pl.BlockSpec((tm, c), lambda i: (i, 0))
    in_specs = [row(d)] + [_const_spec(a) for a in (gain, wqkv, wz, wba, taps, alog, dtb)]
    in_specs += [row(conv_width)] * n_over
    if overrides is None:
        u_spec = pl.BlockSpec((None, SUBLANES, conv_width), lambda i: (i, 0, 0))
        u_shape = jax.ShapeDtypeStruct((n_tiles, SUBLANES, conv_width), F32)
    else:
        u_spec = row(conv_width)
        u_shape = jax.ShapeDtypeStruct((t, conv_width), F32)
    out_specs = [row(qk_width), row(qk_width), row(v_width), row(v_width), row(n_heads), row(n_heads), u_spec]
    out_shape = [jax.ShapeDtypeStruct((t, qk_width), BF16), jax.ShapeDtypeStruct((t, qk_width), BF16),
                 jax.ShapeDtypeStruct((t, v_width), BF16), jax.ShapeDtypeStruct((t, v_width), BF16),
                 jax.ShapeDtypeStruct((t, n_heads), F32), jax.ShapeDtypeStruct((t, n_heads), F32), u_shape]
    return pl.pallas_call(
        functools.partial(_c_proj_kernel, n_heads=n_heads, key_dim=key_dim, seq_len=seq_len,
                          tiles_per_seq=max(seq_len // tm, 1), n_over=n_over),
        grid=(n_tiles,),
        in_specs=in_specs,
        out_specs=out_specs,
        out_shape=out_shape,
        scratch_shapes=[pltpu.VMEM((SUBLANES + tm, conv_width), F32)],
        compiler_params=_params("arbitrary"),
        name="c_proj",
    )(x, gain, wqkv, wz, wba, taps, alog, dtb, *(overrides or []))


def _delta_kernel(q_ref, k_ref, v_ref, z_ref, gcol_ref, beta_ref, grow_ref, s0_ref, nw_ref,
                  o_ref, s_out_ref, m_ref, *, n_heads, chunk):
    step = pl.program_id(1)
    rows = q_ref.shape[0]
    dk = q_ref.shape[1] // n_heads
    dv = v_ref.shape[1] // n_heads
    ri = lax.broadcasted_iota(jnp.int32, (chunk, chunk), 0)
    ci = lax.broadcasted_iota(jnp.int32, (chunk, chunk), 1)
    incl = ci <= ri
    strict = ci < ri
    eye = (ci == ri).astype(F32)

    @pl.when(step == 0)
    def _():
        for h in range(n_heads):
            m_ref[h] = s0_ref[h].T

    for h in range(n_heads):
        kl = slice(h * dk, (h + 1) * dk)
        vl = slice(h * dv, (h + 1) * dv)
        for c in range(rows // chunk):
            rs = slice(c * chunk, (c + 1) * chunk)
            qc = q_ref[rs, kl]
            kc = k_ref[rs, kl]
            kf = kc.astype(F32)
            vf = v_ref[rs, vl].astype(F32)
            gc = gcol_ref[rs, h:h + 1]
            gr = grow_ref[h:h + 1, rs]
            bc = beta_ref[rs, h:h + 1]
            eg = jnp.exp(gc)
            decay = jnp.where(incl, jnp.exp(jnp.where(incl, gc - gr, 0.0)), 0.0)
            a = jnp.where(strict, bc * _dot_nt(kc, kc) * decay, 0.0)
            inv = eye - a
            pw = a
            span = 2
            while span < chunk:
                pw = _dot_f32(pw, pw)
                inv = inv + _dot_f32(inv, pw)
                span *= 2
            w = _dot_f32(inv, jnp.concatenate([bc * vf, (bc * eg) * kf], axis=1))
            w_v = w[:, :dv]
            w_k = w[:, dv:]
            p = _dot_nt(qc, kc) * decay
            q_dec = eg * qc.astype(F32)
            g_last = gc[chunk - 1:chunk, :]
            k_dec = jnp.exp(g_last - gc) * kf
            m = m_ref[h]
            r = _dot(jnp.concatenate([w_k, q_dec], axis=0).astype(BF16), m.astype(BF16))
            u = w_v - r[:chunk]
            ub = u.astype(BF16)
            o = r[chunk:] + _dot(p.astype(BF16), ub)
            m_ref[h] = jnp.exp(g_last) * m + _dot_tn(k_dec.astype(BF16), ub)
            on = o * lax.rsqrt(jnp.mean(o * o, axis=-1, keepdims=True) + NORM_EPS) * nw_ref[...]
            o_ref[rs, vl] = (on * _silu(z_ref[rs, vl].astype(F32))).astype(BF16)

    @pl.when(step == pl.num_programs(1) - 1)
    def _():
        for h in range(n_heads):
            s_out_ref[h] = m_ref[h].T


def _delta_rule(q, k, v, z, gcol, beta, grow, s0, norm_w, *, batch, seq_len, n_heads, rows):
    t = q.shape[0]
    steps = seq_len // rows
    dk = q.shape[1] // n_heads
    dv = v.shape[1] // n_heads
    row = lambda c: pl.BlockSpec((rows, c), lambda b, s: (b * steps + s, 0))
    state = pl.BlockSpec((None, n_heads, dv, dk), lambda b, s: (b, 0, 0, 0))
    return pl.pallas_call(
        functools.partial(_delta_kernel, n_heads=n_heads, chunk=DELTA_CHUNK),
        grid=(batch, steps),
        in_specs=[row(q.shape[1]), row(k.shape[1]), row(v.shape[1]), row(z.shape[1]),
                  row(n_heads), row(n_heads),
                  pl.BlockSpec((None, n_heads, rows), lambda b, s: (b, 0, s)),
                  state, pl.BlockSpec(norm_w.shape, lambda b, s: (0, 0))],
        out_specs=[row(v.shape[1]), state],
        out_shape=[jax.ShapeDtypeStruct((t, v.shape[1]), BF16),
                   jax.ShapeDtypeStruct((batch, n_heads, dv, dk), F32)],
        scratch_shapes=[pltpu.VMEM((n_heads, dk, dv), F32)],
        compiler_params=_params("arbitrary", "arbitrary"),
        name="gated_delta_rule",
    )(q, k, v, z, gcol, beta, grow, s0, norm_w)


def _pad_lanes(a):
    return jnp.pad(a, ((0, 0), (0, LANES - a.shape[1])))


def _rows_to_heads(a, batch, seq_len):
    return a.reshape(batch, seq_len, a.shape[1]).transpose(0, 2, 1)


def _heads_to_rows(a):
    b, h, length = a.shape
    return a.transpose(0, 2, 1).reshape(b * length, h)


def _feature_major_pool(pool):
    n_l, n_p, page = pool.shape[:3]
    perm = (0, 1) + tuple(range(3, pool.ndim)) + (2,)
    return pool.transpose(perm).reshape(n_l * n_p, -1, page)


def _mixer_ab(x, past, w, i, gain, *, batch, seq_len, tm):
    t = x.shape[0]
    hd = w["a_head_dim"]
    nh = w["ab_b_f"].shape[1]
    aw = nh * hd
    w_in = w["ab_w_in"][i]
    bw = (w_in.shape[1] - 3 * aw - nh) // 3
    wb = w_in[:, 3 * aw + nh:].astype(BF16)
    wf = w_in[:, 3 * aw:3 * aw + nh]
    taps = w["ab_conv_w"][i]
    n_keep = taps.shape[0] - 1
    if past is None:
        wq = w_in[:, :aw].astype(BF16)
        wkvt = w_in[:, aw:3 * aw].T.astype(BF16)
        kt, vt, ktb, vtb, lft, qb, ob, tails = _ab_proj_prompt(
            x, gain, wq, wkvt, wf.T.astype(BF16), w["ab_b_f"][i][:, None], wb, taps,
            batch=batch, seq_len=seq_len, head_dim=hd, tm=tm)
        ct = _segment_cumsum(lft, seq_len)
        o_a = _fox_attention(qb, ktb, vtb, ct, head_dim=hd, tile=min(512, seq_len))
        tails = tails.reshape(batch, seq_len // tm, SUBLANES, bw)[:, -1]
        new_buf = tails[:, SUBLANES - n_keep:, :]
        to_tokens = lambda a: a.reshape(batch, nh, hd, seq_len).transpose(0, 3, 1, 2)
        state = (to_tokens(kt), to_tokens(vt), lft.transpose(0, 2, 1), new_buf)
    else:
        overrides = _conv_overrides(past["state_b_conv"][i], seq_len)
        q_s, k32, v32, logf, ob, gcu = _ab_proj_sample(
            x, gain, w_in[:, :3 * aw].astype(BF16), _pad_lanes(wf).astype(BF16),
            _pad_lanes(w["ab_b_f"][i][None, :]), wb, taps, overrides,
            n_heads=nh, head_dim=hd, seq_len=seq_len)
        n_phys, page = past["cache_a_k"].shape[1:3]
        pad_keys = lambda a: jnp.pad(a.reshape(batch, seq_len, -1).transpose(0, 2, 1),
                                     ((0, 0), (0, 0), (0, page - seq_len)))
        o_a = _paged_attention(
            q_s.reshape(batch, seq_len, aw), pad_keys(k32), pad_keys(v32), pad_keys(logf),
            _feature_major_pool(past["cache_a_k"]), _feature_major_pool(past["cache_a_v"]),
            _feature_major_pool(past["cache_a_logf"]), past["page_table"] + i * n_phys, head_dim=hd)
        o_a = o_a.reshape(t, aw)
        new_buf = gcu.reshape(batch, seq_len, bw)[:, seq_len - n_keep:, :]
        state = (k32.reshape(batch, seq_len, nh, hd), v32.reshape(batch, seq_len, nh, hd),
                 logf.reshape(batch, seq_len, nh), new_buf)
    return [o_a, ob], w["ab_w_out"][i].astype(BF16), state


def _mixer_c(x, past, w, i, gain, *, batch, seq_len, tm):
    t = x.shape[0]
    nh = w["c_a_log"].shape[1]
    dv = w["c_norm_w"].shape[1]
    w_in = w["c_w_in"][i]
    v_width = nh * dv
    conv_width = w_in.shape[1] - v_width - 2 * nh
    dk = (conv_width - v_width) // 2 // nh
    wqkv = w_in[:, :conv_width].astype(BF16)
    wz = w_in[:, conv_width:conv_width + v_width].astype(BF16)
    wba = _pad_lanes(w_in[:, conv_width + v_width:]).astype(BF16)
    zeros_h = jnp.zeros((1, nh), F32)
    alog = _pad_lanes(jnp.concatenate([zeros_h, w["c_a_log"][i][None, :]], axis=1))
    dtb = _pad_lanes(jnp.concatenate([zeros_h, w["c_dt_bias"][i][None, :]], axis=1))
    taps = w["c_conv_w"][i]
    n_keep = taps.shape[0] - 1
    tm_c = min(tm, 256)
    if past is None:
        overrides = None
        s0 = jnp.zeros((batch, nh, dv, dk), F32)
    else:
        overrides = _conv_overrides(past["state_c_conv"][i], seq_len)
        s0 = past["state_c_S"][i]
    qn, kn, vn, zb, beta, gdec, u_out = _c_proj(
        x, gain, wqkv, wz, wba, taps, alog, dtb, overrides,
        n_heads=nh, key_dim=dk, seq_len=seq_len, tm=tm_c)
    if past is None:
        tails = u_out.reshape(batch, seq_len // tm_c, SUBLANES, conv_width)[:, -1]
        new_buf = tails[:, SUBLANES - n_keep:, :]
        len_pad = seq_len
    else:
        new_buf = u_out.reshape(batch, seq_len, conv_width)[:, seq_len - n_keep:, :]
        len_pad = -(-seq_len // DELTA_CHUNK) * DELTA_CHUNK
        pad3 = lambda a: jnp.pad(a.reshape(batch, seq_len, a.shape[1]),
                                 ((0, 0), (0, len_pad - seq_len), (0, 0))).reshape(batch * len_pad, a.shape[1])
        qn, kn, vn, zb, beta, gdec = [pad3(a) for a in (qn, kn, vn, zb, beta, gdec)]
    scan_len = -(-len_pad // LANES) * LANES
    g_heads = jnp.pad(_rows_to_heads(gdec, batch, len_pad), ((0, 0), (0, 0), (0, scan_len - len_pad)))
    grow = _segment_cumsum(g_heads, DELTA_CHUNK)[:, :, :len_pad]
    og, s_new = _delta_rule(qn, kn, vn, zb, _heads_to_rows(grow), beta, grow, s0, w["c_norm_w"][i][None, :],
                            batch=batch, seq_len=len_pad, n_heads=nh, rows=min(2 * DELTA_CHUNK, len_pad))
    if len_pad != seq_len:
        og = og.reshape(batch, len_pad, v_width)[:, :seq_len].reshape(t, v_width)
    return [og], w["c_w_out"][i].astype(BF16), (new_buf, s_new)


def _run_trunk(x3, past, w, *, tm):
    batch, seq_len, d = x3.shape
    x = x3.reshape(batch * seq_len, d)
    ab_states, c_states = [], []
    ffn = lambda x, g0, g1, layer, j: _ffn_block(
        x, g0, g1, w["ffn_w_gate"], w["ffn_w_up"], w["ffn_w_down"], layer, j, tm=tm)
    for layer in range(w["norm_g"].shape[0]):
        gains = [w["norm_g"][layer, j][None, :] for j in range(6)]
        x = ffn(x, gains[0], gains[1], layer, 0)
        mixer, states = (_mixer_ab, ab_states) if layer % 2 == 0 else (_mixer_c, c_states)
        parts, w_out, state = mixer(x, past, w, layer // 2, gains[2], batch=batch, seq_len=seq_len, tm=tm)
        states.append(state)
        x = _out_proj(parts, w_out, x, gains[3], tm=tm)
        x = ffn(x, gains[4], gains[5], layer, 1)
    ab_new = [jnp.stack(ts) for ts in zip(*ab_states)]
    c_new = [jnp.stack(ts) for ts in zip(*c_states)]
    return x.reshape(batch, seq_len, d), ab_new, c_new


def kernel(x_prompt, x_sample, cache_a_k, cache_a_v, cache_a_logf, page_table, state_b_conv, state_c_conv, state_c_S, norm_g, ffn_w_gate, ffn_w_up, ffn_w_down, ab_w_in, ab_b_f, ab_conv_w, ab_w_out, c_w_in, c_conv_w, c_a_log, c_dt_bias, c_norm_w, c_w_out):
    w = dict(norm_g=norm_g,
             ffn_w_gate=ffn_w_gate.astype(BF16), ffn_w_up=ffn_w_up.astype(BF16), ffn_w_down=ffn_w_down.astype(BF16),
             ab_w_in=ab_w_in, ab_b_f=ab_b_f, ab_conv_w=ab_conv_w, ab_w_out=ab_w_out,
             c_w_in=c_w_in, c_conv_w=c_conv_w, c_a_log=c_a_log, c_dt_bias=c_dt_bias,
             c_norm_w=c_norm_w, c_w_out=c_w_out, a_head_dim=cache_a_k.shape[-1])
    past = dict(cache_a_k=cache_a_k, cache_a_v=cache_a_v, cache_a_logf=cache_a_logf, page_table=page_table,
                state_b_conv=state_b_conv, state_c_conv=state_c_conv, state_c_S=state_c_S)
    prompt_rows = x_prompt.shape[0] * x_prompt.shape[1]
    sample_rows = x_sample.shape[0] * x_sample.shape[1]
    y_prompt, ab_p, c_p = _run_trunk(x_prompt, None, w, tm=min(512, prompt_rows))
    y_sample, ab_s, c_s = _run_trunk(x_sample, past, w, tm=min(512, sample_rows))
    a_k_prompt, a_v_prompt, a_logf_prompt, b_conv_prompt = ab_p
    a_k_sample, a_v_sample, a_logf_sample, b_conv_sample = ab_s
    c_conv_prompt, c_S_prompt = c_p
    c_conv_sample, c_S_sample = c_s
    return (y_prompt, y_sample,
            a_k_prompt, a_v_prompt, a_logf_prompt,
            a_k_sample, a_v_sample, a_logf_sample,
            b_conv_prompt, b_conv_sample,
            c_conv_prompt, c_conv_sample,
            c_S_prompt, c_S_sample)
```

```python
import functools

import jax
import jax.numpy as jnp
from jax import lax
from jax.experimental import pallas as pl
from jax.experimental.pallas import tpu as pltpu

F32 = jnp.float32
BF16 = jnp.bfloat16
NORM_EPS = 1e-6

LANES = 128
SUBLANES = 8
VMEM_LIMIT_BYTES = 56 * 1024 * 1024
DELTA_CHUNK = 64
PAGES_PER_STEP = 16


def _params(*semantics):
    return pltpu.CompilerParams(dimension_semantics=semantics, vmem_limit_bytes=VMEM_LIMIT_BYTES)


def _resident(shape, index_map):
    return pl.BlockSpec(shape, index_map, pipeline_mode=pl.Buffered(1))


def _const_spec(a):
    return _resident(a.shape, lambda *_: (0,) * a.ndim)


def _rms(x, g):
    return x * lax.rsqrt(jnp.mean(x * x, axis=-1, keepdims=True) + NORM_EPS) * g


def _softplus(x):
    return jnp.maximum(x, 0.0) + jnp.log1p(jnp.exp(-jnp.abs(x)))


def _silu(x):
    return x * jax.nn.sigmoid(x)


def _dot(a, b):
    return jnp.dot(a, b, preferred_element_type=F32)


def _dot_nt(a, b):
    return lax.dot_general(a, b, (((1,), (1,)), ((), ())), preferred_element_type=F32)


def _dot_tn(a, b):
    return lax.dot_general(a, b, (((0,), (0,)), ((), ())), preferred_element_type=F32)


def _bdot(a, b):
    return lax.dot_general(a, b, (((2,), (1,)), ((0,), (0,))), preferred_element_type=F32)


def _bdot_nt(a, b):
    return lax.dot_general(a, b, (((2,), (2,)), ((0,), (0,))), preferred_element_type=F32)


def _bdot_tn(a, b):
    return lax.dot_general(a, b, (((1,), (1,)), ((0,), (0,))), preferred_element_type=F32)


def _ffn_kernel(x_ref, gpre_ref, gpost_ref, wg_ref, wu_ref, wd_ref, o_ref, *, ff_chunk):
    x = x_ref[...]
    h = _rms(x, gpre_ref[...]).astype(BF16)
    d_ff = wg_ref.shape[1]
    y = None
    for c0 in range(0, d_ff, ff_chunk):
        gate = _dot(h, wg_ref[:, c0:c0 + ff_chunk])
        up = _dot(h, wu_ref[:, c0:c0 + ff_chunk])
        act = (_silu(gate) * up).astype(BF16)
        part = _dot(act, wd_ref[c0:c0 + ff_chunk, :])
        y = part if y is None else y + part
    o_ref[...] = x + 0.5 * _rms(y, gpost_ref[...])


def _ffn_chunk(d_ff):
    half = d_ff // 2
    return half if d_ff % 2 == 0 and half % LANES == 0 else d_ff


def _ffn_block(x, g_pre, g_post, wg_all, wu_all, wd_all, layer, j, *, tm):
    t, d = x.shape
    d_ff = wg_all.shape[-1]
    row = pl.BlockSpec((tm, d), lambda i: (i, 0))
    gain = _resident((1, d), lambda i: (0, 0))
    w_in = _resident((None, None, d, d_ff), lambda i: (layer, j, 0, 0))
    w_out = _resident((None, None, d_ff, d), lambda i: (layer, j, 0, 0))
    return pl.pallas_call(
        functools.partial(_ffn_kernel, ff_chunk=_ffn_chunk(d_ff)),
        grid=(t // tm,),
        in_specs=[row, gain, gain, w_in, w_in, w_out],
        out_specs=row,
        out_shape=jax.ShapeDtypeStruct((t, d), F32),
        compiler_params=_params("arbitrary"),
        name="ffn_block",
    )(x, g_pre, g_post, wg_all, wu_all, wd_all)


def _causal_conv(u, taps_ref, ext_ref, overrides, seq_len):
    tm = u.shape[0]
    n_taps = taps_ref.shape[0]
    ext_ref[pl.ds(SUBLANES, tm), :] = u
    out = u * taps_ref[n_taps - 1:n_taps, :]
    if overrides is not None:
        pos = lax.broadcasted_iota(jnp.int32, u.shape, 0) % seq_len
    for j in range(1, n_taps):
        prev = ext_ref[pl.ds(SUBLANES - j, tm), :]
        if overrides is not None:
            prev = jnp.where(pos >= j, prev, overrides[j - 1][...])
        out = out + prev * taps_ref[n_taps - 1 - j:n_taps - j, :]
    return out


def _conv_overrides(state, seq_len):
    b, wm1, c = state.shape
    assert seq_len >= wm1
    outs = []
    for j in range(1, wm1 + 1):
        rows = jnp.pad(state[:, wm1 - j:, :], ((0, 0), (0, seq_len - j), (0, 0)))
        outs.append(rows.reshape(b * seq_len, c))
    return outs


def _conv_prologue(ext_ref, first, width):
    @pl.when(first)
    def _():
        ext_ref[pl.ds(0, SUBLANES), :] = jnp.zeros((SUBLANES, width), F32)


def _log_sigmoid(z):
    return -_softplus(-z)


def _ab_proj_prompt_kernel(x_ref, g_ref, wq_ref, wkvt_ref, wft_ref, bf_ref, wb_ref, taps_ref,
                           kt_ref, vt_ref, ktb_ref, vtb_ref, lft_ref, qb_ref, ob_ref, tail_ref, ext_ref,
                           *, head_dim, tiles_per_seq):
    a_width = wq_ref.shape[1]
    b_width = wb_ref.shape[1] // 3
    h = _rms(x_ref[...], g_ref[...]).astype(BF16)
    qb_ref[...] = (_dot(h, wq_ref[...]) * head_dim ** -0.5).astype(BF16)
    kvt = _dot_nt(wkvt_ref[...], h)
    kt = kvt[:a_width]
    vt = kvt[a_width:]
    kt_ref[...] = kt
    vt_ref[...] = vt
    ktb_ref[...] = kt.astype(BF16)
    vtb_ref[...] = vt.astype(BF16)
    lft_ref[...] = _log_sigmoid(_dot_nt(wft_ref[...], h) + bf_ref[...])

    gbu = _dot(h, wb_ref[...])
    gate_b = gbu[:, :b_width]
    gcu = gbu[:, b_width:2 * b_width] * gbu[:, 2 * b_width:]
    _conv_prologue(ext_ref, pl.program_id(0) % tiles_per_seq == 0, b_width)
    y = _causal_conv(gcu, taps_ref, ext_ref, None, None)
    ob_ref[...] = (gate_b * y).astype(BF16)
    tail = gcu[gcu.shape[0] - SUBLANES:, :]
    tail_ref[...] = tail
    ext_ref[pl.ds(0, SUBLANES), :] = tail


def _ab_proj_prompt(x, gain, wq, wkvt, wft, bf_col, wb, taps, *, batch, seq_len, head_dim, tm):
    t, d = x.shape
    a_width = wq.shape[1]
    n_heads = wft.shape[0]
    b_width = wb.shape[1] // 3
    tiles_per_seq = seq_len // tm
    row = lambda c: pl.BlockSpec((tm, c), lambda i: (i, 0))
    feat = lambda c: pl.BlockSpec((None, c, tm), lambda i: (i // tiles_per_seq, 0, i % tiles_per_seq))
    feat_shape = lambda c, dt: jax.ShapeDtypeStruct((batch, c, seq_len), dt)
    return pl.pallas_call(
        functools.partial(_ab_proj_prompt_kernel, head_dim=head_dim, tiles_per_seq=tiles_per_seq),
        grid=(t // tm,),
        in_specs=[row(d)] + [_const_spec(a) for a in (gain, wq, wkvt, wft, bf_col, wb, taps)],
        out_specs=[feat(a_width), feat(a_width), feat(a_width), feat(a_width), feat(n_heads),
                   row(a_width), row(b_width),
                   pl.BlockSpec((None, SUBLANES, b_width), lambda i: (i, 0, 0))],
        out_shape=[feat_shape(a_width, F32), feat_shape(a_width, F32), feat_shape(a_width, BF16),
                   feat_shape(a_width, BF16), feat_shape(n_heads, F32),
                   jax.ShapeDtypeStruct((t, a_width), BF16), jax.ShapeDtypeStruct((t, b_width), BF16),
                   jax.ShapeDtypeStruct((t // tm, SUBLANES, b_width), F32)],
        scratch_shapes=[pltpu.VMEM((SUBLANES + tm, b_width), F32)],
        compiler_params=_params("arbitrary"),
        name="ab_proj_prompt",
    )(x, gain, wq, wkvt, wft, bf_col, wb, taps)


def _ab_proj_sample_kernel(*refs, n_heads, head_dim, seq_len, n_over):
    x_ref, g_ref, wqkv_ref, wf_ref, bf_ref, wb_ref, taps_ref = refs[:7]
    over = list(refs[7:7 + n_over])
    q_ref, k_ref, v_ref, logf_ref, ob_ref, u_ref, ext_ref = refs[7 + n_over:]
    a_width = n_heads * head_dim
    b_width = wb_ref.shape[1] // 3
    h = _rms(x_ref[...], g_ref[...]).astype(BF16)
    qkv = _dot(h, wqkv_ref[...])
    q_ref[...] = qkv[:, :a_width] * head_dim ** -0.5
    k_ref[...] = qkv[:, a_width:2 * a_width]
    v_ref[...] = qkv[:, 2 * a_width:]
    logf_ref[...] = _log_sigmoid(_dot(h, wf_ref[...]) + bf_ref[...])[:, :n_heads]

    gbu = _dot(h, wb_ref[...])
    gate_b = gbu[:, :b_width]
    gcu = gbu[:, b_width:2 * b_width] * gbu[:, 2 * b_width:]
    _conv_prologue(ext_ref, pl.program_id(0) == 0, b_width)
    y = _causal_conv(gcu, taps_ref, ext_ref, over, seq_len)
    ob_ref[...] = (gate_b * y).astype(BF16)
    u_ref[...] = gcu


def _ab_proj_sample(x, gain, wqkv, wf, bf, wb, taps, overrides, *, n_heads, head_dim, seq_len):
    t, d = x.shape
    a_width = n_heads * head_dim
    b_width = wb.shape[1] // 3
    full = lambda c: pl.BlockSpec((t, c), lambda i: (0, 0))
    f32 = lambda c: jax.ShapeDtypeStruct((t, c), F32)
    return pl.pallas_call(
        functools.partial(_ab_proj_sample_kernel, n_heads=n_heads, head_dim=head_dim, seq_len=seq_len,
                          n_over=len(overrides)),
        grid=(1,),
        in_specs=[full(d)] + [_const_spec(a) for a in (gain, wqkv, wf, bf, wb, taps)]
        + [full(b_width)] * len(overrides),
        out_specs=[full(a_width), full(a_width), full(a_width), full(n_heads), full(b_width), full(b_width)],
        out_shape=[f32(a_width), f32(a_width), f32(a_width), f32(n_heads),
                   jax.ShapeDtypeStruct((t, b_width), BF16), f32(b_width)],
        scratch_shapes=[pltpu.VMEM((SUBLANES + t, b_width), F32)],
        compiler_params=_params("arbitrary"),
        name="ab_proj_sample",
    )(x, gain, wqkv, wf, bf, wb, taps, *overrides)


def _scan_kernel(x_ref, o_ref, *, seg):
    x = x_ref[...]
    pos = lax.broadcasted_iota(jnp.int32, x.shape, 1) % seg
    d = 1
    while d < seg:
        x = x + jnp.where(pos >= d, pltpu.roll(x, d, axis=1), 0.0)
        d *= 2
    o_ref[...] = x


def _segment_cumsum(x, seg):
    b, h, length = x.shape
    spec = pl.BlockSpec((None, h, length), lambda i: (i, 0, 0))
    return pl.pallas_call(
        functools.partial(_scan_kernel, seg=seg),
        grid=(b,),
        in_specs=[spec],
        out_specs=spec,
        out_shape=jax.ShapeDtypeStruct(x.shape, F32),
        compiler_params=_params("arbitrary"),
        name="segment_cumsum",
    )(x)


def _fox_kernel(q_ref, kt_ref, vt_ref, ct_ref, o_ref, *, tile, head_dim):
    hp = pl.program_id(1)
    qi = pl.program_id(2)
    heads = q_ref.shape[1] // head_dim
    row = lax.broadcasted_iota(jnp.int32, (tile, tile), 0)
    col = lax.broadcasted_iota(jnp.int32, (tile, tile), 1)
    outs = []
    for j in range(heads):
        head = hp * heads + j
        feats = slice(j * head_dim, (j + 1) * head_dim)
        q = q_ref[:, feats]

        def block(ki, carry, masked, feats=feats, head=head, q=q):
            m, l, acc = carry
            off = pl.multiple_of(ki * tile, tile)
            s = _dot(q, kt_ref[feats, pl.ds(off, tile)])
            s = s - ct_ref[pl.ds(head, 1), pl.ds(off, tile)]
            if masked:
                s = jnp.where(col <= row, s, -jnp.inf)
            m_new = jnp.maximum(m, jnp.max(s, axis=1, keepdims=True))
            alpha = jnp.exp(m - m_new)
            p = jnp.exp(s - m_new)
            l = alpha * l + jnp.sum(p, axis=1, keepdims=True)
            acc = alpha * acc + _dot_nt(p.astype(BF16), vt_ref[feats, pl.ds(off, tile)])
            return m_new, l, acc

        init = (jnp.full((tile, 1), -jnp.inf, F32), jnp.zeros((tile, 1), F32),
                jnp.zeros((tile, head_dim), F32))
        carry = lax.fori_loop(0, qi, functools.partial(block, masked=False), init)
        _, l, acc = block(qi, carry, True)
        outs.append(acc / l)
    o_ref[...] = jnp.concatenate(outs, axis=1).astype(BF16)


def _fox_attention(qb, ktb, vtb, ct, *, head_dim, tile):
    t, a_width = qb.shape
    batch, n_heads, seq_len = ct.shape
    nq = seq_len // tile
    q_spec = pl.BlockSpec((tile, LANES), lambda b, hp, qi: (b * nq + qi, hp))
    kv_spec = pl.BlockSpec((None, LANES, seq_len), lambda b, hp, qi: (b, hp, 0))
    ct_spec = pl.BlockSpec((None, n_heads, seq_len), lambda b, hp, qi: (b, 0, 0))
    return pl.pallas_call(
        functools.partial(_fox_kernel, tile=tile, head_dim=head_dim),
        grid=(batch, a_width // LANES, nq),
        in_specs=[q_spec, kv_spec, kv_spec, ct_spec],
        out_specs=q_spec,
        out_shape=jax.ShapeDtypeStruct((t, a_width), BF16),
        compiler_params=_params("arbitrary", "arbitrary", "arbitrary"),
        name="fox_attention",
    )(qb, ktb, vtb, ct)


def _split3(x):
    hi = x.astype(BF16)
    r = x - hi.astype(F32)
    mid = r.astype(BF16)
    lo = (r - mid.astype(F32)).astype(BF16)
    return hi, mid, lo


def _paged_attn_kernel(pt_ref, q_ref, knew_ref, vnew_ref, lfnew_ref, tri_ref, *refs,
                       n_pages, n_heads, head_dim):
    del pt_ref
    k_refs = refs[:n_pages]
    v_refs = refs[n_pages:2 * n_pages]
    lf_refs = refs[2 * n_pages:3 * n_pages]
    o_ref, qbd_ref, m_ref, l_ref, acc_ref, pref_ref = refs[3 * n_pages:]
    step = pl.program_id(1)
    nq, width = q_ref.shape
    rows = nq * n_heads
    page = k_refs[0].shape[1]
    head_of_lane = lax.broadcasted_iota(jnp.int32, (n_heads, width), 1) // head_dim
    own = head_of_lane == lax.broadcasted_iota(jnp.int32, (n_heads, width), 0)

    @pl.when(step == 0)
    def _():
        q = q_ref[...]
        for qq in range(nq):
            blk = jnp.where(own, jnp.broadcast_to(q[qq:qq + 1, :], (n_heads, width)), 0.0)
            qbd_ref[pl.ds(qq * n_heads, n_heads), :] = blk
        m_ref[...] = jnp.full(m_ref.shape, -jnp.inf, F32)
        l_ref[...] = jnp.zeros(l_ref.shape, F32)
        acc_ref[...] = jnp.zeros(acc_ref.shape, F32)
        pref_ref[...] = jnp.zeros(pref_ref.shape, F32)

    def attend(k_list, v_list, lf_list, causal):
        n = len(k_list)
        ys = [lf[...] for lf in lf_list]
        y = jnp.concatenate(ys, axis=0) if n > 1 else ys[0]
        pieces = jnp.concatenate(_split3(y), axis=0)
        cw = _dot(pieces, tri_ref[...])
        nh = n * n_heads
        cw = cw[:nh] + cw[nh:2 * nh] + cw[2 * nh:]
        qbd = qbd_ref[...].astype(BF16)
        pref = pref_ref[...]
        s_list = []
        for g in range(n):
            cg = cw[g * n_heads:(g + 1) * n_heads]
            c = cg + pref
            pref = pref + cg[:, page - 1:page]
            s = _dot(qbd, k_list[g][...].astype(BF16))
            s = s - jnp.concatenate([c] * nq, axis=0)
            if causal:
                key = lax.broadcasted_iota(jnp.int32, (rows, page), 1)
                qpos = lax.broadcasted_iota(jnp.int32, (rows, page), 0) // n_heads
                s = jnp.where(key <= qpos, s, -jnp.inf)
            s_list.append(s)
        pref_ref[...] = pref
        s_all = jnp.concatenate(s_list, axis=1) if n > 1 else s_list[0]
        m = m_ref[...]
        m_new = jnp.maximum(m, jnp.max(s_all, axis=1, keepdims=True))
        alpha = jnp.exp(m - m_new)
        p = jnp.exp(s_all - m_new)
        l_ref[...] = alpha * l_ref[...] + jnp.sum(p, axis=1, keepdims=True)
        pv = None
        for g in range(n):
            part = _dot_nt(p[:, g * page:(g + 1) * page].astype(BF16), v_list[g][...].astype(BF16))
            pv = part if pv is None else pv + part
        acc_ref[...] = alpha * acc_ref[...] + pv
        m_ref[...] = m_new

    attend(k_refs, v_refs, lf_refs, causal=False)

    @pl.when(step == pl.num_programs(1) - 1)
    def _():
        attend([knew_ref], [vnew_ref], [lfnew_ref], causal=True)
        o = acc_ref[...] / l_ref[...]
        out_rows = []
        for qq in range(nq):
            blk = jnp.where(own, o[qq * n_heads:(qq + 1) * n_heads, :], 0.0)
            out_rows.append(jnp.sum(blk, axis=0, keepdims=True))
        o_ref[...] = jnp.concatenate(out_rows, axis=0).astype(o_ref.dtype)


def _paged_attention(q, kt_new, vt_new, lf_new, kt_pool, vt_pool, lf_pool, page_table, *, head_dim):
    batch, nq, width = q.shape
    page = kt_pool.shape[2]
    n_heads = lf_pool.shape[1]
    n_pages = PAGES_PER_STEP
    steps = page_table.shape[1] // n_pages
    rows = nq * n_heads
    tri = (jnp.arange(page)[:, None] <= jnp.arange(page)[None, :]).astype(BF16)

    per_batch = lambda shape: pl.BlockSpec((None,) + shape, lambda b, s, pt: (b, 0, 0))

    def paged(shape, g):
        return pl.BlockSpec((None,) + shape, lambda b, s, pt: (pt[b, s * n_pages + g], 0, 0))

    in_specs = [per_batch((nq, width)), per_batch((width, page)), per_batch((width, page)),
                per_batch((n_heads, page)), pl.BlockSpec(tri.shape, lambda b, s, pt: (0, 0))]
    in_specs += [paged((width, page), g) for g in range(n_pages)]
    in_specs += [paged((width, page), g) for g in range(n_pages)]
    in_specs += [paged((n_heads, page), g) for g in range(n_pages)]
    grid_spec = pltpu.PrefetchScalarGridSpec(
        num_scalar_prefetch=1,
        grid=(batch, steps),
        in_specs=in_specs,
        out_specs=per_batch((nq, width)),
        scratch_shapes=[pltpu.VMEM((rows, width), F32), pltpu.VMEM((rows, 1), F32),
                        pltpu.VMEM((rows, 1), F32), pltpu.VMEM((rows, width), F32),
                        pltpu.VMEM((n_heads, 1), F32)],
    )
    return pl.pallas_call(
        functools.partial(_paged_attn_kernel, n_pages=n_pages, n_heads=n_heads, head_dim=head_dim),
        grid_spec=grid_spec,
        out_shape=jax.ShapeDtypeStruct((batch, nq, width), BF16),
        compiler_params=_params("arbitrary", "arbitrary"),
        name="paged_attention",
    )(page_table, q, kt_new, vt_new, lf_new, tri,
      *([kt_pool] * n_pages), *([vt_pool] * n_pages), *([lf_pool] * n_pages))


def _out_proj_kernel(*refs, n_parts):
    parts = refs[:n_parts]
    w_ref, x_ref, g_ref, o_ref = refs[n_parts:]
    mix = None
    k0 = 0
    for a_ref in parts:
        kw = a_ref.shape[1]
        part = _dot(a_ref[...], w_ref[k0:k0 + kw, :])
        mix = part if mix is None else mix + part
        k0 += kw
    o_ref[...] = x_ref[...] + _rms(mix, g_ref[...])


def _out_proj(parts, w, x, gain, *, tm):
    t, d = x.shape
    row = lambda c: pl.BlockSpec((tm, c), lambda i: (i, 0))
    return pl.pallas_call(
        functools.partial(_out_proj_kernel, n_parts=len(parts)),
        grid=(t // tm,),
        in_specs=[row(p.shape[1]) for p in parts] + [_const_spec(w), row(d), _const_spec(gain)],
        out_specs=row(d),
        out_shape=jax.ShapeDtypeStruct((t, d), F32),
        compiler_params=_params("arbitrary"),
        name="out_proj",
    )(*parts, w, x, gain)


def _c_proj_kernel(*refs, n_heads, key_dim, seq_len, tiles_per_seq, n_over):
    x_ref, g_ref, wqkv_ref, wz_ref, wba_ref, taps_ref, alog_ref, dtb_ref = refs[:8]
    over = list(refs[8:8 + n_over]) if n_over else None
    (q_ref, k_ref, v_ref, z_ref, beta_ref, gdec_ref, u_ref, ext_ref) = refs[8 + n_over:]
    qk_width = n_heads * key_dim
    conv_width = wqkv_ref.shape[1]

    h = _rms(x_ref[...], g_ref[...]).astype(BF16)
    qkv = _dot(h, wqkv_ref[...])
    z_ref[...] = _dot(h, wz_ref[...]).astype(BF16)
    ba = _dot(h, wba_ref[...])
    beta_ref[...] = jax.nn.sigmoid(ba)[:, :n_heads]
    gdec = -jnp.exp(alog_ref[...]) * _softplus(ba + dtb_ref[...])
    gdec_ref[...] = gdec[:, n_heads:2 * n_heads]

    first = pl.program_id(0) % tiles_per_seq == 0 if over is None else pl.program_id(0) == 0
    _conv_prologue(ext_ref, first, conv_width)
    y = _silu(_causal_conv(qkv, taps_ref, ext_ref, over, seq_len))
    tm = qkv.shape[0]
    if over is None:
        tail = qkv[tm - SUBLANES:, :]
        u_ref[...] = tail
        ext_ref[pl.ds(0, SUBLANES), :] = tail
    else:
        u_ref[...] = qkv

    for hh in range(n_heads):
        lanes = slice(hh * key_dim, (hh + 1) * key_dim)
        qh = y[:, hh * key_dim:(hh + 1) * key_dim]
        kh = y[:, qk_width + hh * key_dim:qk_width + (hh + 1) * key_dim]
        qn = qh * lax.rsqrt(jnp.sum(qh * qh, axis=-1, keepdims=True) + NORM_EPS) * key_dim ** -0.5
        kn = kh * lax.rsqrt(jnp.sum(kh * kh, axis=-1, keepdims=True) + NORM_EPS)
        q_ref[:, lanes] = qn.astype(BF16)
        k_ref[:, lanes] = kn.astype(BF16)
    v_ref[...] = y[:, 2 * qk_width:].astype(BF16)


def _c_proj(x, gain, wqkv, wz, wba, taps, alog, dtb, overrides, *, n_heads, key_dim, seq_len, tm):
    t, d = x.shape
    conv_width = wqkv.shape[1]
    qk_width = n_heads * key_dim
    v_width = conv_width - 2 * qk_width
    n_tiles = t // tm
    n_over = 0 if overrides is None else len(overrides)
    row = lambda c: pl.BlockSpec((tm, c), lambda i: (i, 0))
    in_specs = [row(d)] + [_const_spec(a) for a in (gain, wqkv, wz, wba, taps, alog, dtb)]
    in_specs += [row(conv_width)] * n_over
    if overrides is None:
        u_spec = pl.BlockSpec((None, SUBLANES, conv_width), lambda i: (i, 0, 0))
        u_shape = jax.ShapeDtypeStruct((n_tiles, SUBLANES, conv_width), F32)
    else:
        u_spec = row(conv_width)
        u_shape = jax.ShapeDtypeStruct((t, conv_width), F32)
    out_specs = [row(qk_width), row(qk_width), row(v_width), row(v_width), row(n_heads), row(n_heads), u_spec]
    out_shape = [jax.ShapeDtypeStruct((t, qk_width), BF16), jax.ShapeDtypeStruct((t, qk_width), BF16),
                 jax.ShapeDtypeStruct((t, v_width), BF16), jax.ShapeDtypeStruct((t, v_width), BF16),
                 jax.ShapeDtypeStruct((t, n_heads), F32), jax.ShapeDtypeStruct((t, n_heads), F32), u_shape]
    return pl.pallas_call(
        functools.partial(_c_proj_kernel, n_heads=n_heads, key_dim=key_dim, seq_len=seq_len,
                          tiles_per_seq=max(seq_len // tm, 1), n_over=n_over),
        grid=(n_tiles,),
        in_specs=in_specs,
        out_specs=out_specs,
        out_shape=out_shape,
        scratch_shapes=[pltpu.VMEM((SUBLANES + tm, conv_width), F32)],
        compiler_params=_params("arbitrary"),
        name="c_proj",
    )(x, gain, wqkv, wz, wba, taps, alog, dtb, *(overrides or []))


def _delta_kernel(q_ref, k_ref, v_ref, z_ref, gcol_ref, beta_ref, grow_ref, s0_ref, nw_ref,
                  o_ref, s_out_ref, m_ref, *, n_heads, chunk):
    step = pl.program_id(1)
    rows = q_ref.shape[0]
    dk = q_ref.shape[1] // n_heads
    dv = v_ref.shape[1] // n_heads
    n_chunks = rows // chunk
    ri = lax.broadcasted_iota(jnp.int32, (chunk, chunk), 0)
    ci = lax.broadcasted_iota(jnp.int32, (chunk, chunk), 1)
    incl = (ci <= ri)[None]
    strict = (ci < ri)[None]

    @pl.when(step == 0)
    def _():
        for h in range(n_heads):
            m_ref[h] = s0_ref[h].T

    probs = [(c, h) for c in range(n_chunks) for h in range(n_heads)]
    rs = lambda c: slice(c * chunk, (c + 1) * chunk)
    stack = lambda f: jnp.stack([f(c, h) for c, h in probs])
    kb = stack(lambda c, h: k_ref[rs(c), h * dk:(h + 1) * dk])
    qb = stack(lambda c, h: q_ref[rs(c), h * dk:(h + 1) * dk])
    vf = stack(lambda c, h: v_ref[rs(c), h * dv:(h + 1) * dv]).astype(F32)
    gc = stack(lambda c, h: gcol_ref[rs(c), h:h + 1])
    gr = stack(lambda c, h: grow_ref[h:h + 1, rs(c)])
    bc = stack(lambda c, h: beta_ref[rs(c), h:h + 1])
    kf = kb.astype(F32)
    eg = jnp.exp(gc)
    decay = jnp.where(incl, jnp.exp(jnp.where(incl, gc - gr, 0.0)), 0.0)
    kqk = _bdot_nt(jnp.concatenate([kb, qb], axis=1), kb)
    a = jnp.where(strict, bc * kqk[:, :chunk] * decay, 0.0)
    nil = -a
    pw = a
    span = 2
    while span < chunk:
        pwb = pw.astype(BF16)
        pw = _bdot(pwb, pwb)
        nil = nil + pw + _bdot(nil.astype(BF16), pw.astype(BF16))
        span *= 2
    rhs = jnp.concatenate([bc * vf, (bc * eg) * kf], axis=2)
    w = rhs + _bdot(nil.astype(BF16), rhs.astype(BF16))
    w_v = w[:, :, :dv]
    p = (kqk[:, chunk:] * decay).astype(BF16)
    g_last = gc[:, chunk - 1:chunk, :]
    lhs = jnp.concatenate([w[:, :, dv:], eg * qb.astype(F32)], axis=1).astype(BF16)
    k_dec = (jnp.exp(g_last - gc) * kf).astype(BF16)
    g_chunk = jnp.exp(g_last)

    for c in range(n_chunks):
        hs = slice(c * n_heads, (c + 1) * n_heads)
        m = m_ref[...]
        r = _bdot(lhs[hs], m.astype(BF16))
        ub = (w_v[hs] - r[:, :chunk]).astype(BF16)
        o = r[:, chunk:] + _bdot(p[hs], ub)
        m_ref[...] = g_chunk[hs] * m + _bdot_tn(k_dec[hs], ub)
        on = o * lax.rsqrt(jnp.mean(o * o, axis=-1, keepdims=True) + NORM_EPS) * nw_ref[...]
        for h in range(n_heads):
            vl = slice(h * dv, (h + 1) * dv)
            o_ref[rs(c), vl] = (on[h] * _silu(z_ref[rs(c), vl].astype(F32))).astype(BF16)

    @pl.when(step == pl.num_programs(1) - 1)
    def _():
        for h in range(n_heads):
            s_out_ref[h] = m_ref[h].T


def _delta_rule(q, k, v, z, gcol, beta, grow, s0, norm_w, *, batch, seq_len, n_heads, rows):
    t = q.shape[0]
    steps = seq_len // rows
    dk = q.shape[1] // n_heads
    dv = v.shape[1] // n_heads
    row = lambda c: pl.BlockSpec((rows, c), lambda b, s: (b * steps + s, 0))
    state = pl.BlockSpec((None, n_heads, dv, dk), lambda b, s: (b, 0, 0, 0))
    return pl.pallas_call(
        functools.partial(_delta_kernel, n_heads=n_heads, chunk=DELTA_CHUNK),
        grid=(batch, steps),
        in_specs=[row(q.shape[1]), row(k.shape[1]), row(v.shape[1]), row(z.shape[1]),
                  row(n_heads), row(n_heads),
                  pl.BlockSpec((None, n_heads, rows), lambda b, s: (b, 0, s)),
                  state, pl.BlockSpec(norm_w.shape, lambda b, s: (0, 0))],
        out_specs=[row(v.shape[1]), state],
        out_shape=[jax.ShapeDtypeStruct((t, v.shape[1]), BF16),
                   jax.ShapeDtypeStruct((batch, n_heads, dv, dk), F32)],
        scratch_shapes=[pltpu.VMEM((n_heads, dk, dv), F32)],
        compiler_params=_params("arbitrary", "arbitrary"),
        name="gated_delta_rule",
    )(q, k, v, z, gcol, beta, grow, s0, norm_w)


def _pad_lanes(a):
    return jnp.pad(a, ((0, 0), (0, LANES - a.shape[1])))


def _rows_to_heads(a, batch, seq_len):
    return a.reshape(batch, seq_len, a.shape[1]).transpose(0, 2, 1)


def _heads_to_rows(a):
    b, h, length = a.shape
    return a.transpose(0, 2, 1).reshape(b * length, h)


def _feature_major_pool(pool):
    n_l, n_p, page = pool.shape[:3]
    perm = (0, 1) + tuple(range(3, pool.ndim)) + (2,)
    return pool.transpose(perm).reshape(n_l * n_p, -1, page)


def _mixer_ab(x, past, w, i, gain, *, batch, seq_len, tm):
    t = x.shape[0]
    hd = w["a_head_dim"]
    nh = w["ab_b_f"].shape[1]
    aw = nh * hd
    w_in = w["ab_w_in"][i]
    bw = (w_in.shape[1] - 3 * aw - nh) // 3
    wb = w_in[:, 3 * aw + nh:].astype(BF16)
    wf = w_in[:, 3 * aw:3 * aw + nh]
    taps = w["ab_conv_w"][i]
    n_keep = taps.shape[0] - 1
    if past is None:
        wq = w_in[:, :aw].astype(BF16)
        wkvt = w_in[:, aw:3 * aw].T.astype(BF16)
        kt, vt, ktb, vtb, lft, qb, ob, tails = _ab_proj_prompt(
            x, gain, wq, wkvt, wf.T.astype(BF16), w["ab_b_f"][i][:, None], wb, taps,
            batch=batch, seq_len=seq_len, head_dim=hd, tm=tm)
        ct = _segment_cumsum(lft, seq_len)
        o_a = _fox_attention(qb, ktb, vtb, ct, head_dim=hd, tile=min(512, seq_len))
        tails = tails.reshape(batch, seq_len // tm, SUBLANES, bw)[:, -1]
        new_buf = tails[:, SUBLANES - n_keep:, :]
        to_tokens = lambda a: a.reshape(batch, nh, hd, seq_len).transpose(0, 3, 1, 2)
        state = (to_tokens(kt), to_tokens(vt), lft.transpose(0, 2, 1), new_buf)
    else:
        overrides = _conv_overrides(past["state_b_conv"][i], seq_len)
        q_s, k32, v32, logf, ob, gcu = _ab_proj_sample(
            x, gain, w_in[:, :3 * aw].astype(BF16), _pad_lanes(wf).astype(BF16),
            _pad_lanes(w["ab_b_f"][i][None, :]), wb, taps, overrides,
            n_heads=nh, head_dim=hd, seq_len=seq_len)
        n_phys, page = past["cache_a_k"].shape[1:3]
        pad_keys = lambda a: jnp.pad(a.reshape(batch, seq_len, -1).transpose(0, 2, 1),
                                     ((0, 0), (0, 0), (0, page - seq_len)))
        o_a = _paged_attention(
            q_s.reshape(batch, seq_len, aw), pad_keys(k32), pad_keys(v32), pad_keys(logf),
            _feature_major_pool(past["cache_a_k"]), _feature_major_pool(past["cache_a_v"]),
            _feature_major_pool(past["cache_a_logf"]), past["page_table"] + i * n_phys, head_dim=hd)
        o_a = o_a.reshape(t, aw)
        new_buf = gcu.reshape(batch, seq_len, bw)[:, seq_len - n_keep:, :]
        state = (k32.reshape(batch, seq_len, nh, hd), v32.reshape(batch, seq_len, nh, hd),
                 logf.reshape(batch, seq_len, nh), new_buf)
    return [o_a, ob], w["ab_w_out"][i].astype(BF16), state


def _mixer_c(x, past, w, i, gain, *, batch, seq_len, tm):
    t = x.shape[0]
    nh = w["c_a_log"].shape[1]
    dv = w["c_norm_w"].shape[1]
    w_in = w["c_w_in"][i]
    v_width = nh * dv
    conv_width = w_in.shape[1] - v_width - 2 * nh
    dk = (conv_width - v_width) // 2 // nh
    wqkv = w_in[:, :conv_width].astype(BF16)
    wz = w_in[:, conv_width:conv_width + v_width].astype(BF16)
    wba = _pad_lanes(w_in[:, conv_width + v_width:]).astype(BF16)
    zeros_h = jnp.zeros((1, nh), F32)
    alog = _pad_lanes(jnp.concatenate([zeros_h, w["c_a_log"][i][None, :]], axis=1))
    dtb = _pad_lanes(jnp.concatenate([zeros_h, w["c_dt_bias"][i][None, :]], axis=1))
    taps = w["c_conv_w"][i]
    n_keep = taps.shape[0] - 1
    tm_c = min(tm, 256)
    if past is None:
        overrides = None
        s0 = jnp.zeros((batch, nh, dv, dk), F32)
    else:
        overrides = _conv_overrides(past["state_c_conv"][i], seq_len)
        s0 = past["state_c_S"][i]
    qn, kn, vn, zb, beta, gdec, u_out = _c_proj(
        x, gain, wqkv, wz, wba, taps, alog, dtb, overrides,
        n_heads=nh, key_dim=dk, seq_len=seq_len, tm=tm_c)
    if past is None:
        tails = u_out.reshape(batch, seq_len // tm_c, SUBLANES, conv_width)[:, -1]
        new_buf = tails[:, SUBLANES - n_keep:, :]
        len_pad = seq_len
    else:
        new_buf = u_out.reshape(batch, seq_len, conv_width)[:, seq_len - n_keep:, :]
        len_pad = -(-seq_len // DELTA_CHUNK) * DELTA_CHUNK
        pad3 = lambda a: jnp.pad(a.reshape(batch, seq_len, a.shape[1]),
                                 ((0, 0), (0, len_pad - seq_len), (0, 0))).reshape(batch * len_pad, a.shape[1])
        qn, kn, vn, zb, beta, gdec = [pad3(a) for a in (qn, kn, vn, zb, beta, gdec)]
    scan_len = -(-len_pad // LANES) * LANES
    g_heads = jnp.pad(_rows_to_heads(gdec, batch, len_pad), ((0, 0), (0, 0), (0, scan_len - len_pad)))
    grow = _segment_cumsum(g_heads, DELTA_CHUNK)[:, :, :len_pad]
    og, s_new = _delta_rule(qn, kn, vn, zb, _heads_to_rows(grow), beta, grow, s0, w["c_norm_w"][i][None, :],
                            batch=batch, seq_len=len_pad, n_heads=nh, rows=min(4 * DELTA_CHUNK, len_pad))
    if len_pad != seq_len:
        og = og.reshape(batch, len_pad, v_width)[:, :seq_len].reshape(t, v_width)
    return [og], w["c_w_out"][i].astype(BF16), (new_buf, s_new)


def _run_trunk(x3, past, w, *, tm):
    batch, seq_len, d = x3.shape
    x = x3.reshape(batch * seq_len, d)
    ab_states, c_states = [], []
    ffn = lambda x, g0, g1, layer, j: _ffn_block(
        x, g0, g1, w["ffn_w_gate"], w["ffn_w_up"], w["ffn_w_down"], layer, j, tm=tm)
    for layer in range(w["norm_g"].shape[0]):
        gains = [w["norm_g"][layer, j][None, :] for j in range(6)]
        x = ffn(x, gains[0], gains[1], layer, 0)
        mixer, states = (_mixer_ab, ab_states) if layer % 2 == 0 else (_mixer_c, c_states)
        parts, w_out, state = mixer(x, past, w, layer // 2, gains[2], batch=batch, seq_len=seq_len, tm=tm)
        states.append(state)
        x = _out_proj(parts, w_out, x, gains[3], tm=tm)
        x = ffn(x, gains[4], gains[5], layer, 1)
    ab_new = [jnp.stack(ts) for ts in zip(*ab_states)]
    c_new = [jnp.stack(ts) for ts in zip(*c_states)]
    return x.reshape(batch, seq_len, d), ab_new, c_new


def kernel(x_prompt, x_sample, cache_a_k, cache_a_v, cache_a_logf, page_table, state_b_conv, state_c_conv, state_c_S, norm_g, ffn_w_gate, ffn_w_up, ffn_w_down, ab_w_in, ab_b_f, ab_conv_w, ab_w_out, c_w_in, c_conv_w, c_a_log, c_dt_bias, c_norm_w, c_w_out):
    w = dict(norm_g=norm_g,
             ffn_w_gate=ffn_w_gate.astype(BF16), ffn_w_up=ffn_w_up.astype(BF16), ffn_w_down=ffn_w_down.astype(BF16),
             ab_w_in=ab_w_in, ab_b_f=ab_b_f, ab_conv_w=ab_conv_w, ab_w_out=ab_w_out,
             c_w_in=c_w_in, c_conv_w=c_conv_w, c_a_log=c_a_log, c_dt_bias=c_dt_bias,
             c_norm_w=c_norm_w, c_w_out=c_w_out, a_head_dim=cache_a_k.shape[-1])
    past = dict(cache_a_k=cache_a_k, cache_a_v=cache_a_v, cache_a_logf=cache_a_logf, page_table=page_table,
                state_b_conv=state_b_conv, state_c_conv=state_c_conv, state_c_S=state_c_S)
    prompt_rows = x_prompt.shape[0] * x_prompt.shape[1]
    sample_rows = x_sample.shape[0] * x_sample.shape[1]
    y_prompt, ab_p, c_p = _run_trunk(x_prompt, None, w, tm=min(512, prompt_rows))
    y_sample, ab_s, c_s = _run_trunk(x_sample, past, w, tm=min(512, sample_rows))
    a_k_prompt, a_v_prompt, a_logf_prompt, b_conv_prompt = ab_p
    a_k_sample, a_v_sample, a_logf_sample, b_conv_sample = ab_s
    c_conv_prompt, c_S_prompt = c_p
    c_conv_sample, c_S_sample = c_s
    return (y_prompt, y_sample,
            a_k_prompt, a_v_prompt, a_logf_prompt,
            a_k_sample, a_v_sample, a_logf_sample,
            b_conv_prompt, b_conv_sample,
            c_conv_prompt, c_conv_sample,
            c_S_prompt, c_S_sample)
```

```python
import functools

import jax
import jax.numpy as jnp
from jax import lax
from jax.experimental import pallas as pl
from jax.experimental.pallas import tpu as pltpu

F32 = jnp.float32
BF16 = jnp.bfloat16
NORM_EPS = 1e-6
LOG2_E = 1.4426950408889634

LANES = 128
SUBLANES = 8
VMEM_LIMIT_BYTES = 56 * 1024 * 1024
DELTA_CHUNK = 64
CONV_CHUNK = 512
PAGES_PER_STEP = 16


def _params(*semantics):
    return pltpu.CompilerParams(dimension_semantics=semantics, vmem_limit_bytes=VMEM_LIMIT_BYTES)


def _resident(shape, index_map):
    return pl.BlockSpec(shape, index_map, pipeline_mode=pl.Buffered(1))


def _const_spec(a):
    return _resident(a.shape, lambda *_: (0,) * a.ndim)


def _rms(x, g):
    return x * lax.rsqrt(jnp.mean(x * x, axis=-1, keepdims=True) + NORM_EPS) * g


def _softplus(x):
    return jnp.maximum(x, 0.0) + jnp.log1p(jnp.exp(-jnp.abs(x)))


def _silu(x):
    return x * jax.nn.sigmoid(x)


def _dot(a, b):
    return jnp.dot(a, b, preferred_element_type=F32)


def _dot_nt(a, b):
    return lax.dot_general(a, b, (((1,), (1,)), ((), ())), preferred_element_type=F32)


def _dot_tn(a, b):
    return lax.dot_general(a, b, (((0,), (0,)), ((), ())), preferred_element_type=F32)


def _bdot(a, b):
    return lax.dot_general(a, b, (((2,), (1,)), ((0,), (0,))), preferred_element_type=F32)


def _bdot_nt(a, b):
    return lax.dot_general(a, b, (((2,), (2,)), ((0,), (0,))), preferred_element_type=F32)


def _bdot_tn(a, b):
    return lax.dot_general(a, b, (((1,), (1,)), ((0,), (0,))), preferred_element_type=F32)


def _ffn_kernel(x_ref, gpre_ref, gpost_ref, wg_ref, wu_ref, wd_ref, o_ref, *, ff_chunk):
    x = x_ref[...]
    h = _rms(x, gpre_ref[...]).astype(BF16)
    d_ff = wg_ref.shape[1]
    proj = lambda c0: (_dot(h, wg_ref[:, c0:c0 + ff_chunk]), _dot(h, wu_ref[:, c0:c0 + ff_chunk]))
    y = None
    nxt = proj(0)
    for c0 in range(0, d_ff, ff_chunk):
        gate, up = nxt
        if c0 + ff_chunk < d_ff:
            nxt = proj(c0 + ff_chunk)
        act = (_silu(gate) * up).astype(BF16)
        part = _dot(act, wd_ref[c0:c0 + ff_chunk, :])
        y = part if y is None else y + part
    o_ref[...] = x + 0.5 * _rms(y, gpost_ref[...])


def _ffn_chunk(d_ff):
    for c in range(4 * LANES, 0, -LANES):
        if d_ff % c == 0:
            return c
    return d_ff


def _ffn_block(x, g_pre, g_post, wg_all, wu_all, wd_all, layer, j, *, tm):
    t, d = x.shape
    d_ff = wg_all.shape[-1]
    row = pl.BlockSpec((tm, d), lambda i: (i, 0))
    gain = _resident((1, d), lambda i: (0, 0))
    w_in = _resident((None, None, d, d_ff), lambda i: (layer, j, 0, 0))
    w_out = _resident((None, None, d_ff, d), lambda i: (layer, j, 0, 0))
    return pl.pallas_call(
        functools.partial(_ffn_kernel, ff_chunk=_ffn_chunk(d_ff)),
        grid=(t // tm,),
        in_specs=[row, gain, gain, w_in, w_in, w_out],
        out_specs=row,
        out_shape=jax.ShapeDtypeStruct((t, d), F32),
        compiler_params=_params("arbitrary"),
        name="ffn_block",
    )(x, g_pre, g_post, wg_all, wu_all, wd_all)


def _causal_conv(u, taps_ref, ext_ref, overrides, seq_len, y_ref=None, col0=0):
    tm = u.shape[0]
    n_taps = taps_ref.shape[0]
    if overrides is None:
        group = SUBLANES * SUBLANES
        blocks = range(col0 // LANES, (col0 + u.shape[1]) // LANES)
        for lb in blocks:
            lanes = slice(lb * LANES, (lb + 1) * LANES)
            ext_ref[lb, pl.ds(SUBLANES, tm), :] = u[:, lb * LANES - col0:(lb + 1) * LANES - col0]
            taps = [taps_ref[n_taps - 1 - j:n_taps - j, lanes] for j in range(n_taps)]
            for g0 in range(0, tm, group):
                win = {d: ext_ref[lb, pl.ds(SUBLANES + g0 + d, SUBLANES, stride=SUBLANES), :]
                       for d in range(1 - n_taps, SUBLANES)}
                for i in range(SUBLANES):
                    acc = win[i] * taps[0]
                    for j in range(1, n_taps):
                        acc = acc + win[i - j] * taps[j]
                    y_ref[lb, pl.ds(g0 + i, SUBLANES, stride=SUBLANES), :] = acc
        return jnp.concatenate([y_ref[lb] for lb in blocks], axis=1)
    ext_ref[pl.ds(SUBLANES, tm), :] = u
    out = u * taps_ref[n_taps - 1:n_taps, :]
    pos = lax.broadcasted_iota(jnp.int32, u.shape, 0) % seq_len
    for j in range(1, n_taps):
        prev = jnp.where(pos >= j, ext_ref[pl.ds(SUBLANES - j, tm), :], overrides[j - 1][...])
        out = out + prev * taps_ref[n_taps - 1 - j:n_taps - j, :]
    return out


def _conv_overrides(state, seq_len):
    b, wm1, c = state.shape
    assert seq_len >= wm1
    outs = []
    for j in range(1, wm1 + 1):
        rows = jnp.pad(state[:, wm1 - j:, :], ((0, 0), (0, seq_len - j), (0, 0)))
        outs.append(rows.reshape(b * seq_len, c))
    return outs


def _conv_carry(ext_ref, tail, col0=0):
    if len(ext_ref.shape) == 2:
        ext_ref[pl.ds(0, SUBLANES), :] = tail
    else:
        for lb in range(tail.shape[1] // LANES):
            ext_ref[col0 // LANES + lb, pl.ds(0, SUBLANES), :] = tail[:, lb * LANES:(lb + 1) * LANES]


def _conv_prologue(ext_ref, first, width):
    @pl.when(first)
    def _():
        _conv_carry(ext_ref, jnp.zeros((SUBLANES, width), F32))


def _conv_scratch(tm, width, blocked):
    if not blocked:
        return [pltpu.VMEM((SUBLANES + tm, width), F32)]
    assert tm % (SUBLANES * SUBLANES) == 0 and width % LANES == 0
    return [pltpu.VMEM((width // LANES, SUBLANES + tm, LANES), F32), pltpu.VMEM((width // LANES, tm, LANES), F32)]


def _log_sigmoid(z):
    return -_softplus(-z)


def _ab_proj_prompt_kernel(x_ref, g_ref, wq_ref, wkvt_ref, wft_ref, bf_ref, wb_ref, taps_ref,
                           kt_ref, vt_ref, ktb_ref, vtb_ref, lft_ref, qb_ref, ob_ref, tail_ref, ext_ref, y_ref,
                           *, head_dim, tiles_per_seq):
    a_width = wq_ref.shape[1]
    b_width = wb_ref.shape[1] // 3
    h = _rms(x_ref[...], g_ref[...]).astype(BF16)
    qb_ref[...] = (_dot(h, wq_ref[...]) * (head_dim ** -0.5 * LOG2_E)).astype(BF16)
    kvt = _dot_nt(wkvt_ref[...], h)
    kt = kvt[:a_width]
    vt = kvt[a_width:]
    kt_ref[...] = kt
    vt_ref[...] = vt
    ktb_ref[...] = kt.astype(BF16)
    vtb_ref[...] = vt.astype(BF16)
    lft_ref[...] = _log_sigmoid(_dot_nt(wft_ref[...], h) + bf_ref[...])

    gbu = _dot(h, wb_ref[...])
    gate_b = gbu[:, :b_width]
    gcu = gbu[:, b_width:2 * b_width] * gbu[:, 2 * b_width:]
    _conv_prologue(ext_ref, pl.program_id(0) % tiles_per_seq == 0, b_width)
    y = _causal_conv(gcu, taps_ref, ext_ref, None, None, y_ref)
    ob_ref[...] = (gate_b * y).astype(BF16)
    tail = gcu[gcu.shape[0] - SUBLANES:, :]
    tail_ref[...] = tail
    _conv_carry(ext_ref, tail)


def _ab_proj_prompt(x, gain, wq, wkvt, wft, bf_col, wb, taps, *, batch, seq_len, head_dim, tm):
    t, d = x.shape
    a_width = wq.shape[1]
    n_heads = wft.shape[0]
    b_width = wb.shape[1] // 3
    tiles_per_seq = seq_len // tm
    row = lambda c: pl.BlockSpec((tm, c), lambda i: (i, 0))
    feat = lambda c: pl.BlockSpec((None, c, tm), lambda i: (i // tiles_per_seq, 0, i % tiles_per_seq))
    feat_shape = lambda c, dt: jax.ShapeDtypeStruct((batch, c, seq_len), dt)
    return pl.pallas_call(
        functools.partial(_ab_proj_prompt_kernel, head_dim=head_dim, tiles_per_seq=tiles_per_seq),
        grid=(t // tm,),
        in_specs=[row(d)] + [_const_spec(a) for a in (gain, wq, wkvt, wft, bf_col, wb, taps)],
        out_specs=[feat(a_width), feat(a_width), feat(a_width), feat(a_width), feat(n_heads),
                   row(a_width), row(b_width),
                   pl.BlockSpec((None, SUBLANES, b_width), lambda i: (i, 0, 0))],
        out_shape=[feat_shape(a_width, F32), feat_shape(a_width, F32), feat_shape(a_width, BF16),
                   feat_shape(a_width, BF16), feat_shape(n_heads, F32),
                   jax.ShapeDtypeStruct((t, a_width), BF16), jax.ShapeDtypeStruct((t, b_width), BF16),
                   jax.ShapeDtypeStruct((t // tm, SUBLANES, b_width), F32)],
        scratch_shapes=_conv_scratch(tm, b_width, True),
        compiler_params=_params("arbitrary"),
        name="ab_proj_prompt",
    )(x, gain, wq, wkvt, wft, bf_col, wb, taps)


def _ab_proj_sample_kernel(*refs, n_heads, head_dim, seq_len, n_over):
    x_ref, g_ref, wqkv_ref, wf_ref, bf_ref, wb_ref, taps_ref = refs[:7]
    over = list(refs[7:7 + n_over])
    q_ref, k_ref, v_ref, logf_ref, ob_ref, u_ref, ext_ref = refs[7 + n_over:]
    a_width = n_heads * head_dim
    b_width = wb_ref.shape[1] // 3
    h = _rms(x_ref[...], g_ref[...]).astype(BF16)
    qkv = _dot(h, wqkv_ref[...])
    q_ref[...] = qkv[:, :a_width] * head_dim ** -0.5
    k_ref[...] = qkv[:, a_width:2 * a_width]
    v_ref[...] = qkv[:, 2 * a_width:]
    logf_ref[...] = _log_sigmoid(_dot(h, wf_ref[...]) + bf_ref[...])[:, :n_heads]

    gbu = _dot(h, wb_ref[...])
    gate_b = gbu[:, :b_width]
    gcu = gbu[:, b_width:2 * b_width] * gbu[:, 2 * b_width:]
    _conv_prologue(ext_ref, pl.program_id(0) == 0, b_width)
    y = _causal_conv(gcu, taps_ref, ext_ref, over, seq_len)
    ob_ref[...] = (gate_b * y).astype(BF16)
    u_ref[...] = gcu


def _ab_proj_sample(x, gain, wqkv, wf, bf, wb, taps, overrides, *, n_heads, head_dim, seq_len):
    t, d = x.shape
    a_width = n_heads * head_dim
    b_width = wb.shape[1] // 3
    full = lambda c: pl.BlockSpec((t, c), lambda i: (0, 0))
    f32 = lambda c: jax.ShapeDtypeStruct((t, c), F32)
    return pl.pallas_call(
        functools.partial(_ab_proj_sample_kernel, n_heads=n_heads, head_dim=head_dim, seq_len=seq_len,
                          n_over=len(overrides)),
        grid=(1,),
        in_specs=[full(d)] + [_const_spec(a) for a in (gain, wqkv, wf, bf, wb, taps)]
        + [full(b_width)] * len(overrides),
        out_specs=[full(a_width), full(a_width), full(a_width), full(n_heads), full(b_width), full(b_width)],
        out_shape=[f32(a_width), f32(a_width), f32(a_width), f32(n_heads),
                   jax.ShapeDtypeStruct((t, b_width), BF16), f32(b_width)],
        scratch_shapes=[pltpu.VMEM((SUBLANES + t, b_width), F32)],
        compiler_params=_params("arbitrary"),
        name="ab_proj_sample",
    )(x, gain, wqkv, wf, bf, wb, taps, *overrides)


def _scan_kernel(x_ref, o_ref, *, seg):
    x = x_ref[...]
    pos = lax.broadcasted_iota(jnp.int32, x.shape, 1) % seg
    d = 1
    while d < seg:
        x = x + jnp.where(pos >= d, pltpu.roll(x, d, axis=1), 0.0)
        d *= 2
    o_ref[...] = x


def _segment_cumsum(x, seg):
    b, h, length = x.shape
    spec = pl.BlockSpec((None, h, length), lambda i: (i, 0, 0))
    return pl.pallas_call(
        functools.partial(_scan_kernel, seg=seg),
        grid=(b,),
        in_specs=[spec],
        out_specs=spec,
        out_shape=jax.ShapeDtypeStruct(x.shape, F32),
        compiler_params=_params("arbitrary"),
        name="segment_cumsum",
    )(x)


def _split3(x):
    hi = x.astype(BF16)
    r = x - hi.astype(F32)
    mid = r.astype(BF16)
    lo = (r - mid.astype(F32)).astype(BF16)
    return hi, mid, lo


def _forget_bias_kernel(lf_ref, o_ref, *, head_dim):
    x = lf_ref[...]
    n_heads, length = x.shape
    pos = lax.broadcasted_iota(jnp.int32, x.shape, 1)
    d = 1
    while d < length:
        x = x + jnp.where(pos >= d, pltpu.roll(x, d, axis=1), 0.0)
        d *= 2
    pieces = [p.astype(F32) for p in _split3(-LOG2_E * x)]
    pad = jnp.zeros((head_dim - len(pieces), length), F32)
    for h in range(n_heads):
        blk = jnp.concatenate([p[h:h + 1] for p in pieces] + [pad], axis=0)
        o_ref[pl.ds(h * head_dim, head_dim), :] = blk.astype(BF16)


def _forget_bias_rows(lft, head_dim):
    batch, n_heads, length = lft.shape
    return pl.pallas_call(
        functools.partial(_forget_bias_kernel, head_dim=head_dim),
        grid=(batch,),
        in_specs=[pl.BlockSpec((None, n_heads, length), lambda b: (b, 0, 0))],
        out_specs=pl.BlockSpec((None, n_heads * head_dim, length), lambda b: (b, 0, 0)),
        out_shape=jax.ShapeDtypeStruct((batch, n_heads * head_dim, length), BF16),
        compiler_params=_params("arbitrary"),
        name="forget_bias_rows",
    )(lft)


def _fox_kernel(q_ref, kt_ref, vt_ref, cb_ref, o_ref, *, tile, kblock, head_dim):
    qi = pl.program_id(2)
    heads = q_ref.shape[1] // head_dim
    sub = tile // kblock
    row = lax.broadcasted_iota(jnp.int32, (tile, kblock), 0)
    col = lax.broadcasted_iota(jnp.int32, (tile, kblock), 1)
    head_of_lane = lax.broadcasted_iota(jnp.int32, q_ref.shape, 1) // head_dim
    q_pair = q_ref[...]
    ones_rows = jnp.ones((head_dim, kblock), BF16)
    owns = [head_of_lane == j for j in range(heads)]
    q_augs = [jnp.where(own, q_pair, jnp.ones_like(q_pair)) for own in owns]

    def block(ki, carry, diag=None):
        off = pl.multiple_of(ki * kblock, kblock)
        feats = [slice(j * head_dim, (j + 1) * head_dim) for j in range(heads)]
        order = lambda j, data, aux: [data, aux] if j == 0 else [aux, data]
        scores = []
        for j in range(heads):
            kt_aug = jnp.concatenate(
                order(j, kt_ref[feats[j], pl.ds(off, kblock)], cb_ref[feats[j], pl.ds(off, kblock)]), axis=0)
            scores.append(_dot(q_augs[j], kt_aug))
        new = []
        for j, (m, acc) in enumerate(carry):
            s = scores[j] if diag is None else jnp.where(col + diag * kblock <= row, scores[j], -jnp.inf)
            m_new = jnp.maximum(m, jnp.max(s, axis=1, keepdims=True))
            p = jnp.exp2(s - m_new).astype(BF16)
            vt_aug = jnp.concatenate(order(j, vt_ref[feats[j], pl.ds(off, kblock)], ones_rows), axis=0)
            new.append((m_new, jnp.exp2(m - m_new) * acc + _dot_nt(p, vt_aug)))
        return tuple(new)

    init = tuple((jnp.full((tile, 1), -jnp.inf, F32), jnp.zeros(q_ref.shape, F32)) for _ in range(heads))
    carry = lax.fori_loop(0, qi * sub, block, init)
    for r in range(sub):
        carry = block(qi * sub + r, carry, diag=r)
    out = None
    for own, (_, acc) in zip(owns, carry):
        o = acc / pltpu.roll(acc, head_dim, axis=1)
        out = o if out is None else jnp.where(own, o, out)
    o_ref[...] = out.astype(BF16)


def _fox_attention(qb, ktb, vtb, cb, *, head_dim, tile, kblock):
    t, a_width = qb.shape
    batch, _, seq_len = ktb.shape
    nq = seq_len // tile
    q_spec = pl.BlockSpec((tile, LANES), lambda b, hp, qi: (b * nq + qi, hp))
    kv_spec = pl.BlockSpec((None, LANES, seq_len), lambda b, hp, qi: (b, hp, 0))
    return pl.pallas_call(
        functools.partial(_fox_kernel, tile=tile, kblock=kblock, head_dim=head_dim),
        grid=(batch, a_width // LANES, nq),
        in_specs=[q_spec, kv_spec, kv_spec, kv_spec],
        out_specs=q_spec,
        out_shape=jax.ShapeDtypeStruct((t, a_width), BF16),
        compiler_params=_params("arbitrary", "arbitrary", "arbitrary"),
        name="fox_attention",
    )(qb, ktb, vtb, cb)


def _paged_attn_kernel(pt_ref, q_ref, knew_ref, vnew_ref, lfnew_ref, tri_ref, *refs,
                       n_pages, n_heads, head_dim):
    del pt_ref
    k_refs = refs[:n_pages]
    v_refs = refs[n_pages:2 * n_pages]
    lf_refs = refs[2 * n_pages:3 * n_pages]
    o_ref, qbd_ref, m_ref, l_ref, acc_ref, pref_ref = refs[3 * n_pages:]
    step = pl.program_id(1)
    nq, width = q_ref.shape
    rows = nq * n_heads
    page = k_refs[0].shape[1]
    head_of_lane = lax.broadcasted_iota(jnp.int32, (n_heads, width), 1) // head_dim
    own = head_of_lane == lax.broadcasted_iota(jnp.int32, (n_heads, width), 0)

    @pl.when(step == 0)
    def _():
        q = q_ref[...]
        for qq in range(nq):
            blk = jnp.where(own, jnp.broadcast_to(q[qq:qq + 1, :], (n_heads, width)), 0.0)
            qbd_ref[pl.ds(qq * n_heads, n_heads), :] = blk
        m_ref[...] = jnp.full(m_ref.shape, -jnp.inf, F32)
        l_ref[...] = jnp.zeros(l_ref.shape, F32)
        acc_ref[...] = jnp.zeros(acc_ref.shape, F32)
        pref_ref[...] = jnp.zeros(pref_ref.shape, F32)

    def attend(k_list, v_list, lf_list, causal):
        n = len(k_list)
        ys = [lf[...] for lf in lf_list]
        y = jnp.concatenate(ys, axis=0) if n > 1 else ys[0]
        pieces = jnp.concatenate(_split3(y), axis=0)
        cw = _dot(pieces, tri_ref[...])
        nh = n * n_heads
        cw = cw[:nh] + cw[nh:2 * nh] + cw[2 * nh:]
        qbd = qbd_ref[...].astype(BF16)
        pref = pref_ref[...]
        s_list = []
        for g in range(n):
            cg = cw[g * n_heads:(g + 1) * n_heads]
            c = cg + pref
            pref = pref + cg[:, page - 1:page]
            s = _dot(qbd, k_list[g][...].astype(BF16))
            s = s - jnp.concatenate([c] * nq, axis=0)
            if causal:
                key = lax.broadcasted_iota(jnp.int32, (rows, page), 1)
                qpos = lax.broadcasted_iota(jnp.int32, (rows, page), 0) // n_heads
                s = jnp.where(key <= qpos, s, -jnp.inf)
            s_list.append(s)
        pref_ref[...] = pref
        s_all = jnp.concatenate(s_list, axis=1) if n > 1 else s_list[0]
        m = m_ref[...]
        m_new = jnp.maximum(m, jnp.max(s_all, axis=1, keepdims=True))
        alpha = jnp.exp(m - m_new)
        p = jnp.exp(s_all - m_new)
        l_ref[...] = alpha * l_ref[...] + jnp.sum(p, axis=1, keepdims=True)
        pv = None
        for g in range(n):
            part = _dot_nt(p[:, g * page:(g + 1) * page].astype(BF16), v_list[g][...].astype(BF16))
            pv = part if pv is None else pv + part
        acc_ref[...] = alpha * acc_ref[...] + pv
        m_ref[...] = m_new

    attend(k_refs, v_refs, lf_refs, causal=False)

    @pl.when(step == pl.num_programs(1) - 1)
    def _():
        attend([knew_ref], [vnew_ref], [lfnew_ref], causal=True)
        o = acc_ref[...] / l_ref[...]
        out_rows = []
        for qq in range(nq):
            blk = jnp.where(own, o[qq * n_heads:(qq + 1) * n_heads, :], 0.0)
            out_rows.append(jnp.sum(blk, axis=0, keepdims=True))
        o_ref[...] = jnp.concatenate(out_rows, axis=0).astype(o_ref.dtype)


def _paged_attention(q, kt_new, vt_new, lf_new, kt_pool, vt_pool, lf_pool, page_table, *, head_dim):
    batch, nq, width = q.shape
    page = kt_pool.shape[2]
    n_heads = lf_pool.shape[1]
    n_pages = PAGES_PER_STEP
    steps = page_table.shape[1] // n_pages
    rows = nq * n_heads
    tri = (jnp.arange(page)[:, None] <= jnp.arange(page)[None, :]).astype(BF16)

    per_batch = lambda shape: pl.BlockSpec((None,) + shape, lambda b, s, pt: (b, 0, 0))

    def paged(shape, g):
        return pl.BlockSpec((None,) + shape, lambda b, s, pt: (pt[b, s * n_pages + g], 0, 0))

    in_specs = [per_batch((nq, width)), per_batch((width, page)), per_batch((width, page)),
                per_batch((n_heads, page)), pl.BlockSpec(tri.shape, lambda b, s, pt: (0, 0))]
    in_specs += [paged((width, page), g) for g in range(n_pages)]
    in_specs += [paged((width, page), g) for g in range(n_pages)]
    in_specs += [paged((n_heads, page), g) for g in range(n_pages)]
    grid_spec = pltpu.PrefetchScalarGridSpec(
        num_scalar_prefetch=1,
        grid=(batch, steps),
        in_specs=in_specs,
        out_specs=per_batch((nq, width)),
        scratch_shapes=[pltpu.VMEM((rows, width), F32), pltpu.VMEM((rows, 1), F32),
                        pltpu.VMEM((rows, 1), F32), pltpu.VMEM((rows, width), F32),
                        pltpu.VMEM((n_heads, 1), F32)],
    )
    return pl.pallas_call(
        functools.partial(_paged_attn_kernel, n_pages=n_pages, n_heads=n_heads, head_dim=head_dim),
        grid_spec=grid_spec,
        out_shape=jax.ShapeDtypeStruct((batch, nq, width), BF16),
        compiler_params=_params("arbitrary", "arbitrary"),
        name="paged_attention",
    )(page_table, q, kt_new, vt_new, lf_new, tri,
      *([kt_pool] * n_pages), *([vt_pool] * n_pages), *([lf_pool] * n_pages))


def _out_proj_kernel(*refs, n_parts):
    parts = refs[:n_parts]
    w_ref, x_ref, g_ref, o_ref = refs[n_parts:]
    mix = None
    k0 = 0
    for a_ref in parts:
        kw = a_ref.shape[1]
        part = _dot(a_ref[...], w_ref[k0:k0 + kw, :])
        mix = part if mix is None else mix + part
        k0 += kw
    o_ref[...] = x_ref[...] + _rms(mix, g_ref[...])


def _out_proj(parts, w, x, gain, *, tm):
    t, d = x.shape
    row = lambda c: pl.BlockSpec((tm, c), lambda i: (i, 0))
    return pl.pallas_call(
        functools.partial(_out_proj_kernel, n_parts=len(parts)),
        grid=(t // tm,),
        in_specs=[row(p.shape[1]) for p in parts] + [_const_spec(w), row(d), _const_spec(gain)],
        out_specs=row(d),
        out_shape=jax.ShapeDtypeStruct((t, d), F32),
        compiler_params=_params("arbitrary"),
        name="out_proj",
    )(*parts, w, x, gain)


def _c_proj_kernel(*refs, n_heads, key_dim, seq_len, tiles_per_seq, n_over):
    x_ref, g_ref, wqkv_ref, wz_ref, wba_ref, taps_ref, alog_ref, dtb_ref = refs[:8]
    over = list(refs[8:8 + n_over]) if n_over else None
    (q_ref, k_ref, v_ref, z_ref, beta_ref, gdec_ref, u_ref, ext_ref) = refs[8 + n_over:8 + n_over + 8]
    y_ref = None if over else refs[-1]
    qk_width = n_heads * key_dim
    conv_width = wqkv_ref.shape[1]

    h = _rms(x_ref[...], g_ref[...]).astype(BF16)
    tm = h.shape[0]
    first = pl.program_id(0) % tiles_per_seq == 0 if over is None else pl.program_id(0) == 0
    _conv_prologue(ext_ref, first, conv_width)

    cw = CONV_CHUNK if over is None else conv_width
    nxt = _dot(h, wqkv_ref[:, :cw])
    for c0 in range(0, conv_width, cw):
        cur = nxt
        if c0 + cw < conv_width:
            nxt = _dot(h, wqkv_ref[:, c0 + cw:c0 + 2 * cw])
        y = _silu(_causal_conv(cur, taps_ref, ext_ref, over, seq_len, y_ref, c0))
        if over is None:
            tail = cur[tm - SUBLANES:, :]
            u_ref[:, c0:c0 + cw] = tail
            _conv_carry(ext_ref, tail, c0)
        else:
            u_ref[...] = cur
        for b0 in range(0, cw, key_dim):
            col = c0 + b0
            blk = y[:, b0:b0 + key_dim]
            if col < 2 * qk_width:
                blk = blk * lax.rsqrt(jnp.sum(blk * blk, axis=-1, keepdims=True) + NORM_EPS)
            if col < qk_width:
                q_ref[:, col:col + key_dim] = (blk * key_dim ** -0.5).astype(BF16)
            elif col < 2 * qk_width:
                k_ref[:, col - qk_width:col - qk_width + key_dim] = blk.astype(BF16)
            else:
                v_ref[:, col - 2 * qk_width:col - 2 * qk_width + key_dim] = blk.astype(BF16)

    z_ref[...] = _dot(h, wz_ref[...]).astype(BF16)
    ba = _dot(h, wba_ref[...])
    beta_ref[...] = jax.nn.sigmoid(ba)[:, :n_heads]
    gdec = -jnp.exp(alog_ref[...]) * _softplus(ba + dtb_ref[...])
    gdec_ref[...] = gdec[:, n_heads:2 * n_heads]


def _c_proj(x, gain, wqkv, wz, wba, taps, alog, dtb, overrides, *, n_heads, key_dim, seq_len, tm):
    t, d = x.shape
    conv_width = wqkv.shape[1]
    qk_width = n_heads * key_dim
    v_width = conv_width - 2 * qk_width
    n_tiles = t // tm
    n_over = 0 if overrides is None else len(overrides)
    row = lambda c: pl.BlockSpec((tm, c), lambda i: (i, 0))
    in_specs = [row(d)] + [_const_spec(a) for a in (gain, wqkv, wz, wba, taps, alog, dtb)]
    in_specs += [row(conv_width)] * n_over
    if overrides is None:
        u_spec = pl.BlockSpec((None, SUBLANES, conv_width), lambda i: (i, 0, 0))
        u_shape = jax.ShapeDtypeStruct((n_tiles, SUBLANES, conv_width), F32)
    else:
        u_spec = row(conv_width)
        u_shape = jax.ShapeDtypeStruct((t, conv_width), F32)
    out_specs = [row(qk_width), row(qk_width), row(v_width), row(v_width), row(n_heads), row(n_heads), u_spec]
    out_shape = [jax.ShapeDtypeStruct((t, qk_width), BF16), jax.ShapeDtypeStruct((t, qk_width), BF16),
                 jax.ShapeDtypeStruct((t, v_width), BF16), jax.ShapeDtypeStruct((t, v_width), BF16),
                 jax.ShapeDtypeStruct((t, n_heads), F32), jax.ShapeDtypeStruct((t, n_heads), F32), u_shape]
    return pl.pallas_call(
        functools.partial(_c_proj_kernel, n_heads=n_heads, key_dim=key_dim, seq_len=seq_len,
                          tiles_per_seq=max(seq_len // tm, 1), n_over=n_over),
        grid=(n_tiles,),
        in_specs=in_specs,
        out_specs=out_specs,
        out_shape=out_shape,
        scratch_shapes=_conv_scratch(tm, conv_width, overrides is None),
        compiler_params=_params("arbitrary"),
        name="c_proj",
    )(x, gain, wqkv, wz, wba, taps, alog, dtb, *(overrides or []))


def _delta_kernel(q_ref, k_ref, v_ref, z_ref, gcol_ref, beta_ref, grow_ref, s0_ref, nw_ref,
                  o_ref, s_out_ref, m_ref, *, n_heads, chunk):
    step = pl.program_id(1)
    rows = q_ref.shape[0]
    dk = q_ref.shape[1] // n_heads
    dv = v_ref.shape[1] // n_heads
    n_chunks = rows // chunk
    ri = lax.broadcasted_iota(jnp.int32, (chunk, chunk), 0)
    ci = lax.broadcasted_iota(jnp.int32, (chunk, chunk), 1)
    incl = (ci <= ri)[None]
    strict = (ci < ri)[None]

    @pl.when(step == 0)
    def _():
        for h in range(n_heads):
            m_ref[h] = s0_ref[h].T

    probs = [(c, h) for c in range(n_chunks) for h in range(n_heads)]
    rs = lambda c: slice(c * chunk, (c + 1) * chunk)
    stack = lambda f: jnp.stack([f(c, h) for c, h in probs])
    kb = stack(lambda c, h: k_ref[rs(c), h * dk:(h + 1) * dk])
    qb = stack(lambda c, h: q_ref[rs(c), h * dk:(h + 1) * dk])
    vf = stack(lambda c, h: v_ref[rs(c), h * dv:(h + 1) * dv]).astype(F32)
    gc = stack(lambda c, h: gcol_ref[rs(c), h:h + 1])
    gr = stack(lambda c, h: grow_ref[h:h + 1, rs(c)])
    bc = stack(lambda c, h: beta_ref[rs(c), h:h + 1])
    kf = kb.astype(F32)
    eg = jnp.exp(gc)
    decay = jnp.where(incl, jnp.exp(jnp.where(incl, gc - gr, 0.0)), 0.0)
    kqk = _bdot_nt(jnp.concatenate([kb, qb], axis=1), kb)
    a = jnp.where(strict, bc * kqk[:, :chunk] * decay, 0.0)
    nil = -a
    pw = a
    span = 2
    while span < chunk:
        pwb = pw.astype(BF16)
        pw = _bdot(pwb, pwb)
        nil = nil + pw + _bdot(nil.astype(BF16), pw.astype(BF16))
        span *= 2
    rhs = jnp.concatenate([bc * vf, (bc * eg) * kf], axis=2)
    w = rhs + _bdot(nil.astype(BF16), rhs.astype(BF16))
    w_v = w[:, :, :dv]
    p = (kqk[:, chunk:] * decay).astype(BF16)
    g_last = gc[:, chunk - 1:chunk, :]
    lhs = jnp.concatenate([w[:, :, dv:], eg * qb.astype(F32)], axis=1).astype(BF16)
    k_dec = (jnp.exp(g_last - gc) * kf).astype(BF16)
    g_chunk = jnp.exp(g_last)

    for c in range(n_chunks):
        hs = slice(c * n_heads, (c + 1) * n_heads)
        m = m_ref[...]
        r = _bdot(lhs[hs], m.astype(BF16))
        ub = (w_v[hs] - r[:, :chunk]).astype(BF16)
        o = r[:, chunk:] + _bdot(p[hs], ub)
        m_ref[...] = g_chunk[hs] * m + _bdot_tn(k_dec[hs], ub)
        on = o * lax.rsqrt(jnp.mean(o * o, axis=-1, keepdims=True) + NORM_EPS) * nw_ref[...]
        for h in range(n_heads):
            vl = slice(h * dv, (h + 1) * dv)
            o_ref[rs(c), vl] = (on[h] * _silu(z_ref[rs(c), vl].astype(F32))).astype(BF16)

    @pl.when(step == pl.num_programs(1) - 1)
    def _():
        for h in range(n_heads):
            s_out_ref[h] = m_ref[h].T


def _delta_rule(q, k, v, z, gcol, beta, grow, s0, norm_w, *, batch, seq_len, n_heads, rows):
    t = q.shape[0]
    steps = seq_len // rows
    dk = q.shape[1] // n_heads
    dv = v.shape[1] // n_heads
    row = lambda c: pl.BlockSpec((rows, c), lambda b, s: (b * steps + s, 0))
    state = pl.BlockSpec((None, n_heads, dv, dk), lambda b, s: (b, 0, 0, 0))
    return pl.pallas_call(
        functools.partial(_delta_kernel, n_heads=n_heads, chunk=DELTA_CHUNK),
        grid=(batch, steps),
        in_specs=[row(q.shape[1]), row(k.shape[1]), row(v.shape[1]), row(z.shape[1]),
                  row(n_heads), row(n_heads),
                  pl.BlockSpec((None, n_heads, rows), lambda b, s: (b, 0, s)),
                  state, pl.BlockSpec(norm_w.shape, lambda b, s: (0, 0))],
        out_specs=[row(v.shape[1]), state],
        out_shape=[jax.ShapeDtypeStruct((t, v.shape[1]), BF16),
                   jax.ShapeDtypeStruct((batch, n_heads, dv, dk), F32)],
        scratch_shapes=[pltpu.VMEM((n_heads, dk, dv), F32)],
        compiler_params=_params("arbitrary", "arbitrary"),
        name="gated_delta_rule",
    )(q, k, v, z, gcol, beta, grow, s0, norm_w)


def _pad_lanes(a):
    return jnp.pad(a, ((0, 0), (0, LANES - a.shape[1])))


def _rows_to_heads(a, batch, seq_len):
    return a.reshape(batch, seq_len, a.shape[1]).transpose(0, 2, 1)


def _heads_to_rows(a):
    b, h, length = a.shape
    return a.transpose(0, 2, 1).reshape(b * length, h)


def _feature_major_pool(pool):
    n_l, n_p, page = pool.shape[:3]
    perm = (0, 1) + tuple(range(3, pool.ndim)) + (2,)
    return pool.transpose(perm).reshape(n_l * n_p, -1, page)


def _mixer_ab(x, past, w, i, gain, *, batch, seq_len, tm):
    t = x.shape[0]
    hd = w["a_head_dim"]
    nh = w["ab_b_f"].shape[1]
    aw = nh * hd
    w_in = w["ab_w_in"][i]
    bw = (w_in.shape[1] - 3 * aw - nh) // 3
    wb = w_in[:, 3 * aw + nh:].astype(BF16)
    wf = w_in[:, 3 * aw:3 * aw + nh]
    taps = w["ab_conv_w"][i]
    n_keep = taps.shape[0] - 1
    if past is None:
        wq = w_in[:, :aw].astype(BF16)
        wkvt = w_in[:, aw:3 * aw].T.astype(BF16)
        kt, vt, ktb, vtb, lft, qb, ob, tails = _ab_proj_prompt(
            x, gain, wq, wkvt, wf.T.astype(BF16), w["ab_b_f"][i][:, None], wb, taps,
            batch=batch, seq_len=seq_len, head_dim=hd, tm=tm)
        o_a = _fox_attention(qb, ktb, vtb, _forget_bias_rows(lft, hd), head_dim=hd,
                             tile=min(1024, seq_len), kblock=min(512, seq_len))
        tails = tails.reshape(batch, seq_len // tm, SUBLANES, bw)[:, -1]
        new_buf = tails[:, SUBLANES - n_keep:, :]
        to_tokens = lambda a: a.reshape(batch, nh, hd, seq_len).transpose(0, 3, 1, 2)
        state = (to_tokens(kt), to_tokens(vt), lft.transpose(0, 2, 1), new_buf)
    else:
        overrides = _conv_overrides(past["state_b_conv"][i], seq_len)
        q_s, k32, v32, logf, ob, gcu = _ab_proj_sample(
            x, gain, w_in[:, :3 * aw].astype(BF16), _pad_lanes(wf).astype(BF16),
            _pad_lanes(w["ab_b_f"][i][None, :]), wb, taps, overrides,
            n_heads=nh, head_dim=hd, seq_len=seq_len)
        n_phys, page = past["cache_a_k"].shape[1:3]
        pad_keys = lambda a: jnp.pad(a.reshape(batch, seq_len, -1).transpose(0, 2, 1),
                                     ((0, 0), (0, 0), (0, page - seq_len)))
        o_a = _paged_attention(
            q_s.reshape(batch, seq_len, aw), pad_keys(k32), pad_keys(v32), pad_keys(logf),
            _feature_major_pool(past["cache_a_k"]), _feature_major_pool(past["cache_a_v"]),
            _feature_major_pool(past["cache_a_logf"]), past["page_table"] + i * n_phys, head_dim=hd)
        o_a = o_a.reshape(t, aw)
        new_buf = gcu.reshape(batch, seq_len, bw)[:, seq_len - n_keep:, :]
        state = (k32.reshape(batch, seq_len, nh, hd), v32.reshape(batch, seq_len, nh, hd),
                 logf.reshape(batch, seq_len, nh), new_buf)
    return [o_a, ob], w["ab_w_out"][i].astype(BF16), state


def _mixer_c(x, past, w, i, gain, *, batch, seq_len, tm):
    t = x.shape[0]
    nh = w["c_a_log"].shape[1]
    dv = w["c_norm_w"].shape[1]
    w_in = w["c_w_in"][i]
    v_width = nh * dv
    conv_width = w_in.shape[1] - v_width - 2 * nh
    dk = (conv_width - v_width) // 2 // nh
    wqkv = w_in[:, :conv_width].astype(BF16)
    wz = w_in[:, conv_width:conv_width + v_width].astype(BF16)
    wba = _pad_lanes(w_in[:, conv_width + v_width:]).astype(BF16)
    zeros_h = jnp.zeros((1, nh), F32)
    alog = _pad_lanes(jnp.concatenate([zeros_h, w["c_a_log"][i][None, :]], axis=1))
    dtb = _pad_lanes(jnp.concatenate([zeros_h, w["c_dt_bias"][i][None, :]], axis=1))
    taps = w["c_conv_w"][i]
    n_keep = taps.shape[0] - 1
    tm_c = min(tm, 256)
    if past is None:
        overrides = None
        s0 = jnp.zeros((batch, nh, dv, dk), F32)
    else:
        overrides = _conv_overrides(past["state_c_conv"][i], seq_len)
        s0 = past["state_c_S"][i]
    qn, kn, vn, zb, beta, gdec, u_out = _c_proj(
        x, gain, wqkv, wz, wba, taps, alog, dtb, overrides,
        n_heads=nh, key_dim=dk, seq_len=seq_len, tm=tm_c)
    if past is None:
        tails = u_out.reshape(batch, seq_len // tm_c, SUBLANES, conv_width)[:, -1]
        new_buf = tails[:, SUBLANES - n_keep:, :]
        len_pad = seq_len
    else:
        new_buf = u_out.reshape(batch, seq_len, conv_width)[:, seq_len - n_keep:, :]
        len_pad = -(-seq_len // DELTA_CHUNK) * DELTA_CHUNK
        pad3 = lambda a: jnp.pad(a.reshape(batch, seq_len, a.shape[1]),
                                 ((0, 0), (0, len_pad - seq_len), (0, 0))).reshape(batch * len_pad, a.shape[1])
        qn, kn, vn, zb, beta, gdec = [pad3(a) for a in (qn, kn, vn, zb, beta, gdec)]
    scan_len = -(-len_pad // LANES) * LANES
    g_heads = jnp.pad(_rows_to_heads(gdec, batch, len_pad), ((0, 0), (0, 0), (0, scan_len - len_pad)))
    grow = _segment_cumsum(g_heads, DELTA_CHUNK)[:, :, :len_pad]
    og, s_new = _delta_rule(qn, kn, vn, zb, _heads_to_rows(grow), beta, grow, s0, w["c_norm_w"][i][None, :],
                            batch=batch, seq_len=len_pad, n_heads=nh, rows=min(4 * DELTA_CHUNK, len_pad))
    if len_pad != seq_len:
        og = og.reshape(batch, len_pad, v_width)[:, :seq_len].reshape(t, v_width)
    return [og], w["c_w_out"][i].astype(BF16), (new_buf, s_new)


def _run_trunk(x3, past, w, *, tm):
    batch, seq_len, d = x3.shape
    x = x3.reshape(batch * seq_len, d)
    ab_states, c_states = [], []
    ffn = lambda x, g0, g1, layer, j: _ffn_block(
        x, g0, g1, w["ffn_w_gate"], w["ffn_w_up"], w["ffn_w_down"], layer, j, tm=tm)
    for layer in range(w["norm_g"].shape[0]):
        gains = [w["norm_g"][layer, j][None, :] for j in range(6)]
        x = ffn(x, gains[0], gains[1], layer, 0)
        mixer, states = (_mixer_ab, ab_states) if layer % 2 == 0 else (_mixer_c, c_states)
        parts, w_out, state = mixer(x, past, w, layer // 2, gains[2], batch=batch, seq_len=seq_len, tm=tm)
        states.append(state)
        x = _out_proj(parts, w_out, x, gains[3], tm=tm)
        x = ffn(x, gains[4], gains[5], layer, 1)
    ab_new = [jnp.stack(ts) for ts in zip(*ab_states)]
    c_new = [jnp.stack(ts) for ts in zip(*c_states)]
    return x.reshape(batch, seq_len, d), ab_new, c_new


def kernel(x_prompt, x_sample, cache_a_k, cache_a_v, cache_a_logf, page_table, state_b_conv, state_c_conv, state_c_S, norm_g, ffn_w_gate, ffn_w_up, ffn_w_down, ab_w_in, ab_b_f, ab_conv_w, ab_w_out, c_w_in, c_conv_w, c_a_log, c_dt_bias, c_norm_w, c_w_out):
    w = dict(norm_g=norm_g,
             ffn_w_gate=ffn_w_gate.astype(BF16), ffn_w_up=ffn_w_up.astype(BF16), ffn_w_down=ffn_w_down.astype(BF16),
             ab_w_in=ab_w_in, ab_b_f=ab_b_f, ab_conv_w=ab_conv_w, ab_w_out=ab_w_out,
             c_w_in=c_w_in, c_conv_w=c_conv_w, c_a_log=c_a_log, c_dt_bias=c_dt_bias,
             c_norm_w=c_norm_w, c_w_out=c_w_out, a_head_dim=cache_a_k.shape[-1])
    past = dict(cache_a_k=cache_a_k, cache_a_v=cache_a_v, cache_a_logf=cache_a_logf, page_table=page_table,
                state_b_conv=state_b_conv, state_c_conv=state_c_conv, state_c_S=state_c_S)
    prompt_rows = x_prompt.shape[0] * x_prompt.shape[1]
    sample_rows = x_sample.shape[0] * x_sample.shape[1]
    y_prompt, ab_p, c_p = _run_trunk(x_prompt, None, w, tm=min(512, prompt_rows))
    y_sample, ab_s, c_s = _run_trunk(x_sample, past, w, tm=min(512, sample_rows))
    a_k_prompt, a_v_prompt, a_logf_prompt, b_conv_prompt = ab_p
    a_k_sample, a_v_sample, a_logf_sample, b_conv_sample = ab_s
    c_conv_prompt, c_S_prompt = c_p
    c_conv_sample, c_S_sample = c_s
    return (y_prompt, y_sample,
            a_k_prompt, a_v_prompt, a_logf_prompt,
            a_k_sample, a_v_sample, a_logf_sample,
            b_conv_prompt, b_conv_sample,
            c_conv_prompt, c_conv_sample,
            c_S_prompt, c_S_sample)
```

```python
import functools

import jax
import jax.numpy as jnp
from jax import lax
from jax.experimental import pallas as pl
from jax.experimental.pallas import tpu as pltpu

F32 = jnp.float32
BF16 = jnp.bfloat16
NORM_EPS = 1e-6
LOG2_E = 1.4426950408889634

LANES = 128
SUBLANES = 8
VMEM_LIMIT_BYTES = 56 * 1024 * 1024
DELTA_CHUNK = 64
DELTA_ROWS = 512
CONV_ROW_STRIDE = 4
CONV_CHUNK = 512
PAGES_PER_STEP = 16
PAGE_GROUPS = 2


def _params(*semantics):
    return pltpu.CompilerParams(dimension_semantics=semantics, vmem_limit_bytes=VMEM_LIMIT_BYTES)


def _resident(shape, index_map):
    return pl.BlockSpec(shape, index_map, pipeline_mode=pl.Buffered(1))


def _const_spec(a):
    return _resident(a.shape, lambda *_: (0,) * a.ndim)


def _rms(x, g):
    return x * lax.rsqrt(jnp.mean(x * x, axis=-1, keepdims=True) + NORM_EPS) * g


def _softplus(x):
    return jnp.maximum(x, 0.0) + jnp.log1p(jnp.exp(-jnp.abs(x)))


def _silu(x):
    return x * jax.nn.sigmoid(x)


def _dot(a, b):
    return jnp.dot(a, b, preferred_element_type=F32)


def _dot_nt(a, b):
    return lax.dot_general(a, b, (((1,), (1,)), ((), ())), preferred_element_type=F32)


def _dot_tn(a, b):
    return lax.dot_general(a, b, (((0,), (0,)), ((), ())), preferred_element_type=F32)


def _bdot(a, b):
    return lax.dot_general(a, b, (((2,), (1,)), ((0,), (0,))), preferred_element_type=F32)


def _bdot_nt(a, b):
    return lax.dot_general(a, b, (((2,), (2,)), ((0,), (0,))), preferred_element_type=F32)


def _bdot_tn(a, b):
    return lax.dot_general(a, b, (((1,), (1,)), ((0,), (0,))), preferred_element_type=F32)


def _mix_rows(part_refs, w_ref):
    mix = None
    k0 = 0
    for a_ref in part_refs:
        kw = a_ref.shape[1]
        part = _dot(a_ref[...], w_ref[k0:k0 + kw, :])
        mix = part if mix is None else mix + part
        k0 += kw
    return mix


def _ffn_kernel(*refs, n_parts, ff_chunk):
    part_refs = refs[:n_parts]
    if n_parts:
        w_mix_ref, g_mix_ref = refs[n_parts:n_parts + 2]
        refs = refs[n_parts + 2:]
    x_ref, gpre_ref, gpost_ref, wg_ref, wu_ref, wd_ref, o_ref = refs
    x = x_ref[...]
    if n_parts:
        x = x + _rms(_mix_rows(part_refs, w_mix_ref), g_mix_ref[...])
    h = _rms(x, gpre_ref[...]).astype(BF16)
    d_ff = wg_ref.shape[1]
    proj = lambda c0: (_dot(h, wg_ref[:, c0:c0 + ff_chunk]), _dot(h, wu_ref[:, c0:c0 + ff_chunk]))
    y = None
    nxt = proj(0)
    for c0 in range(0, d_ff, ff_chunk):
        gate, up = nxt
        if c0 + ff_chunk < d_ff:
            nxt = proj(c0 + ff_chunk)
        act = (_silu(gate) * up).astype(BF16)
        part = _dot(act, wd_ref[c0:c0 + ff_chunk, :])
        y = part if y is None else y + part
    o_ref[...] = x + 0.5 * _rms(y, gpost_ref[...])


def _ffn_chunk(d_ff):
    for c in range(4 * LANES, 0, -LANES):
        if d_ff % c == 0:
            return c
    return d_ff


def _ffn_block(x, g_pre, g_post, wg_all, wu_all, wd_all, layer, j, *, tm, mixer=None):
    t, d = x.shape
    d_ff = wg_all.shape[-1]
    row = lambda c: pl.BlockSpec((tm, c), lambda i: (i, 0))
    gain = _resident((1, d), lambda i: (0, 0))
    w_in = _resident((None, None, d, d_ff), lambda i: (layer, j, 0, 0))
    w_out = _resident((None, None, d_ff, d), lambda i: (layer, j, 0, 0))
    parts, w_mix, g_mix = mixer if mixer is not None else ([], None, None)
    mix_args = list(parts) + ([w_mix, g_mix] if parts else [])
    mix_specs = [row(p.shape[1]) for p in parts] + ([_const_spec(w_mix), gain] if parts else [])
    return pl.pallas_call(
        functools.partial(_ffn_kernel, n_parts=len(parts), ff_chunk=_ffn_chunk(d_ff)),
        grid=(t // tm,),
        in_specs=mix_specs + [row(d), gain, gain, w_in, w_in, w_out],
        out_specs=row(d),
        out_shape=jax.ShapeDtypeStruct((t, d), F32),
        compiler_params=_params("arbitrary"),
        name="ffn_block",
    )(*mix_args, x, g_pre, g_post, wg_all, wu_all, wd_all)


def _causal_conv(u, taps_ref, ext_ref, overrides, seq_len, y_ref=None, col0=0):
    tm = u.shape[0]
    n_taps = taps_ref.shape[0]
    if overrides is None:
        stride = CONV_ROW_STRIDE
        group = SUBLANES * stride
        assert n_taps <= stride + 1 <= SUBLANES + 1
        blocks = range(col0 // LANES, (col0 + u.shape[1]) // LANES)
        for lb in blocks:
            lanes = slice(lb * LANES, (lb + 1) * LANES)
            ext_ref[lb, pl.ds(SUBLANES, tm), :] = u[:, lb * LANES - col0:(lb + 1) * LANES - col0]
            taps = [taps_ref[n_taps - 1 - j:n_taps - j, lanes] for j in range(n_taps)]
            for g0 in range(0, tm, group):
                win = {d: ext_ref[lb, pl.ds(SUBLANES + g0 + d, SUBLANES, stride=stride), :]
                       for d in range(1 - n_taps, stride)}
                for i in range(stride):
                    acc = win[i] * taps[0]
                    for j in range(1, n_taps):
                        acc = acc + win[i - j] * taps[j]
                    y_ref[lb, pl.ds(g0 + i, SUBLANES, stride=stride), :] = acc
        return jnp.concatenate([y_ref[lb] for lb in blocks], axis=1)
    ext_ref[pl.ds(SUBLANES, tm), :] = u
    out = u * taps_ref[n_taps - 1:n_taps, :]
    pos = lax.broadcasted_iota(jnp.int32, u.shape, 0) % seq_len
    for j in range(1, n_taps):
        prev = jnp.where(pos >= j, ext_ref[pl.ds(SUBLANES - j, tm), :], overrides[j - 1][...])
        out = out + prev * taps_ref[n_taps - 1 - j:n_taps - j, :]
    return out


def _conv_overrides(state, seq_len):
    b, wm1, c = state.shape
    assert seq_len >= wm1
    outs = []
    for j in range(1, wm1 + 1):
        rows = jnp.pad(state[:, wm1 - j:, :], ((0, 0), (0, seq_len - j), (0, 0)))
        outs.append(rows.reshape(b * seq_len, c))
    return outs


def _conv_carry(ext_ref, tail, col0=0):
    if len(ext_ref.shape) == 2:
        ext_ref[pl.ds(0, SUBLANES), :] = tail
    else:
        for lb in range(tail.shape[1] // LANES):
            ext_ref[col0 // LANES + lb, pl.ds(0, SUBLANES), :] = tail[:, lb * LANES:(lb + 1) * LANES]


def _conv_prologue(ext_ref, first, width):
    @pl.when(first)
    def _():
        _conv_carry(ext_ref, jnp.zeros((SUBLANES, width), F32))


def _conv_scratch(tm, width, blocked):
    if not blocked:
        return [pltpu.VMEM((SUBLANES + tm, width), F32)]
    assert tm % (SUBLANES * CONV_ROW_STRIDE) == 0 and width % LANES == 0
    return [pltpu.VMEM((width // LANES, SUBLANES + tm, LANES), F32), pltpu.VMEM((width // LANES, tm, LANES), F32)]


def _log_sigmoid(z):
    return -_softplus(-z)


def _ab_proj_prompt_kernel(x_ref, g_ref, wq_ref, wkvt_ref, wft_ref, bf_ref, wb_ref, taps_ref,
                           kt_ref, vt_ref, ktb_ref, vtb_ref, lft_ref, qb_ref, ob_ref, tail_ref, ext_ref, y_ref,
                           *, head_dim, tiles_per_seq):
    a_width = wq_ref.shape[1]
    b_width = wb_ref.shape[1] // 3
    h = _rms(x_ref[...], g_ref[...]).astype(BF16)
    qb_ref[...] = (_dot(h, wq_ref[...]) * (head_dim ** -0.5 * LOG2_E)).astype(BF16)
    kvt = _dot_nt(wkvt_ref[...], h)
    kt = kvt[:a_width]
    vt = kvt[a_width:]
    kt_ref[...] = kt
    vt_ref[...] = vt
    ktb_ref[...] = kt.astype(BF16)
    vtb_ref[...] = vt.astype(BF16)
    lft_ref[...] = _log_sigmoid(_dot_nt(wft_ref[...], h) + bf_ref[...])

    gbu = _dot(h, wb_ref[...])
    gate_b = gbu[:, :b_width]
    gcu = gbu[:, b_width:2 * b_width] * gbu[:, 2 * b_width:]
    _conv_prologue(ext_ref, pl.program_id(0) % tiles_per_seq == 0, b_width)
    y = _causal_conv(gcu, taps_ref, ext_ref, None, None, y_ref)
    ob_ref[...] = (gate_b * y).astype(BF16)
    tail = gcu[gcu.shape[0] - SUBLANES:, :]
    tail_ref[...] = tail
    _conv_carry(ext_ref, tail)


def _ab_proj_prompt(x, gain, wq, wkvt, wft, bf_col, wb, taps, *, batch, seq_len, head_dim, tm):
    t, d = x.shape
    a_width = wq.shape[1]
    n_heads = wft.shape[0]
    b_width = wb.shape[1] // 3
    tiles_per_seq = seq_len // tm
    row = lambda c: pl.BlockSpec((tm, c), lambda i: (i, 0))
    feat = lambda c: pl.BlockSpec((None, c, tm), lambda i: (i // tiles_per_seq, 0, i % tiles_per_seq))
    feat_shape = lambda c, dt: jax.ShapeDtypeStruct((batch, c, seq_len), dt)
    return pl.pallas_call(
        functools.partial(_ab_proj_prompt_kernel, head_dim=head_dim, tiles_per_seq=tiles_per_seq),
        grid=(t // tm,),
        in_specs=[row(d)] + [_const_spec(a) for a in (gain, wq, wkvt, wft, bf_col, wb, taps)],
        out_specs=[feat(a_width), feat(a_width), feat(a_width), feat(a_width), feat(n_heads),
                   row(a_width), row(b_width),
                   pl.BlockSpec((None, SUBLANES, b_width), lambda i: (i, 0, 0))],
        out_shape=[feat_shape(a_width, F32), feat_shape(a_width, F32), feat_shape(a_width, BF16),
                   feat_shape(a_width, BF16), feat_shape(n_heads, F32),
                   jax.ShapeDtypeStruct((t, a_width), BF16), jax.ShapeDtypeStruct((t, b_width), BF16),
                   jax.ShapeDtypeStruct((t // tm, SUBLANES, b_width), F32)],
        scratch_shapes=_conv_scratch(tm, b_width, True),
        compiler_params=_params("arbitrary"),
        name="ab_proj_prompt",
    )(x, gain, wq, wkvt, wft, bf_col, wb, taps)


def _ab_proj_sample_kernel(*refs, n_heads, head_dim, seq_len, n_over):
    x_ref, g_ref, wqkv_ref, wf_ref, bf_ref, wb_ref, taps_ref = refs[:7]
    over = list(refs[7:7 + n_over])
    q_ref, k_ref, v_ref, logf_ref, ob_ref, u_ref, ext_ref = refs[7 + n_over:]
    a_width = n_heads * head_dim
    b_width = wb_ref.shape[1] // 3
    h = _rms(x_ref[...], g_ref[...]).astype(BF16)
    qkv = _dot(h, wqkv_ref[...])
    q_ref[...] = qkv[:, :a_width] * head_dim ** -0.5
    k_ref[...] = qkv[:, a_width:2 * a_width]
    v_ref[...] = qkv[:, 2 * a_width:]
    logf_ref[...] = _log_sigmoid(_dot(h, wf_ref[...]) + bf_ref[...])[:, :n_heads]

    gbu = _dot(h, wb_ref[...])
    gate_b = gbu[:, :b_width]
    gcu = gbu[:, b_width:2 * b_width] * gbu[:, 2 * b_width:]
    _conv_prologue(ext_ref, pl.program_id(0) == 0, b_width)
    y = _causal_conv(gcu, taps_ref, ext_ref, over, seq_len)
    ob_ref[...] = (gate_b * y).astype(BF16)
    u_ref[...] = gcu


def _ab_proj_sample(x, gain, wqkv, wf, bf, wb, taps, overrides, *, n_heads, head_dim, seq_len):
    t, d = x.shape
    a_width = n_heads * head_dim
    b_width = wb.shape[1] // 3
    full = lambda c: pl.BlockSpec((t, c), lambda i: (0, 0))
    f32 = lambda c: jax.ShapeDtypeStruct((t, c), F32)
    return pl.pallas_call(
        functools.partial(_ab_proj_sample_kernel, n_heads=n_heads, head_dim=head_dim, seq_len=seq_len,
                          n_over=len(overrides)),
        grid=(1,),
        in_specs=[full(d)] + [_const_spec(a) for a in (gain, wqkv, wf, bf, wb, taps)]
        + [full(b_width)] * len(overrides),
        out_specs=[full(a_width), full(a_width), full(a_width), full(n_heads), full(b_width), full(b_width)],
        out_shape=[f32(a_width), f32(a_width), f32(a_width), f32(n_heads),
                   jax.ShapeDtypeStruct((t, b_width), BF16), f32(b_width)],
        scratch_shapes=[pltpu.VMEM((SUBLANES + t, b_width), F32)],
        compiler_params=_params("arbitrary"),
        name="ab_proj_sample",
    )(x, gain, wqkv, wf, bf, wb, taps, *overrides)


def _scan_kernel(x_ref, o_ref, *, seg):
    x = x_ref[...]
    pos = lax.broadcasted_iota(jnp.int32, x.shape, 1) % seg
    d = 1
    while d < seg:
        x = x + jnp.where(pos >= d, pltpu.roll(x, d, axis=1), 0.0)
        d *= 2
    o_ref[...] = x


def _segment_cumsum(x, seg):
    b, h, length = x.shape
    spec = pl.BlockSpec((None, h, length), lambda i: (i, 0, 0))
    return pl.pallas_call(
        functools.partial(_scan_kernel, seg=seg),
        grid=(b,),
        in_specs=[spec],
        out_specs=spec,
        out_shape=jax.ShapeDtypeStruct(x.shape, F32),
        compiler_params=_params("arbitrary"),
        name="segment_cumsum",
    )(x)


def _split3(x):
    hi = x.astype(BF16)
    r = x - hi.astype(F32)
    mid = r.astype(BF16)
    lo = (r - mid.astype(F32)).astype(BF16)
    return hi, mid, lo


def _forget_bias_kernel(lf_ref, o_ref, *, head_dim):
    x = lf_ref[...]
    n_heads, length = x.shape
    pos = lax.broadcasted_iota(jnp.int32, x.shape, 1)
    d = 1
    while d < length:
        x = x + jnp.where(pos >= d, pltpu.roll(x, d, axis=1), 0.0)
        d *= 2
    pieces = [p.astype(F32) for p in _split3(-LOG2_E * x)]
    pad = jnp.zeros((head_dim - len(pieces), length), F32)
    for h in range(n_heads):
        blk = jnp.concatenate([p[h:h + 1] for p in pieces] + [pad], axis=0)
        o_ref[pl.ds(h * head_dim, head_dim), :] = blk.astype(BF16)


def _forget_bias_rows(lft, head_dim):
    batch, n_heads, length = lft.shape
    return pl.pallas_call(
        functools.partial(_forget_bias_kernel, head_dim=head_dim),
        grid=(batch,),
        in_specs=[pl.BlockSpec((None, n_heads, length), lambda b: (b, 0, 0))],
        out_specs=pl.BlockSpec((None, n_heads * head_dim, length), lambda b: (b, 0, 0)),
        out_shape=jax.ShapeDtypeStruct((batch, n_heads * head_dim, length), BF16),
        compiler_params=_params("arbitrary"),
        name="forget_bias_rows",
    )(lft)


def _fox_kernel(q_ref, kt_ref, vt_ref, cb_ref, o_ref, *, tile, kblock, head_dim):
    qi = pl.program_id(2)
    heads = q_ref.shape[1] // head_dim
    sub = tile // kblock
    row = lax.broadcasted_iota(jnp.int32, (tile, kblock), 0)
    col = lax.broadcasted_iota(jnp.int32, (tile, kblock), 1)
    head_of_lane = lax.broadcasted_iota(jnp.int32, q_ref.shape, 1) // head_dim
    q_pair = q_ref[...]
    ones_rows = jnp.ones((head_dim, kblock), BF16)
    owns = [head_of_lane == j for j in range(heads)]
    q_augs = [jnp.where(own, q_pair, jnp.ones_like(q_pair)) for own in owns]

    def block(ki, carry, diag=None):
        off = pl.multiple_of(ki * kblock, kblock)
        feats = [slice(j * head_dim, (j + 1) * head_dim) for j in range(heads)]
        order = lambda j, data, aux: [data, aux] if j == 0 else [aux, data]
        scores = []
        for j in range(heads):
            kt_aug = jnp.concatenate(
                order(j, kt_ref[feats[j], pl.ds(off, kblock)], cb_ref[feats[j], pl.ds(off, kblock)]), axis=0)
            scores.append(_dot(q_augs[j], kt_aug))
        new = []
        for j, (m, acc) in enumerate(carry):
            s = scores[j] if diag is None else jnp.where(col + diag * kblock <= row, scores[j], -jnp.inf)
            m_new = jnp.maximum(m, jnp.max(s, axis=1, keepdims=True))
            p = jnp.exp2(s - m_new).astype(BF16)
            vt_aug = jnp.concatenate(order(j, vt_ref[feats[j], pl.ds(off, kblock)], ones_rows), axis=0)
            new.append((m_new, jnp.exp2(m - m_new) * acc + _dot_nt(p, vt_aug)))
        return tuple(new)

    init = tuple((jnp.full((tile, 1), -jnp.inf, F32), jnp.zeros(q_ref.shape, F32)) for _ in range(heads))
    carry = lax.fori_loop(0, qi * sub, block, init)
    for r in range(sub):
        carry = block(qi * sub + r, carry, diag=r)
    out = None
    for own, (_, acc) in zip(owns, carry):
        o = acc / pltpu.roll(acc, head_dim, axis=1)
        out = o if out is None else jnp.where(own, o, out)
    o_ref[...] = out.astype(BF16)


def _fox_attention(qb, ktb, vtb, cb, *, head_dim, tile, kblock):
    t, a_width = qb.shape
    batch, _, seq_len = ktb.shape
    nq = seq_len // tile
    q_spec = pl.BlockSpec((tile, LANES), lambda b, hp, qi: (b * nq + qi, hp))
    kv_spec = pl.BlockSpec((None, LANES, seq_len), lambda b, hp, qi: (b, hp, 0))
    return pl.pallas_call(
        functools.partial(_fox_kernel, tile=tile, kblock=kblock, head_dim=head_dim),
        grid=(batch, a_width // LANES, nq),
        in_specs=[q_spec, kv_spec, kv_spec, kv_spec],
        out_specs=q_spec,
        out_shape=jax.ShapeDtypeStruct((t, a_width), BF16),
        compiler_params=_params("arbitrary", "arbitrary", "arbitrary"),
        name="fox_attention",
    )(qb, ktb, vtb, cb)


def _paged_attn_kernel(pt_ref, q_ref, knew_ref, vnew_ref, lfnew_ref, tri_ref, *refs,
                       n_pages, n_heads, head_dim):
    del pt_ref
    k_refs = refs[:n_pages]
    v_refs = refs[n_pages:2 * n_pages]
    lf_refs = refs[2 * n_pages:3 * n_pages]
    o_ref, qbd_ref, m_ref, l_ref, acc_ref, pref_ref = refs[3 * n_pages:]
    step = pl.program_id(1)
    nq, width = q_ref.shape
    rows = nq * n_heads
    page = k_refs[0].shape[1]
    head_of_lane = lax.broadcasted_iota(jnp.int32, (n_heads, width), 1) // head_dim
    own = head_of_lane == lax.broadcasted_iota(jnp.int32, (n_heads, width), 0)

    @pl.when(step == 0)
    def _():
        q = q_ref[...]
        for qq in range(nq):
            blk = jnp.where(own, jnp.broadcast_to(q[qq:qq + 1, :], (n_heads, width)), 0.0)
            qbd_ref[pl.ds(qq * n_heads, n_heads), :] = blk
        m_ref[...] = jnp.full(m_ref.shape, -jnp.inf, F32)
        l_ref[...] = jnp.zeros(l_ref.shape, F32)
        acc_ref[...] = jnp.zeros(acc_ref.shape, F32)
        pref_ref[...] = jnp.zeros(pref_ref.shape, F32)

    def attend(k_list, v_list, lf_list, causal):
        n = len(k_list)
        ys = [lf[...] for lf in lf_list]
        y = jnp.concatenate(ys, axis=0) if n > 1 else ys[0]
        pieces = jnp.concatenate(_split3(y), axis=0)
        cw = _dot(pieces, tri_ref[...])
        nh = n * n_heads
        cw = cw[:nh] + cw[nh:2 * nh] + cw[2 * nh:]
        qbd = qbd_ref[...].astype(BF16)
        pref = pref_ref[...]
        s_list = []
        for g in range(n):
            cg = cw[g * n_heads:(g + 1) * n_heads]
            c = cg + pref
            pref = pref + cg[:, page - 1:page]
            s = _dot(qbd, k_list[g][...].astype(BF16))
            s = s - jnp.concatenate([c] * nq, axis=0)
            if causal:
                key = lax.broadcasted_iota(jnp.int32, (rows, page), 1)
                qpos = lax.broadcasted_iota(jnp.int32, (rows, page), 0) // n_heads
                s = jnp.where(key <= qpos, s, -jnp.inf)
            s_list.append(s)
        pref_ref[...] = pref
        m, l, acc = m_ref[...], l_ref[...], acc_ref[...]
        group = max(n // PAGE_GROUPS, 1)
        for g0 in range(0, n, group):
            s_grp = jnp.concatenate(s_list[g0:g0 + group], axis=1) if group > 1 else s_list[g0]
            m_new = jnp.maximum(m, jnp.max(s_grp, axis=1, keepdims=True))
            alpha = jnp.exp(m - m_new)
            p = jnp.exp(s_grp - m_new)
            l = alpha * l + jnp.sum(p, axis=1, keepdims=True)
            pv = None
            for g in range(group):
                part = _dot_nt(p[:, g * page:(g + 1) * page].astype(BF16), v_list[g0 + g][...].astype(BF16))
                pv = part if pv is None else pv + part
            acc = alpha * acc + pv
            m = m_new
        m_ref[...], l_ref[...], acc_ref[...] = m, l, acc

    attend(k_refs, v_refs, lf_refs, causal=False)

    @pl.when(step == pl.num_programs(1) - 1)
    def _():
        attend([knew_ref], [vnew_ref], [lfnew_ref], causal=True)
        o = acc_ref[...] / l_ref[...]
        out_rows = []
        for qq in range(nq):
            blk = jnp.where(own, o[qq * n_heads:(qq + 1) * n_heads, :], 0.0)
            out_rows.append(jnp.sum(blk, axis=0, keepdims=True))
        o_ref[...] = jnp.concatenate(out_rows, axis=0).astype(o_ref.dtype)


def _paged_attention(q, kt_new, vt_new, lf_new, kt_pool, vt_pool, lf_pool, page_table, *, head_dim):
    batch, nq, width = q.shape
    page = kt_pool.shape[2]
    n_heads = lf_pool.shape[1]
    n_pages = PAGES_PER_STEP
    steps = page_table.shape[1] // n_pages
    rows = nq * n_heads
    tri = (jnp.arange(page)[:, None] <= jnp.arange(page)[None, :]).astype(BF16)

    per_batch = lambda shape: pl.BlockSpec((None,) + shape, lambda b, s, pt: (b, 0, 0))

    def paged(shape, g):
        return pl.BlockSpec((None,) + shape, lambda b, s, pt: (pt[b, s * n_pages + g], 0, 0))

    in_specs = [per_batch((nq, width)), per_batch((width, page)), per_batch((width, page)),
                per_batch((n_heads, page)), pl.BlockSpec(tri.shape, lambda b, s, pt: (0, 0))]
    in_specs += [paged((width, page), g) for g in range(n_pages)]
    in_specs += [paged((width, page), g) for g in range(n_pages)]
    in_specs += [paged((n_heads, page), g) for g in range(n_pages)]
    grid_spec = pltpu.PrefetchScalarGridSpec(
        num_scalar_prefetch=1,
        grid=(batch, steps),
        in_specs=in_specs,
        out_specs=per_batch((nq, width)),
        scratch_shapes=[pltpu.VMEM((rows, width), F32), pltpu.VMEM((rows, 1), F32),
                        pltpu.VMEM((rows, 1), F32), pltpu.VMEM((rows, width), F32),
                        pltpu.VMEM((n_heads, 1), F32)],
    )
    return pl.pallas_call(
        functools.partial(_paged_attn_kernel, n_pages=n_pages, n_heads=n_heads, head_dim=head_dim),
        grid_spec=grid_spec,
        out_shape=jax.ShapeDtypeStruct((batch, nq, width), BF16),
        compiler_params=_params("arbitrary", "arbitrary"),
        name="paged_attention",
    )(page_table, q, kt_new, vt_new, lf_new, tri,
      *([kt_pool] * n_pages), *([vt_pool] * n_pages), *([lf_pool] * n_pages))


def _c_proj_kernel(*refs, n_heads, key_dim, seq_len, tiles_per_seq, n_over):
    x_ref, g_ref, wqkv_ref, wz_ref, wba_ref, taps_ref, alog_ref, dtb_ref = refs[:8]
    over = list(refs[8:8 + n_over]) if n_over else None
    (q_ref, k_ref, v_ref, z_ref, beta_ref, gdec_ref, u_ref, ext_ref) = refs[8 + n_over:8 + n_over + 8]
    y_ref = None if over else refs[-1]
    qk_width = n_heads * key_dim
    conv_width = wqkv_ref.shape[1]

    h = _rms(x_ref[...], g_ref[...]).astype(BF16)
    tm = h.shape[0]
    first = pl.program_id(0) % tiles_per_seq == 0 if over is None else pl.program_id(0) == 0
    _conv_prologue(ext_ref, first, conv_width)

    cw = CONV_CHUNK if over is None else conv_width
    nxt = _dot(h, wqkv_ref[:, :cw])
    for c0 in range(0, conv_width, cw):
        cur = nxt
        if c0 + cw < conv_width:
            nxt = _dot(h, wqkv_ref[:, c0 + cw:c0 + 2 * cw])
        y = _silu(_causal_conv(cur, taps_ref, ext_ref, over, seq_len, y_ref, c0))
        if over is None:
            tail = cur[tm - SUBLANES:, :]
            u_ref[:, c0:c0 + cw] = tail
            _conv_carry(ext_ref, tail, c0)
        else:
            u_ref[...] = cur
        for b0 in range(0, cw, key_dim):
            col = c0 + b0
            blk = y[:, b0:b0 + key_dim]
            if col < 2 * qk_width:
                blk = blk * lax.rsqrt(jnp.sum(blk * blk, axis=-1, keepdims=True) + NORM_EPS)
            if col < qk_width:
                q_ref[:, col:col + key_dim] = (blk * key_dim ** -0.5).astype(BF16)
            elif col < 2 * qk_width:
                k_ref[:, col - qk_width:col - qk_width + key_dim] = blk.astype(BF16)
            else:
                v_ref[:, col - 2 * qk_width:col - 2 * qk_width + key_dim] = blk.astype(BF16)

    z_ref[...] = _dot(h, wz_ref[...]).astype(BF16)
    ba = _dot(h, wba_ref[...])
    beta_ref[...] = jax.nn.sigmoid(ba)[:, :n_heads]
    gdec = -jnp.exp(alog_ref[...]) * _softplus(ba + dtb_ref[...])
    gdec_ref[...] = gdec[:, n_heads:2 * n_heads]


def _c_proj(x, gain, wqkv, wz, wba, taps, alog, dtb, overrides, *, n_heads, key_dim, seq_len, tm):
    t, d = x.shape
    conv_width = wqkv.shape[1]
    qk_width = n_heads * key_dim
    v_width = conv_width - 2 * qk_width
    n_tiles = t // tm
    n_over = 0 if overrides is None else len(overrides)
    row = lambda c: pl.BlockSpec((tm, c), lambda i: (i, 0))
    in_specs = [row(d)] + [_const_spec(a) for a in (gain, wqkv, wz, wba, taps, alog, dtb)]
    in_specs += [row(conv_width)] * n_over
    if overrides is None:
        u_spec = pl.BlockSpec((None, SUBLANES, conv_width), lambda i: (i, 0, 0))
        u_shape = jax.ShapeDtypeStruct((n_tiles, SUBLANES, conv_width), F32)
    else:
        u_spec = row(conv_width)
        u_shape = jax.ShapeDtypeStruct((t, conv_width), F32)
    out_specs = [row(qk_width), row(qk_width), row(v_width), row(v_width), row(n_heads), row(n_heads), u_spec]
    out_shape = [jax.ShapeDtypeStruct((t, qk_width), BF16), jax.ShapeDtypeStruct((t, qk_width), BF16),
                 jax.ShapeDtypeStruct((t, v_width), BF16), jax.ShapeDtypeStruct((t, v_width), BF16),
                 jax.ShapeDtypeStruct((t, n_heads), F32), jax.ShapeDtypeStruct((t, n_heads), F32), u_shape]
    return pl.pallas_call(
        functools.partial(_c_proj_kernel, n_heads=n_heads, key_dim=key_dim, seq_len=seq_len,
                          tiles_per_seq=max(seq_len // tm, 1), n_over=n_over),
        grid=(n_tiles,),
        in_specs=in_specs,
        out_specs=out_specs,
        out_shape=out_shape,
        scratch_shapes=_conv_scratch(tm, conv_width, overrides is None),
        compiler_params=_params("arbitrary"),
        name="c_proj",
    )(x, gain, wqkv, wz, wba, taps, alog, dtb, *(overrides or []))


def _delta_kernel(q_ref, k_ref, v_ref, z_ref, gcol_ref, beta_ref, grow_ref, s0_ref, nw_ref,
                  o_ref, s_out_ref, m_ref, *, n_heads, chunk):
    step = pl.program_id(1)
    rows = q_ref.shape[0]
    dk = q_ref.shape[1] // n_heads
    dv = v_ref.shape[1] // n_heads
    n_chunks = rows // chunk
    ri = lax.broadcasted_iota(jnp.int32, (chunk, chunk), 0)
    ci = lax.broadcasted_iota(jnp.int32, (chunk, chunk), 1)
    incl = (ci <= ri)[None]
    strict = (ci < ri)[None]

    @pl.when(step == 0)
    def _():
        for h in range(n_heads):
            m_ref[h] = s0_ref[h].T

    probs = [(c, h) for c in range(n_chunks) for h in range(n_heads)]
    rs = lambda c: slice(c * chunk, (c + 1) * chunk)
    stack = lambda f: jnp.stack([f(c, h) for c, h in probs])
    kb = stack(lambda c, h: k_ref[rs(c), h * dk:(h + 1) * dk])
    qb = stack(lambda c, h: q_ref[rs(c), h * dk:(h + 1) * dk])
    vf = stack(lambda c, h: v_ref[rs(c), h * dv:(h + 1) * dv]).astype(F32)
    gc = stack(lambda c, h: gcol_ref[rs(c), h:h + 1])
    gr = stack(lambda c, h: grow_ref[h:h + 1, rs(c)])
    bc = stack(lambda c, h: beta_ref[rs(c), h:h + 1])
    kf = kb.astype(F32)
    eg = jnp.exp(gc)
    decay = jnp.where(incl, jnp.exp(jnp.where(incl, gc - gr, 0.0)), 0.0)
    kqk = _bdot_nt(jnp.concatenate([kb, qb], axis=1), kb)
    a = jnp.where(strict, bc * kqk[:, :chunk] * decay, 0.0)
    nil = -a
    pw = a
    span = 2
    while span < chunk:
        pwb = pw.astype(BF16)
        pw = _bdot(pwb, pwb)
        nil = nil + pw + _bdot(nil.astype(BF16), pw.astype(BF16))
        span *= 2
    rhs = jnp.concatenate([bc * vf, (bc * eg) * kf], axis=2)
    w = rhs + _bdot(nil.astype(BF16), rhs.astype(BF16))
    w_v = w[:, :, :dv]
    p = (kqk[:, chunk:] * decay).astype(BF16)
    g_last = gc[:, chunk - 1:chunk, :]
    lhs = jnp.concatenate([w[:, :, dv:], eg * qb.astype(F32)], axis=1).astype(BF16)
    k_dec = (jnp.exp(g_last - gc) * kf).astype(BF16)
    g_chunk = jnp.exp(g_last)

    for c in range(n_chunks):
        hs = slice(c * n_heads, (c + 1) * n_heads)
        m = m_ref[...]
        r = _bdot(lhs[hs], m.astype(BF16))
        ub = (w_v[hs] - r[:, :chunk]).astype(BF16)
        o = r[:, chunk:] + _bdot(p[hs], ub)
        m_ref[...] = g_chunk[hs] * m + _bdot_tn(k_dec[hs], ub)
        on = o * lax.rsqrt(jnp.mean(o * o, axis=-1, keepdims=True) + NORM_EPS) * nw_ref[...]
        for h in range(n_heads):
            vl = slice(h * dv, (h + 1) * dv)
            o_ref[rs(c), vl] = (on[h] * _silu(z_ref[rs(c), vl].astype(F32))).astype(BF16)

    @pl.when(step == pl.num_programs(1) - 1)
    def _():
        for h in range(n_heads):
            s_out_ref[h] = m_ref[h].T


def _delta_rule(q, k, v, z, gcol, beta, grow, s0, norm_w, *, batch, seq_len, n_heads, chunk, rows):
    t = q.shape[0]
    steps = seq_len // rows
    dk = q.shape[1] // n_heads
    dv = v.shape[1] // n_heads
    row = lambda c: pl.BlockSpec((rows, c), lambda b, s: (b * steps + s, 0))
    state = pl.BlockSpec((None, n_heads, dv, dk), lambda b, s: (b, 0, 0, 0))
    return pl.pallas_call(
        functools.partial(_delta_kernel, n_heads=n_heads, chunk=chunk),
        grid=(batch, steps),
        in_specs=[row(q.shape[1]), row(k.shape[1]), row(v.shape[1]), row(z.shape[1]),
                  row(n_heads), row(n_heads),
                  pl.BlockSpec((None, n_heads, rows), lambda b, s: (b, 0, s)),
                  state, pl.BlockSpec(norm_w.shape, lambda b, s: (0, 0))],
        out_specs=[row(v.shape[1]), state],
        out_shape=[jax.ShapeDtypeStruct((t, v.shape[1]), BF16),
                   jax.ShapeDtypeStruct((batch, n_heads, dv, dk), F32)],
        scratch_shapes=[pltpu.VMEM((n_heads, dk, dv), F32)],
        compiler_params=_params("arbitrary", "arbitrary"),
        name="gated_delta_rule",
    )(q, k, v, z, gcol, beta, grow, s0, norm_w)


def _pad_lanes(a):
    return jnp.pad(a, ((0, 0), (0, LANES - a.shape[1])))


def _rows_to_heads(a, batch, seq_len):
    return a.reshape(batch, seq_len, a.shape[1]).transpose(0, 2, 1)


def _heads_to_rows(a):
    b, h, length = a.shape
    return a.transpose(0, 2, 1).reshape(b * length, h)


def _feature_major_pool(pool):
    n_l, n_p, page = pool.shape[:3]
    perm = (0, 1) + tuple(range(3, pool.ndim)) + (2,)
    return pool.transpose(perm).reshape(n_l * n_p, -1, page)


def _mixer_ab(x, past, w, i, gain, *, batch, seq_len, tm):
    t = x.shape[0]
    hd = w["a_head_dim"]
    nh = w["ab_b_f"].shape[1]
    aw = nh * hd
    w_in = w["ab_w_in"][i]
    bw = (w_in.shape[1] - 3 * aw - nh) // 3
    wb = w_in[:, 3 * aw + nh:].astype(BF16)
    wf = w_in[:, 3 * aw:3 * aw + nh]
    taps = w["ab_conv_w"][i]
    n_keep = taps.shape[0] - 1
    if past is None:
        wq = w_in[:, :aw].astype(BF16)
        wkvt = w_in[:, aw:3 * aw].T.astype(BF16)
        kt, vt, ktb, vtb, lft, qb, ob, tails = _ab_proj_prompt(
            x, gain, wq, wkvt, wf.T.astype(BF16), w["ab_b_f"][i][:, None], wb, taps,
            batch=batch, seq_len=seq_len, head_dim=hd, tm=tm)
        o_a = _fox_attention(qb, ktb, vtb, _forget_bias_rows(lft, hd), head_dim=hd,
                             tile=min(1024, seq_len), kblock=min(512, seq_len))
        tails = tails.reshape(batch, seq_len // tm, SUBLANES, bw)[:, -1]
        new_buf = tails[:, SUBLANES - n_keep:, :]
        to_tokens = lambda a: a.reshape(batch, nh, hd, seq_len).transpose(0, 3, 1, 2)
        state = (to_tokens(kt), to_tokens(vt), lft.transpose(0, 2, 1), new_buf)
    else:
        overrides = _conv_overrides(past["state_b_conv"][i], seq_len)
        q_s, k32, v32, logf, ob, gcu = _ab_proj_sample(
            x, gain, w_in[:, :3 * aw].astype(BF16), _pad_lanes(wf).astype(BF16),
            _pad_lanes(w["ab_b_f"][i][None, :]), wb, taps, overrides,
            n_heads=nh, head_dim=hd, seq_len=seq_len)
        n_phys, page = past["cache_a_k"].shape[1:3]
        pad_keys = lambda a: jnp.pad(a.reshape(batch, seq_len, -1).transpose(0, 2, 1),
                                     ((0, 0), (0, 0), (0, page - seq_len)))
        o_a = _paged_attention(
            q_s.reshape(batch, seq_len, aw), pad_keys(k32), pad_keys(v32), pad_keys(logf),
            _feature_major_pool(past["cache_a_k"]), _feature_major_pool(past["cache_a_v"]),
            _feature_major_pool(past["cache_a_logf"]), past["page_table"] + i * n_phys, head_dim=hd)
        o_a = o_a.reshape(t, aw)
        new_buf = gcu.reshape(batch, seq_len, bw)[:, seq_len - n_keep:, :]
        state = (k32.reshape(batch, seq_len, nh, hd), v32.reshape(batch, seq_len, nh, hd),
                 logf.reshape(batch, seq_len, nh), new_buf)
    return [o_a, ob], w["ab_w_out"][i].astype(BF16), state


def _mixer_c(x, past, w, i, gain, *, batch, seq_len, tm):
    t = x.shape[0]
    nh = w["c_a_log"].shape[1]
    dv = w["c_norm_w"].shape[1]
    w_in = w["c_w_in"][i]
    v_width = nh * dv
    conv_width = w_in.shape[1] - v_width - 2 * nh
    dk = (conv_width - v_width) // 2 // nh
    wqkv = w_in[:, :conv_width].astype(BF16)
    wz = w_in[:, conv_width:conv_width + v_width].astype(BF16)
    wba = _pad_lanes(w_in[:, conv_width + v_width:]).astype(BF16)
    zeros_h = jnp.zeros((1, nh), F32)
    alog = _pad_lanes(jnp.concatenate([zeros_h, w["c_a_log"][i][None, :]], axis=1))
    dtb = _pad_lanes(jnp.concatenate([zeros_h, w["c_dt_bias"][i][None, :]], axis=1))
    taps = w["c_conv_w"][i]
    n_keep = taps.shape[0] - 1
    tm_c = min(tm, 256)
    if past is None:
        overrides = None
        s0 = jnp.zeros((batch, nh, dv, dk), F32)
    else:
        overrides = _conv_overrides(past["state_c_conv"][i], seq_len)
        s0 = past["state_c_S"][i]
    qn, kn, vn, zb, beta, gdec, u_out = _c_proj(
        x, gain, wqkv, wz, wba, taps, alog, dtb, overrides,
        n_heads=nh, key_dim=dk, seq_len=seq_len, tm=tm_c)
    if past is None:
        tails = u_out.reshape(batch, seq_len // tm_c, SUBLANES, conv_width)[:, -1]
        new_buf = tails[:, SUBLANES - n_keep:, :]
        len_pad = seq_len
    else:
        new_buf = u_out.reshape(batch, seq_len, conv_width)[:, seq_len - n_keep:, :]
        len_pad = -(-seq_len // DELTA_CHUNK) * DELTA_CHUNK
        pad3 = lambda a: jnp.pad(a.reshape(batch, seq_len, a.shape[1]),
                                 ((0, 0), (0, len_pad - seq_len), (0, 0))).reshape(batch * len_pad, a.shape[1])
        qn, kn, vn, zb, beta, gdec = [pad3(a) for a in (qn, kn, vn, zb, beta, gdec)]
    scan_len = -(-len_pad // LANES) * LANES
    g_heads = jnp.pad(_rows_to_heads(gdec, batch, len_pad), ((0, 0), (0, 0), (0, scan_len - len_pad)))
    grow = _segment_cumsum(g_heads, DELTA_CHUNK)[:, :, :len_pad]
    og, s_new = _delta_rule(qn, kn, vn, zb, _heads_to_rows(grow), beta, grow, s0, w["c_norm_w"][i][None, :],
                            batch=batch, seq_len=len_pad, n_heads=nh, chunk=DELTA_CHUNK,
                            rows=min(DELTA_ROWS, len_pad))
    if len_pad != seq_len:
        og = og.reshape(batch, len_pad, v_width)[:, :seq_len].reshape(t, v_width)
    return [og], w["c_w_out"][i].astype(BF16), (new_buf, s_new)


def _run_trunk(x3, past, w, *, tm):
    batch, seq_len, d = x3.shape
    x = x3.reshape(batch * seq_len, d)
    ab_states, c_states = [], []
    ffn = lambda x, g0, g1, layer, j, mixer=None: _ffn_block(
        x, g0, g1, w["ffn_w_gate"], w["ffn_w_up"], w["ffn_w_down"], layer, j, tm=tm, mixer=mixer)
    for layer in range(w["norm_g"].shape[0]):
        gains = [w["norm_g"][layer, j][None, :] for j in range(6)]
        x = ffn(x, gains[0], gains[1], layer, 0)
        mixer, states = (_mixer_ab, ab_states) if layer % 2 == 0 else (_mixer_c, c_states)
        parts, w_out, state = mixer(x, past, w, layer // 2, gains[2], batch=batch, seq_len=seq_len, tm=tm)
        states.append(state)
        x = ffn(x, gains[4], gains[5], layer, 1, mixer=(parts, w_out, gains[3]))
    ab_new = [jnp.stack(ts) for ts in zip(*ab_states)]
    c_new = [jnp.stack(ts) for ts in zip(*c_states)]
    return x.reshape(batch, seq_len, d), ab_new, c_new


def kernel(x_prompt, x_sample, cache_a_k, cache_a_v, cache_a_logf, page_table, state_b_conv, state_c_conv, state_c_S, norm_g, ffn_w_gate, ffn_w_up, ffn_w_down, ab_w_in, ab_b_f, ab_conv_w, ab_w_out, c_w_in, c_conv_w, c_a_log, c_dt_bias, c_norm_w, c_w_out):
    w = dict(norm_g=norm_g,
             ffn_w_gate=ffn_w_gate.astype(BF16), ffn_w_up=ffn_w_up.astype(BF16), ffn_w_down=ffn_w_down.astype(BF16),
             ab_w_in=ab_w_in, ab_b_f=ab_b_f, ab_conv_w=ab_conv_w, ab_w_out=ab_w_out,
             c_w_in=c_w_in, c_conv_w=c_conv_w, c_a_log=c_a_log, c_dt_bias=c_dt_bias,
             c_norm_w=c_norm_w, c_w_out=c_w_out, a_head_dim=cache_a_k.shape[-1])
    past = dict(cache_a_k=cache_a_k, cache_a_v=cache_a_v, cache_a_logf=cache_a_logf, page_table=page_table,
                state_b_conv=state_b_conv, state_c_conv=state_c_conv, state_c_S=state_c_S)
    prompt_rows = x_prompt.shape[0] * x_prompt.shape[1]
    sample_rows = x_sample.shape[0] * x_sample.shape[1]
    y_prompt, ab_p, c_p = _run_trunk(x_prompt, None, w, tm=min(512, prompt_rows))
    y_sample, ab_s, c_s = _run_trunk(x_sample, past, w, tm=min(512, sample_rows))
    a_k_prompt, a_v_prompt, a_logf_prompt, b_conv_prompt = ab_p
    a_k_sample, a_v_sample, a_logf_sample, b_conv_sample = ab_s
    c_conv_prompt, c_S_prompt = c_p
    c_conv_sample, c_S_sample = c_s
    return (y_prompt, y_sample,
            a_k_prompt, a_v_prompt, a_logf_prompt,
            a_k_sample, a_v_sample, a_logf_sample,
            b_conv_prompt, b_conv_sample,
            c_conv_prompt, c_conv_sample,
            c_S_prompt, c_S_sample)
```

```python
import functools

import jax
import jax.numpy as jnp
from jax import lax
from jax.experimental import pallas as pl
from jax.experimental.pallas import tpu as pltpu

F32 = jnp.float32
BF16 = jnp.bfloat16
NORM_EPS = 1e-6
LOG2_E = 1.4426950408889634

LANES = 128
SUBLANES = 8
VMEM_LIMIT_BYTES = 56 * 1024 * 1024
DELTA_CHUNK = 64
DELTA_ROWS = 512
CONV_ROW_STRIDE = 4
CONV_CHUNK = 512
PAGES_PER_STEP = 16
PAGE_GROUPS = 2
PAGE_SLOTS = 3


def _params(*semantics):
    return pltpu.CompilerParams(dimension_semantics=semantics, vmem_limit_bytes=VMEM_LIMIT_BYTES)


def _resident(shape, index_map):
    return pl.BlockSpec(shape, index_map, pipeline_mode=pl.Buffered(1))


def _const_spec(a):
    return _resident(a.shape, lambda *_: (0,) * a.ndim)


def _rms(x, g):
    return x * lax.rsqrt(jnp.mean(x * x, axis=-1, keepdims=True) + NORM_EPS) * g


def _softplus(x):
    return jnp.maximum(x, 0.0) + jnp.log1p(jnp.exp(-jnp.abs(x)))


def _silu(x):
    return x * jax.nn.sigmoid(x)


def _dot(a, b):
    return jnp.dot(a, b, preferred_element_type=F32)


def _dot_nt(a, b):
    return lax.dot_general(a, b, (((1,), (1,)), ((), ())), preferred_element_type=F32)


def _dot_tn(a, b):
    return lax.dot_general(a, b, (((0,), (0,)), ((), ())), preferred_element_type=F32)


def _bdot(a, b):
    return lax.dot_general(a, b, (((2,), (1,)), ((0,), (0,))), preferred_element_type=F32)


def _bdot_nt(a, b):
    return lax.dot_general(a, b, (((2,), (2,)), ((0,), (0,))), preferred_element_type=F32)


def _bdot_tn(a, b):
    return lax.dot_general(a, b, (((1,), (1,)), ((0,), (0,))), preferred_element_type=F32)


def _mix_rows(part_refs, w_ref):
    mix = None
    k0 = 0
    for a_ref in part_refs:
        kw = a_ref.shape[1]
        part = _dot(a_ref[...], w_ref[k0:k0 + kw, :])
        mix = part if mix is None else mix + part
        k0 += kw
    return mix


def _ffn_kernel(*refs, n_parts, ff_chunk):
    part_refs = refs[:n_parts]
    if n_parts:
        w_mix_ref, g_mix_ref = refs[n_parts:n_parts + 2]
        refs = refs[n_parts + 2:]
    x_ref, gpre_ref, gpost_ref, wg_ref, wu_ref, wd_ref, o_ref = refs
    x = x_ref[...]
    if n_parts:
        x = x + _rms(_mix_rows(part_refs, w_mix_ref), g_mix_ref[...])
    h = _rms(x, gpre_ref[...]).astype(BF16)
    d_ff = wg_ref.shape[1]
    proj = lambda c0: (_dot(h, wg_ref[:, c0:c0 + ff_chunk]), _dot(h, wu_ref[:, c0:c0 + ff_chunk]))
    y = None
    nxt = proj(0)
    for c0 in range(0, d_ff, ff_chunk):
        gate, up = nxt
        if c0 + ff_chunk < d_ff:
            nxt = proj(c0 + ff_chunk)
        act = (_silu(gate) * up).astype(BF16)
        part = _dot(act, wd_ref[c0:c0 + ff_chunk, :])
        y = part if y is None else y + part
    o_ref[...] = x + 0.5 * _rms(y, gpost_ref[...])


def _ffn_chunk(d_ff):
    for c in range(4 * LANES, 0, -LANES):
        if d_ff % c == 0:
            return c
    return d_ff


def _ffn_block(x, g_pre, g_post, wg_all, wu_all, wd_all, layer, j, *, tm, mixer=None):
    t, d = x.shape
    d_ff = wg_all.shape[-1]
    row = lambda c: pl.BlockSpec((tm, c), lambda i: (i, 0))
    gain = _resident((1, d), lambda i: (0, 0))
    w_in = _resident((None, None, d, d_ff), lambda i: (layer, j, 0, 0))
    w_out = _resident((None, None, d_ff, d), lambda i: (layer, j, 0, 0))
    parts, w_mix, g_mix = mixer if mixer is not None else ([], None, None)
    mix_args = list(parts) + ([w_mix, g_mix] if parts else [])
    mix_specs = [row(p.shape[1]) for p in parts] + ([_const_spec(w_mix), gain] if parts else [])
    return pl.pallas_call(
        functools.partial(_ffn_kernel, n_parts=len(parts), ff_chunk=_ffn_chunk(d_ff)),
        grid=(t // tm,),
        in_specs=mix_specs + [row(d), gain, gain, w_in, w_in, w_out],
        out_specs=row(d),
        out_shape=jax.ShapeDtypeStruct((t, d), F32),
        compiler_params=_params("arbitrary"),
        name="ffn_block",
    )(*mix_args, x, g_pre, g_post, wg_all, wu_all, wd_all)


def _causal_conv(u, taps_ref, ext_ref, overrides, seq_len, y_ref=None, col0=0):
    tm = u.shape[0]
    n_taps = taps_ref.shape[0]
    if overrides is None:
        stride = CONV_ROW_STRIDE
        group = SUBLANES * stride
        assert n_taps <= stride + 1 <= SUBLANES + 1
        blocks = range(col0 // LANES, (col0 + u.shape[1]) // LANES)
        for lb in blocks:
            lanes = slice(lb * LANES, (lb + 1) * LANES)
            ext_ref[lb, pl.ds(SUBLANES, tm), :] = u[:, lb * LANES - col0:(lb + 1) * LANES - col0]
            taps = [taps_ref[n_taps - 1 - j:n_taps - j, lanes] for j in range(n_taps)]
            for g0 in range(0, tm, group):
                win = {d: ext_ref[lb, pl.ds(SUBLANES + g0 + d, SUBLANES, stride=stride), :]
                       for d in range(1 - n_taps, stride)}
                for i in range(stride):
                    acc = win[i] * taps[0]
                    for j in range(1, n_taps):
                        acc = acc + win[i - j] * taps[j]
                    y_ref[lb, pl.ds(g0 + i, SUBLANES, stride=stride), :] = acc
        return jnp.concatenate([y_ref[lb] for lb in blocks], axis=1)
    ext_ref[pl.ds(SUBLANES, tm), :] = u
    out = u * taps_ref[n_taps - 1:n_taps, :]
    pos = lax.broadcasted_iota(jnp.int32, u.shape, 0) % seq_len
    for j in range(1, n_taps):
        prev = jnp.where(pos >= j, ext_ref[pl.ds(SUBLANES - j, tm), :], overrides[j - 1][...])
        out = out + prev * taps_ref[n_taps - 1 - j:n_taps - j, :]
    return out


def _conv_overrides(state, seq_len):
    b, wm1, c = state.shape
    assert seq_len >= wm1
    outs = []
    for j in range(1, wm1 + 1):
        rows = jnp.pad(state[:, wm1 - j:, :], ((0, 0), (0, seq_len - j), (0, 0)))
        outs.append(rows.reshape(b * seq_len, c))
    return outs


def _conv_carry(ext_ref, tail, col0=0):
    if len(ext_ref.shape) == 2:
        ext_ref[pl.ds(0, SUBLANES), :] = tail
    else:
        for lb in range(tail.shape[1] // LANES):
            ext_ref[col0 // LANES + lb, pl.ds(0, SUBLANES), :] = tail[:, lb * LANES:(lb + 1) * LANES]


def _conv_prologue(ext_ref, first, width):
    @pl.when(first)
    def _():
        _conv_carry(ext_ref, jnp.zeros((SUBLANES, width), F32))


def _conv_scratch(tm, width, blocked):
    if not blocked:
        return [pltpu.VMEM((SUBLANES + tm, width), F32)]
    assert tm % (SUBLANES * CONV_ROW_STRIDE) == 0 and width % LANES == 0
    return [pltpu.VMEM((width // LANES, SUBLANES + tm, LANES), F32), pltpu.VMEM((width // LANES, tm, LANES), F32)]


def _log_sigmoid(z):
    return -_softplus(-z)


def _ab_proj_prompt_kernel(x_ref, g_ref, wq_ref, wkvt_ref, wft_ref, bf_ref, wb_ref, taps_ref,
                           kt_ref, vt_ref, ktb_ref, vtb_ref, lft_ref, qb_ref, ob_ref, tail_ref, ext_ref, y_ref,
                           *, head_dim, tiles_per_seq):
    a_width = wq_ref.shape[1]
    b_width = wb_ref.shape[1] // 3
    h = _rms(x_ref[...], g_ref[...]).astype(BF16)
    qb_ref[...] = (_dot(h, wq_ref[...]) * (head_dim ** -0.5 * LOG2_E)).astype(BF16)
    kvt = _dot_nt(wkvt_ref[...], h)
    kt = kvt[:a_width]
    vt = kvt[a_width:]
    kt_ref[...] = kt
    vt_ref[...] = vt
    ktb_ref[...] = kt.astype(BF16)
    vtb_ref[...] = vt.astype(BF16)
    lft_ref[...] = _log_sigmoid(_dot_nt(wft_ref[...], h) + bf_ref[...])

    gbu = _dot(h, wb_ref[...])
    gate_b = gbu[:, :b_width]
    gcu = gbu[:, b_width:2 * b_width] * gbu[:, 2 * b_width:]
    _conv_prologue(ext_ref, pl.program_id(0) % tiles_per_seq == 0, b_width)
    y = _causal_conv(gcu, taps_ref, ext_ref, None, None, y_ref)
    ob_ref[...] = (gate_b * y).astype(BF16)
    tail = gcu[gcu.shape[0] - SUBLANES:, :]
    tail_ref[...] = tail
    _conv_carry(ext_ref, tail)


def _ab_proj_prompt(x, gain, wq, wkvt, wft, bf_col, wb, taps, *, batch, seq_len, head_dim, tm):
    t, d = x.shape
    a_width = wq.shape[1]
    n_heads = wft.shape[0]
    b_width = wb.shape[1] // 3
    tiles_per_seq = seq_len // tm
    row = lambda c: pl.BlockSpec((tm, c), lambda i: (i, 0))
    feat = lambda c: pl.BlockSpec((None, c, tm), lambda i: (i // tiles_per_seq, 0, i % tiles_per_seq))
    feat_shape = lambda c, dt: jax.ShapeDtypeStruct((batch, c, seq_len), dt)
    return pl.pallas_call(
        functools.partial(_ab_proj_prompt_kernel, head_dim=head_dim, tiles_per_seq=tiles_per_seq),
        grid=(t // tm,),
        in_specs=[row(d)] + [_const_spec(a) for a in (gain, wq, wkvt, wft, bf_col, wb, taps)],
        out_specs=[feat(a_width), feat(a_width), feat(a_width), feat(a_width), feat(n_heads),
                   row(a_width), row(b_width),
                   pl.BlockSpec((None, SUBLANES, b_width), lambda i: (i, 0, 0))],
        out_shape=[feat_shape(a_width, F32), feat_shape(a_width, F32), feat_shape(a_width, BF16),
                   feat_shape(a_width, BF16), feat_shape(n_heads, F32),
                   jax.ShapeDtypeStruct((t, a_width), BF16), jax.ShapeDtypeStruct((t, b_width), BF16),
                   jax.ShapeDtypeStruct((t // tm, SUBLANES, b_width), F32)],
        scratch_shapes=_conv_scratch(tm, b_width, True),
        compiler_params=_params("arbitrary"),
        name="ab_proj_prompt",
    )(x, gain, wq, wkvt, wft, bf_col, wb, taps)


def _ab_proj_sample_kernel(*refs, n_heads, head_dim, seq_len, n_over):
    x_ref, g_ref, wqkv_ref, wf_ref, bf_ref, wb_ref, taps_ref = refs[:7]
    over = list(refs[7:7 + n_over])
    q_ref, k_ref, v_ref, logf_ref, ob_ref, u_ref, ext_ref = refs[7 + n_over:]
    a_width = n_heads * head_dim
    b_width = wb_ref.shape[1] // 3
    h = _rms(x_ref[...], g_ref[...]).astype(BF16)
    qkv = _dot(h, wqkv_ref[...])
    q_ref[...] = qkv[:, :a_width] * head_dim ** -0.5
    k_ref[...] = qkv[:, a_width:2 * a_width]
    v_ref[...] = qkv[:, 2 * a_width:]
    logf_ref[...] = _log_sigmoid(_dot(h, wf_ref[...]) + bf_ref[...])[:, :n_heads]

    gbu = _dot(h, wb_ref[...])
    gate_b = gbu[:, :b_width]
    gcu = gbu[:, b_width:2 * b_width] * gbu[:, 2 * b_width:]
    _conv_prologue(ext_ref, pl.program_id(0) == 0, b_width)
    y = _causal_conv(gcu, taps_ref, ext_ref, over, seq_len)
    ob_ref[...] = (gate_b * y).astype(BF16)
    u_ref[...] = gcu


def _ab_proj_sample(x, gain, wqkv, wf, bf, wb, taps, overrides, *, n_heads, head_dim, seq_len):
    t, d = x.shape
    a_width = n_heads * head_dim
    b_width = wb.shape[1] // 3
    full = lambda c: pl.BlockSpec((t, c), lambda i: (0, 0))
    f32 = lambda c: jax.ShapeDtypeStruct((t, c), F32)
    return pl.pallas_call(
        functools.partial(_ab_proj_sample_kernel, n_heads=n_heads, head_dim=head_dim, seq_len=seq_len,
                          n_over=len(overrides)),
        grid=(1,),
        in_specs=[full(d)] + [_const_spec(a) for a in (gain, wqkv, wf, bf, wb, taps)]
        + [full(b_width)] * len(overrides),
        out_specs=[full(a_width), full(a_width), full(a_width), full(n_heads), full(b_width), full(b_width)],
        out_shape=[f32(a_width), f32(a_width), f32(a_width), f32(n_heads),
                   jax.ShapeDtypeStruct((t, b_width), BF16), f32(b_width)],
        scratch_shapes=[pltpu.VMEM((SUBLANES + t, b_width), F32)],
        compiler_params=_params("arbitrary"),
        name="ab_proj_sample",
    )(x, gain, wqkv, wf, bf, wb, taps, *overrides)


def _scan_kernel(x_ref, o_ref, *, seg):
    x = x_ref[...]
    pos = lax.broadcasted_iota(jnp.int32, x.shape, 1) % seg
    d = 1
    while d < seg:
        x = x + jnp.where(pos >= d, pltpu.roll(x, d, axis=1), 0.0)
        d *= 2
    o_ref[...] = x


def _segment_cumsum(x, seg):
    b, h, length = x.shape
    spec = pl.BlockSpec((None, h, length), lambda i: (i, 0, 0))
    return pl.pallas_call(
        functools.partial(_scan_kernel, seg=seg),
        grid=(b,),
        in_specs=[spec],
        out_specs=spec,
        out_shape=jax.ShapeDtypeStruct(x.shape, F32),
        compiler_params=_params("arbitrary"),
        name="segment_cumsum",
    )(x)


def _split3(x):
    hi = x.astype(BF16)
    r = x - hi.astype(F32)
    mid = r.astype(BF16)
    lo = (r - mid.astype(F32)).astype(BF16)
    return hi, mid, lo


def _forget_bias_kernel(lf_ref, o_ref, *, head_dim):
    x = lf_ref[...]
    n_heads, length = x.shape
    pos = lax.broadcasted_iota(jnp.int32, x.shape, 1)
    d = 1
    while d < length:
        x = x + jnp.where(pos >= d, pltpu.roll(x, d, axis=1), 0.0)
        d *= 2
    pieces = [p.astype(F32) for p in _split3(-LOG2_E * x)]
    pad = jnp.zeros((head_dim - len(pieces), length), F32)
    for h in range(n_heads):
        blk = jnp.concatenate([p[h:h + 1] for p in pieces] + [pad], axis=0)
        o_ref[pl.ds(h * head_dim, head_dim), :] = blk.astype(BF16)


def _forget_bias_rows(lft, head_dim):
    batch, n_heads, length = lft.shape
    return pl.pallas_call(
        functools.partial(_forget_bias_kernel, head_dim=head_dim),
        grid=(batch,),
        in_specs=[pl.BlockSpec((None, n_heads, length), lambda b: (b, 0, 0))],
        out_specs=pl.BlockSpec((None, n_heads * head_dim, length), lambda b: (b, 0, 0)),
        out_shape=jax.ShapeDtypeStruct((batch, n_heads * head_dim, length), BF16),
        compiler_params=_params("arbitrary"),
        name="forget_bias_rows",
    )(lft)


def _fox_kernel(q_ref, kt_ref, vt_ref, cb_ref, o_ref, *, tile, kblock, head_dim):
    qi = pl.program_id(2)
    heads = q_ref.shape[1] // head_dim
    sub = tile // kblock
    row = lax.broadcasted_iota(jnp.int32, (kblock, kblock), 0)
    col = lax.broadcasted_iota(jnp.int32, (kblock, kblock), 1)
    head_of_lane = lax.broadcasted_iota(jnp.int32, q_ref.shape, 1) // head_dim
    q_pair = q_ref[...]
    ones_rows = jnp.ones((head_dim, kblock), BF16)
    owns = [head_of_lane == j for j in range(heads)]
    q_augs = [jnp.where(own, q_pair, jnp.ones_like(q_pair)) for own in owns]

    def block(ki, carry, q_rows=slice(None), triangular=False):
        off = pl.multiple_of(ki * kblock, kblock)
        feats = [slice(j * head_dim, (j + 1) * head_dim) for j in range(heads)]
        order = lambda j, data, aux: [data, aux] if j == 0 else [aux, data]
        scores = []
        for j in range(heads):
            kt_aug = jnp.concatenate(
                order(j, kt_ref[feats[j], pl.ds(off, kblock)], cb_ref[feats[j], pl.ds(off, kblock)]), axis=0)
            scores.append(_dot(q_augs[j][q_rows], kt_aug))
        new = []
        for j, (m, acc) in enumerate(carry):
            s = jnp.where(col <= row, scores[j], -jnp.inf) if triangular else scores[j]
            m_new = jnp.maximum(m, jnp.max(s, axis=1, keepdims=True))
            p = jnp.exp2(s - m_new).astype(BF16)
            vt_aug = jnp.concatenate(order(j, vt_ref[feats[j], pl.ds(off, kblock)], ones_rows), axis=0)
            new.append((m_new, jnp.exp2(m - m_new) * acc + _dot_nt(p, vt_aug)))
        return tuple(new)

    init = tuple((jnp.full((tile, 1), -jnp.inf, F32), jnp.zeros(q_ref.shape, F32)) for _ in range(heads))
    carry = lax.fori_loop(0, qi * sub, block, init)
    pieces = []
    for i in range(sub):
        q_rows = slice(i * kblock, (i + 1) * kblock)
        piece = tuple((m[q_rows], acc[q_rows]) for m, acc in carry)
        for r in range(i + 1):
            piece = block(qi * sub + r, piece, q_rows, triangular=r == i)
        pieces.append(piece)
    carry = tuple((None, jnp.concatenate([piece[j][1] for piece in pieces], axis=0)) for j in range(heads))
    out = None
    for own, (_, acc) in zip(owns, carry):
        o = acc / pltpu.roll(acc, head_dim, axis=1)
        out = o if out is None else jnp.where(own, o, out)
    o_ref[...] = out.astype(BF16)


def _fox_attention(qb, ktb, vtb, cb, *, head_dim, tile, kblock):
    t, a_width = qb.shape
    batch, _, seq_len = ktb.shape
    nq = seq_len // tile
    q_spec = pl.BlockSpec((tile, LANES), lambda b, hp, qi: (b * nq + qi, hp))
    kv_spec = pl.BlockSpec((None, LANES, seq_len), lambda b, hp, qi: (b, hp, 0))
    return pl.pallas_call(
        functools.partial(_fox_kernel, tile=tile, kblock=kblock, head_dim=head_dim),
        grid=(batch, a_width // LANES, nq),
        in_specs=[q_spec, kv_spec, kv_spec, kv_spec],
        out_specs=q_spec,
        out_shape=jax.ShapeDtypeStruct((t, a_width), BF16),
        compiler_params=_params("arbitrary", "arbitrary", "arbitrary"),
        name="fox_attention",
    )(qb, ktb, vtb, cb)


def _paged_attn_kernel(pt_ref, q_ref, knew_ref, vnew_ref, lfnew_ref, tri_ref, kpool_ref, vpool_ref, lfpool_ref,
                       o_ref, qbd_ref, m_ref, l_ref, acc_ref, pref_ref, kbuf, vbuf, lfbuf, sems,
                       *, n_pages, n_heads, head_dim):
    step = pl.program_id(1)
    steps = pl.num_programs(1)
    lin = pl.program_id(0) * steps + step
    total = pl.num_programs(0) * steps
    n_slots = kbuf.shape[0]
    nq, width = q_ref.shape
    rows = nq * n_heads
    page = kbuf.shape[3]

    def gather(lin_step, slot, pages=None):
        out = []
        for j in range(n_pages):
            pg = 0 if pages is None else pages[j]
            out += [pltpu.make_async_copy(kpool_ref.at[pg], kbuf.at[slot, j], sems.at[0, slot]),
                    pltpu.make_async_copy(vpool_ref.at[pg], vbuf.at[slot, j], sems.at[1, slot]),
                    pltpu.make_async_copy(lfpool_ref.at[pg], lfbuf.at[slot, j], sems.at[2, slot])]
        return out

    def start_gather(lin_step, slot):
        bi = lin_step // steps
        p0 = (lin_step % steps) * n_pages
        for c in gather(lin_step, slot, [pt_ref[bi, p0 + j] for j in range(n_pages)]):
            c.start()

    @pl.when(lin == 0)
    def _():
        for d in range(n_slots - 1):
            start_gather(d, d)

    @pl.when(lin + (n_slots - 1) < total)
    def _():
        start_gather(lin + (n_slots - 1), (lin + (n_slots - 1)) % n_slots)

    slot = lin % n_slots
    for c in gather(lin, slot):
        c.wait()
    k_refs = [kbuf.at[slot, j] for j in range(n_pages)]
    v_refs = [vbuf.at[slot, j] for j in range(n_pages)]
    lf_refs = [lfbuf.at[slot, j] for j in range(n_pages)]
    head_of_lane = lax.broadcasted_iota(jnp.int32, (n_heads, width), 1) // head_dim
    own = head_of_lane == lax.broadcasted_iota(jnp.int32, (n_heads, width), 0)

    @pl.when(step == 0)
    def _():
        q = q_ref[...]
        for qq in range(nq):
            blk = jnp.where(own, jnp.broadcast_to(q[qq:qq + 1, :], (n_heads, width)), 0.0)
            qbd_ref[pl.ds(qq * n_heads, n_heads), :] = blk
        m_ref[...] = jnp.full(m_ref.shape, -jnp.inf, F32)
        l_ref[...] = jnp.zeros(l_ref.shape, F32)
        acc_ref[...] = jnp.zeros(acc_ref.shape, F32)
        pref_ref[...] = jnp.zeros(pref_ref.shape, F32)

    def attend(k_list, v_list, lf_list, causal):
        n = len(k_list)
        ys = [lf[...] for lf in lf_list]
        y = jnp.concatenate(ys, axis=0) if n > 1 else ys[0]
        pieces = jnp.concatenate(_split3(y), axis=0)
        cw = _dot(pieces, tri_ref[...])
        nh = n * n_heads
        cw = cw[:nh] + cw[nh:2 * nh] + cw[2 * nh:]
        qbd = qbd_ref[...].astype(BF16)
        pref = pref_ref[...]
        s_list = []
        for g in range(n):
            cg = cw[g * n_heads:(g + 1) * n_heads]
            c = cg + pref
            pref = pref + cg[:, page - 1:page]
            s = _dot(qbd, k_list[g][...].astype(BF16))
            s = s - jnp.concatenate([c] * nq, axis=0)
            if causal:
                key = lax.broadcasted_iota(jnp.int32, (rows, page), 1)
                qpos = lax.broadcasted_iota(jnp.int32, (rows, page), 0) // n_heads
                s = jnp.where(key <= qpos, s, -jnp.inf)
            s_list.append(s)
        pref_ref[...] = pref
        m, l, acc = m_ref[...], l_ref[...], acc_ref[...]
        group = max(n // PAGE_GROUPS, 1)
        for g0 in range(0, n, group):
            s_grp = jnp.concatenate(s_list[g0:g0 + group], axis=1) if group > 1 else s_list[g0]
            m_new = jnp.maximum(m, jnp.max(s_grp, axis=1, keepdims=True))
            alpha = jnp.exp(m - m_new)
            p = jnp.exp(s_grp - m_new)
            l = alpha * l + jnp.sum(p, axis=1, keepdims=True)
            pv = None
            for g in range(group):
                part = _dot_nt(p[:, g * page:(g + 1) * page].astype(BF16), v_list[g0 + g][...].astype(BF16))
                pv = part if pv is None else pv + part
            acc = alpha * acc + pv
            m = m_new
        m_ref[...], l_ref[...], acc_ref[...] = m, l, acc

    attend(k_refs, v_refs, lf_refs, causal=False)

    @pl.when(step == pl.num_programs(1) - 1)
    def _():
        attend([knew_ref], [vnew_ref], [lfnew_ref], causal=True)
        o = acc_ref[...] / l_ref[...]
        out_rows = []
        for qq in range(nq):
            blk = jnp.where(own, o[qq * n_heads:(qq + 1) * n_heads, :], 0.0)
            out_rows.append(jnp.sum(blk, axis=0, keepdims=True))
        o_ref[...] = jnp.concatenate(out_rows, axis=0).astype(o_ref.dtype)


def _paged_attention(q, kt_new, vt_new, lf_new, kt_pool, vt_pool, lf_pool, page_table, *, head_dim):
    batch, nq, width = q.shape
    page = kt_pool.shape[2]
    n_heads = lf_pool.shape[1]
    n_pages = PAGES_PER_STEP
    steps = page_table.shape[1] // n_pages
    rows = nq * n_heads
    tri = (jnp.arange(page)[:, None] <= jnp.arange(page)[None, :]).astype(BF16)

    assert batch * steps >= PAGE_SLOTS - 1
    per_batch = lambda shape: pl.BlockSpec((None,) + shape, lambda b, s, pt: (b, 0, 0))
    in_hbm = pl.BlockSpec(memory_space=pl.ANY)
    grid_spec = pltpu.PrefetchScalarGridSpec(
        num_scalar_prefetch=1,
        grid=(batch, steps),
        in_specs=[per_batch((nq, width)), per_batch((width, page)), per_batch((width, page)),
                  per_batch((n_heads, page)), pl.BlockSpec(tri.shape, lambda b, s, pt: (0, 0)),
                  in_hbm, in_hbm, in_hbm],
        out_specs=per_batch((nq, width)),
        scratch_shapes=[pltpu.VMEM((rows, width), F32), pltpu.VMEM((rows, 1), F32),
                        pltpu.VMEM((rows, 1), F32), pltpu.VMEM((rows, width), F32),
                        pltpu.VMEM((n_heads, 1), F32),
                        pltpu.VMEM((PAGE_SLOTS, n_pages, width, page), F32),
                        pltpu.VMEM((PAGE_SLOTS, n_pages, width, page), F32),
                        pltpu.VMEM((PAGE_SLOTS, n_pages, n_heads, page), F32),
                        pltpu.SemaphoreType.DMA((3, PAGE_SLOTS))],
    )
    return pl.pallas_call(
        functools.partial(_paged_attn_kernel, n_pages=n_pages, n_heads=n_heads, head_dim=head_dim),
        grid_spec=grid_spec,
        out_shape=jax.ShapeDtypeStruct((batch, nq, width), BF16),
        compiler_params=_params("arbitrary", "arbitrary"),
        name="paged_attention",
    )(page_table, q, kt_new, vt_new, lf_new, tri, kt_pool, vt_pool, lf_pool)


def _c_proj_kernel(*refs, n_heads, key_dim, seq_len, tiles_per_seq, n_over):
    x_ref, g_ref, wqkv_ref, wz_ref, wba_ref, taps_ref, alog_ref, dtb_ref = refs[:8]
    over = list(refs[8:8 + n_over]) if n_over else None
    (q_ref, k_ref, v_ref, z_ref, beta_ref, gdec_ref, u_ref, ext_ref) = refs[8 + n_over:8 + n_over + 8]
    y_ref = None if over else refs[-1]
    qk_width = n_heads * key_dim
    conv_width = wqkv_ref.shape[1]

    h = _rms(x_ref[...], g_ref[...]).astype(BF16)
    tm = h.shape[0]
    first = pl.program_id(0) % tiles_per_seq == 0 if over is None else pl.program_id(0) == 0
    _conv_prologue(ext_ref, first, conv_width)

    cw = CONV_CHUNK if over is None else conv_width
    nxt = _dot(h, wqkv_ref[:, :cw])
    for c0 in range(0, conv_width, cw):
        cur = nxt
        if c0 + cw < conv_width:
            nxt = _dot(h, wqkv_ref[:, c0 + cw:c0 + 2 * cw])
        y = _silu(_causal_conv(cur, taps_ref, ext_ref, over, seq_len, y_ref, c0))
        if over is None:
            tail = cur[tm - SUBLANES:, :]
            u_ref[:, c0:c0 + cw] = tail
            _conv_carry(ext_ref, tail, c0)
        else:
            u_ref[...] = cur
        for b0 in range(0, cw, key_dim):
            col = c0 + b0
            blk = y[:, b0:b0 + key_dim]
            if col < 2 * qk_width:
                blk = blk * lax.rsqrt(jnp.sum(blk * blk, axis=-1, keepdims=True) + NORM_EPS)
            if col < qk_width:
                q_ref[:, col:col + key_dim] = (blk * key_dim ** -0.5).astype(BF16)
            elif col < 2 * qk_width:
                k_ref[:, col - qk_width:col - qk_width + key_dim] = blk.astype(BF16)
            else:
                v_ref[:, col - 2 * qk_width:col - 2 * qk_width + key_dim] = blk.astype(BF16)

    z_ref[...] = _dot(h, wz_ref[...]).astype(BF16)
    ba = _dot(h, wba_ref[...])
    beta_ref[...] = jax.nn.sigmoid(ba)[:, :n_heads]
    gdec = -jnp.exp(alog_ref[...]) * _softplus(ba + dtb_ref[...])
    gdec_ref[...] = gdec[:, n_heads:2 * n_heads]


def _c_proj(x, gain, wqkv, wz, wba, taps, alog, dtb, overrides, *, n_heads, key_dim, seq_len, tm):
    t, d = x.shape
    conv_width = wqkv.shape[1]
    qk_width = n_heads * key_dim
    v_width = conv_width - 2 * qk_width
    n_tiles = t // tm
    n_over = 0 if overrides is None else len(overrides)
    row = lambda c: pl.BlockSpec((tm, c), lambda i: (i, 0))
    in_specs = [row(d)] + [_const_spec(a) for a in (gain, wqkv, wz, wba, taps, alog, dtb)]
    in_specs += [row(conv_width)] * n_over
    if overrides is None:
        u_spec = pl.BlockSpec((None, SUBLANES, conv_width), lambda i: (i, 0, 0))
        u_shape = jax.ShapeDtypeStruct((n_tiles, SUBLANES, conv_width), F32)
    else:
        u_spec = row(conv_width)
        u_shape = jax.ShapeDtypeStruct((t, conv_width), F32)
    out_specs = [row(qk_width), row(qk_width), row(v_width), row(v_width), row(n_heads), row(n_heads), u_spec]
    out_shape = [jax.ShapeDtypeStruct((t, qk_width), BF16), jax.ShapeDtypeStruct((t, qk_width), BF16),
                 jax.ShapeDtypeStruct((t, v_width), BF16), jax.ShapeDtypeStruct((t, v_width), BF16),
                 jax.ShapeDtypeStruct((t, n_heads), F32), jax.ShapeDtypeStruct((t, n_heads), F32), u_shape]
    return pl.pallas_call(
        functools.partial(_c_proj_kernel, n_heads=n_heads, key_dim=key_dim, seq_len=seq_len,
                          tiles_per_seq=max(seq_len // tm, 1), n_over=n_over),
        grid=(n_tiles,),
        in_specs=in_specs,
        out_specs=out_specs,
        out_shape=out_shape,
        scratch_shapes=_conv_scratch(tm, conv_width, overrides is None),
        compiler_params=_params("arbitrary"),
        name="c_proj",
    )(x, gain, wqkv, wz, wba, taps, alog, dtb, *(overrides or []))


def _delta_kernel(q_ref, k_ref, v_ref, z_ref, gcol_ref, beta_ref, grow_ref, s0_ref, nw_ref,
                  o_ref, s_out_ref, m_ref, *, n_heads, chunk):
    step = pl.program_id(1)
    rows = q_ref.shape[0]
    dk = q_ref.shape[1] // n_heads
    dv = v_ref.shape[1] // n_heads
    n_chunks = rows // chunk
    ri = lax.broadcasted_iota(jnp.int32, (chunk, chunk), 0)
    ci = lax.broadcasted_iota(jnp.int32, (chunk, chunk), 1)
    incl = (ci <= ri)[None]
    strict = (ci < ri)[None]

    @pl.when(step == 0)
    def _():
        for h in range(n_heads):
            m_ref[h] = s0_ref[h].T

    probs = [(c, h) for c in range(n_chunks) for h in range(n_heads)]
    rs = lambda c: slice(c * chunk, (c + 1) * chunk)
    stack = lambda f: jnp.stack([f(c, h) for c, h in probs])
    kb = stack(lambda c, h: k_ref[rs(c), h * dk:(h + 1) * dk])
    qb = stack(lambda c, h: q_ref[rs(c), h * dk:(h + 1) * dk])
    vf = stack(lambda c, h: v_ref[rs(c), h * dv:(h + 1) * dv]).astype(F32)
    gc = stack(lambda c, h: gcol_ref[rs(c), h:h + 1])
    gr = stack(lambda c, h: grow_ref[h:h + 1, rs(c)])
    bc = stack(lambda c, h: beta_ref[rs(c), h:h + 1])
    kf = kb.astype(F32)
    eg = jnp.exp(gc)
    decay = jnp.where(incl, jnp.exp(jnp.where(incl, gc - gr, 0.0)), 0.0)
    kqk = _bdot_nt(jnp.concatenate([kb, qb], axis=1), kb)
    a = jnp.where(strict, bc * kqk[:, :chunk] * decay, 0.0)
    nil = -a
    pw = a
    span = 2
    while span < chunk:
        pwb = pw.astype(BF16)
        pw = _bdot(pwb, pwb)
        nil = nil + pw + _bdot(nil.astype(BF16), pw.astype(BF16))
        span *= 2
    rhs = jnp.concatenate([bc * vf, (bc * eg) * kf], axis=2)
    w = rhs + _bdot(nil.astype(BF16), rhs.astype(BF16))
    w_v = w[:, :, :dv]
    p = (kqk[:, chunk:] * decay).astype(BF16)
    g_last = gc[:, chunk - 1:chunk, :]
    lhs = jnp.concatenate([w[:, :, dv:], eg * qb.astype(F32)], axis=1).astype(BF16)
    k_dec = (jnp.exp(g_last - gc) * kf).astype(BF16)
    g_chunk = jnp.exp(g_last)

    for c in range(n_chunks):
        hs = slice(c * n_heads, (c + 1) * n_heads)
        m = m_ref[...]
        r = _bdot(lhs[hs], m.astype(BF16))
        ub = (w_v[hs] - r[:, :chunk]).astype(BF16)
        o = r[:, chunk:] + _bdot(p[hs], ub)
        m_ref[...] = g_chunk[hs] * m + _bdot_tn(k_dec[hs], ub)
        on = o * lax.rsqrt(jnp.mean(o * o, axis=-1, keepdims=True) + NORM_EPS) * nw_ref[...]
        for h in range(n_heads):
            vl = slice(h * dv, (h + 1) * dv)
            o_ref[rs(c), vl] = (on[h] * _silu(z_ref[rs(c), vl].astype(F32))).astype(BF16)

    @pl.when(step == pl.num_programs(1) - 1)
    def _():
        for h in range(n_heads):
            s_out_ref[h] = m_ref[h].T


def _delta_rule(q, k, v, z, gcol, beta, grow, s0, norm_w, *, batch, seq_len, n_heads, chunk, rows):
    t = q.shape[0]
    steps = seq_len // rows
    dk = q.shape[1] // n_heads
    dv = v.shape[1] // n_heads
    row = lambda c: pl.BlockSpec((rows, c), lambda b, s: (b * steps + s, 0))
    state = pl.BlockSpec((None, n_heads, dv, dk), lambda b, s: (b, 0, 0, 0))
    return pl.pallas_call(
        functools.partial(_delta_kernel, n_heads=n_heads, chunk=chunk),
        grid=(batch, steps),
        in_specs=[row(q.shape[1]), row(k.shape[1]), row(v.shape[1]), row(z.shape[1]),
                  row(n_heads), row(n_heads),
                  pl.BlockSpec((None, n_heads, rows), lambda b, s: (b, 0, s)),
                  state, pl.BlockSpec(norm_w.shape, lambda b, s: (0, 0))],
        out_specs=[row(v.shape[1]), state],
        out_shape=[jax.ShapeDtypeStruct((t, v.shape[1]), BF16),
                   jax.ShapeDtypeStruct((batch, n_heads, dv, dk), F32)],
        scratch_shapes=[pltpu.VMEM((n_heads, dk, dv), F32)],
        compiler_params=_params("arbitrary", "arbitrary"),
        name="gated_delta_rule",
    )(q, k, v, z, gcol, beta, grow, s0, norm_w)


def _pad_lanes(a):
    return jnp.pad(a, ((0, 0), (0, LANES - a.shape[1])))


def _rows_to_heads(a, batch, seq_len):
    return a.reshape(batch, seq_len, a.shape[1]).transpose(0, 2, 1)


def _heads_to_rows(a):
    b, h, length = a.shape
    return a.transpose(0, 2, 1).reshape(b * length, h)


def _feature_major_pool(pool):
    n_l, n_p, page = pool.shape[:3]
    perm = (0, 1) + tuple(range(3, pool.ndim)) + (2,)
    return pool.transpose(perm).reshape(n_l * n_p, -1, page)


def _mixer_ab(x, past, w, i, gain, *, batch, seq_len, tm):
    t = x.shape[0]
    hd = w["a_head_dim"]
    nh = w["ab_b_f"].shape[1]
    aw = nh * hd
    w_in = w["ab_w_in"][i]
    bw = (w_in.shape[1] - 3 * aw - nh) // 3
    wb = w_in[:, 3 * aw + nh:].astype(BF16)
    wf = w_in[:, 3 * aw:3 * aw + nh]
    taps = w["ab_conv_w"][i]
    n_keep = taps.shape[0] - 1
    if past is None:
        wq = w_in[:, :aw].astype(BF16)
        wkvt = w_in[:, aw:3 * aw].T.astype(BF16)
        kt, vt, ktb, vtb, lft, qb, ob, tails = _ab_proj_prompt(
            x, gain, wq, wkvt, wf.T.astype(BF16), w["ab_b_f"][i][:, None], wb, taps,
            batch=batch, seq_len=seq_len, head_dim=hd, tm=tm)
        o_a = _fox_attention(qb, ktb, vtb, _forget_bias_rows(lft, hd), head_dim=hd,
                             tile=min(1024, seq_len), kblock=min(512, seq_len))
        tails = tails.reshape(batch, seq_len // tm, SUBLANES, bw)[:, -1]
        new_buf = tails[:, SUBLANES - n_keep:, :]
        to_tokens = lambda a: a.reshape(batch, nh, hd, seq_len).transpose(0, 3, 1, 2)
        state = (to_tokens(kt), to_tokens(vt), lft.transpose(0, 2, 1), new_buf)
    else:
        overrides = _conv_overrides(past["state_b_conv"][i], seq_len)
        q_s, k32, v32, logf, ob, gcu = _ab_proj_sample(
            x, gain, w_in[:, :3 * aw].astype(BF16), _pad_lanes(wf).astype(BF16),
            _pad_lanes(w["ab_b_f"][i][None, :]), wb, taps, overrides,
            n_heads=nh, head_dim=hd, seq_len=seq_len)
        n_phys, page = past["cache_a_k"].shape[1:3]
        pad_keys = lambda a: jnp.pad(a.reshape(batch, seq_len, -1).transpose(0, 2, 1),
                                     ((0, 0), (0, 0), (0, page - seq_len)))
        o_a = _paged_attention(
            q_s.reshape(batch, seq_len, aw), pad_keys(k32), pad_keys(v32), pad_keys(logf),
            _feature_major_pool(past["cache_a_k"]), _feature_major_pool(past["cache_a_v"]),
            _feature_major_pool(past["cache_a_logf"]), past["page_table"] + i * n_phys, head_dim=hd)
        o_a = o_a.reshape(t, aw)
        new_buf = gcu.reshape(batch, seq_len, bw)[:, seq_len - n_keep:, :]
        state = (k32.reshape(batch, seq_len, nh, hd), v32.reshape(batch, seq_len, nh, hd),
                 logf.reshape(batch, seq_len, nh), new_buf)
    return [o_a, ob], w["ab_w_out"][i].astype(BF16), state


def _mixer_c(x, past, w, i, gain, *, batch, seq_len, tm):
    t = x.shape[0]
    nh = w["c_a_log"].shape[1]
    dv = w["c_norm_w"].shape[1]
    w_in = w["c_w_in"][i]
    v_width = nh * dv
    conv_width = w_in.shape[1] - v_width - 2 * nh
    dk = (conv_width - v_width) // 2 // nh
    wqkv = w_in[:, :conv_width].astype(BF16)
    wz = w_in[:, conv_width:conv_width + v_width].astype(BF16)
    wba = _pad_lanes(w_in[:, conv_width + v_width:]).astype(BF16)
    zeros_h = jnp.zeros((1, nh), F32)
    alog = _pad_lanes(jnp.concatenate([zeros_h, w["c_a_log"][i][None, :]], axis=1))
    dtb = _pad_lanes(jnp.concatenate([zeros_h, w["c_dt_bias"][i][None, :]], axis=1))
    taps = w["c_conv_w"][i]
    n_keep = taps.shape[0] - 1
    tm_c = min(tm, 512)
    if past is None:
        overrides = None
        s0 = jnp.zeros((batch, nh, dv, dk), F32)
    else:
        overrides = _conv_overrides(past["state_c_conv"][i], seq_len)
        s0 = past["state_c_S"][i]
    qn, kn, vn, zb, beta, gdec, u_out = _c_proj(
        x, gain, wqkv, wz, wba, taps, alog, dtb, overrides,
        n_heads=nh, key_dim=dk, seq_len=seq_len, tm=tm_c)
    if past is None:
        tails = u_out.reshape(batch, seq_len // tm_c, SUBLANES, conv_width)[:, -1]
        new_buf = tails[:, SUBLANES - n_keep:, :]
        len_pad = seq_len
    else:
        new_buf = u_out.reshape(batch, seq_len, conv_width)[:, seq_len - n_keep:, :]
        len_pad = -(-seq_len // DELTA_CHUNK) * DELTA_CHUNK
        pad3 = lambda a: jnp.pad(a.reshape(batch, seq_len, a.shape[1]),
                                 ((0, 0), (0, len_pad - seq_len), (0, 0))).reshape(batch * len_pad, a.shape[1])
        qn, kn, vn, zb, beta, gdec = [pad3(a) for a in (qn, kn, vn, zb, beta, gdec)]
    scan_len = -(-len_pad // LANES) * LANES
    g_heads = jnp.pad(_rows_to_heads(gdec, batch, len_pad), ((0, 0), (0, 0), (0, scan_len - len_pad)))
    grow = _segment_cumsum(g_heads, DELTA_CHUNK)[:, :, :len_pad]
    og, s_new = _delta_rule(qn, kn, vn, zb, _heads_to_rows(grow), beta, grow, s0, w["c_norm_w"][i][None, :],
                            batch=batch, seq_len=len_pad, n_heads=nh, chunk=DELTA_CHUNK,
                            rows=min(DELTA_ROWS, len_pad))
    if len_pad != seq_len:
        og = og.reshape(batch, len_pad, v_width)[:, :seq_len].reshape(t, v_width)
    return [og], w["c_w_out"][i].astype(BF16), (new_buf, s_new)


def _run_trunk(x3, past, w, *, tm):
    batch, seq_len, d = x3.shape
    x = x3.reshape(batch * seq_len, d)
    ab_states, c_states = [], []
    ffn = lambda x, g0, g1, layer, j, mixer=None: _ffn_block(
        x, g0, g1, w["ffn_w_gate"], w["ffn_w_up"], w["ffn_w_down"], layer, j, tm=tm, mixer=mixer)
    for layer in range(w["norm_g"].shape[0]):
        gains = [w["norm_g"][layer, j][None, :] for j in range(6)]
        x = ffn(x, gains[0], gains[1], layer, 0)
        mixer, states = (_mixer_ab, ab_states) if layer % 2 == 0 else (_mixer_c, c_states)
        parts, w_out, state = mixer(x, past, w, layer // 2, gains[2], batch=batch, seq_len=seq_len, tm=tm)
        states.append(state)
        x = ffn(x, gains[4], gains[5], layer, 1, mixer=(parts, w_out, gains[3]))
    ab_new = [jnp.stack(ts) for ts in zip(*ab_states)]
    c_new = [jnp.stack(ts) for ts in zip(*c_states)]
    return x.reshape(batch, seq_len, d), ab_new, c_new


def kernel(x_prompt, x_sample, cache_a_k, cache_a_v, cache_a_logf, page_table, state_b_conv, state_c_conv, state_c_S, norm_g, ffn_w_gate, ffn_w_up, ffn_w_down, ab_w_in, ab_b_f, ab_conv_w, ab_w_out, c_w_in, c_conv_w, c_a_log, c_dt_bias, c_norm_w, c_w_out):
    w = dict(norm_g=norm_g,
             ffn_w_gate=ffn_w_gate.astype(BF16), ffn_w_up=ffn_w_up.astype(BF16), ffn_w_down=ffn_w_down.astype(BF16),
             ab_w_in=ab_w_in, ab_b_f=ab_b_f, ab_conv_w=ab_conv_w, ab_w_out=ab_w_out,
             c_w_in=c_w_in, c_conv_w=c_conv_w, c_a_log=c_a_log, c_dt_bias=c_dt_bias,
             c_norm_w=c_norm_w, c_w_out=c_w_out, a_head_dim=cache_a_k.shape[-1])
    past = dict(cache_a_k=cache_a_k, cache_a_v=cache_a_v, cache_a_logf=cache_a_logf, page_table=page_table,
                state_b_conv=state_b_conv, state_c_conv=state_c_conv, state_c_S=state_c_S)
    prompt_rows = x_prompt.shape[0] * x_prompt.shape[1]
    sample_rows = x_sample.shape[0] * x_sample.shape[1]
    y_prompt, ab_p, c_p = _run_trunk(x_prompt, None, w, tm=min(512, prompt_rows))
    y_sample, ab_s, c_s = _run_trunk(x_sample, past, w, tm=min(512, sample_rows))
    a_k_prompt, a_v_prompt, a_logf_prompt, b_conv_prompt = ab_p
    a_k_sample, a_v_sample, a_logf_sample, b_conv_sample = ab_s
    c_conv_prompt, c_S_prompt = c_p
    c_conv_sample, c_S_sample = c_s
    return (y_prompt, y_sample,
            a_k_prompt, a_v_prompt, a_logf_prompt,
            a_k_sample, a_v_sample, a_logf_sample,
            b_conv_prompt, b_conv_sample,
            c_conv_prompt, c_conv_sample,
            c_S_prompt, c_S_sample)
```

```python
import functools

import jax
import jax.numpy as jnp
from jax import lax
from jax.experimental import pallas as pl
from jax.experimental.pallas import tpu as pltpu

F32 = jnp.float32
BF16 = jnp.bfloat16
NORM_EPS = 1e-6
LOG2_E = 1.4426950408889634

LANES = 128
SUBLANES = 8
VMEM_LIMIT_BYTES = 56 * 1024 * 1024
DELTA_CHUNK = 64
DELTA_ROWS = 512
DELTA_BATCH = 4
CONV_ROW_STRIDE = 4
CONV_CHUNK = 512
PAGES_PER_STEP = 16
PAGE_GROUPS = 2
PAGE_SLOTS = 3


def _params(*semantics):
    return pltpu.CompilerParams(dimension_semantics=semantics, vmem_limit_bytes=VMEM_LIMIT_BYTES)


def _resident(shape, index_map):
    return pl.BlockSpec(shape, index_map, pipeline_mode=pl.Buffered(1))


def _const_spec(a):
    return _resident(a.shape, lambda *_: (0,) * a.ndim)


def _rms(x, g):
    return x * lax.rsqrt(jnp.mean(x * x, axis=-1, keepdims=True) + NORM_EPS) * g


def _softplus(x):
    return jnp.maximum(x, 0.0) + jnp.log1p(jnp.exp(-jnp.abs(x)))


def _silu(x):
    return x * jax.nn.sigmoid(x)


def _dot(a, b):
    return jnp.dot(a, b, preferred_element_type=F32)


def _dot_nt(a, b):
    return lax.dot_general(a, b, (((1,), (1,)), ((), ())), preferred_element_type=F32)


def _dot_tn(a, b):
    return lax.dot_general(a, b, (((0,), (0,)), ((), ())), preferred_element_type=F32)


def _bdot(a, b):
    return lax.dot_general(a, b, (((2,), (1,)), ((0,), (0,))), preferred_element_type=F32)


def _bdot_nt(a, b):
    return lax.dot_general(a, b, (((2,), (2,)), ((0,), (0,))), preferred_element_type=F32)


def _bdot_tn(a, b):
    return lax.dot_general(a, b, (((1,), (1,)), ((0,), (0,))), preferred_element_type=F32)


def _mix_rows(part_refs, w_ref):
    mix = None
    k0 = 0
    for a_ref in part_refs:
        kw = a_ref.shape[1]
        part = _dot(a_ref[...], w_ref[k0:k0 + kw, :])
        mix = part if mix is None else mix + part
        k0 += kw
    return mix


def _ffn_kernel(*refs, n_parts, ff_chunk):
    part_refs = refs[:n_parts]
    if n_parts:
        w_mix_ref, g_mix_ref = refs[n_parts:n_parts + 2]
        refs = refs[n_parts + 2:]
    x_ref, gpre_ref, gpost_ref, wg_ref, wu_ref, wd_ref, o_ref = refs
    x = x_ref[...]
    if n_parts:
        x = x + _rms(_mix_rows(part_refs, w_mix_ref), g_mix_ref[...])
    h = _rms(x, gpre_ref[...]).astype(BF16)
    d_ff = wg_ref.shape[1]
    proj = lambda c0: (_dot(h, wg_ref[:, c0:c0 + ff_chunk].astype(BF16)),
                       _dot(h, wu_ref[:, c0:c0 + ff_chunk].astype(BF16)))
    y = None
    nxt = proj(0)
    for c0 in range(0, d_ff, ff_chunk):
        gate, up = nxt
        if c0 + ff_chunk < d_ff:
            nxt = proj(c0 + ff_chunk)
        act = (_silu(gate) * up).astype(BF16)
        part = _dot(act, wd_ref[c0:c0 + ff_chunk, :].astype(BF16))
        y = part if y is None else y + part
    o_ref[...] = x + 0.5 * _rms(y, gpost_ref[...])


def _ffn_chunk(d_ff):
    for c in range(4 * LANES, 0, -LANES):
        if d_ff % c == 0:
            return c
    return d_ff


def _ffn_block(x, g_pre, g_post, wg_all, wu_all, wd_all, layer, j, *, tm, mixer=None):
    t, d = x.shape
    d_ff = wg_all.shape[-1]
    row = lambda c: pl.BlockSpec((tm, c), lambda i: (i, 0))
    gain = _resident((1, d), lambda i: (0, 0))
    w_in = _resident((None, None, d, d_ff), lambda i: (layer, j, 0, 0))
    w_out = _resident((None, None, d_ff, d), lambda i: (layer, j, 0, 0))
    parts, w_mix, g_mix = mixer if mixer is not None else ([], None, None)
    mix_args = list(parts) + ([w_mix, g_mix] if parts else [])
    mix_specs = [row(p.shape[1]) for p in parts] + ([_const_spec(w_mix), gain] if parts else [])
    return pl.pallas_call(
        functools.partial(_ffn_kernel, n_parts=len(parts), ff_chunk=_ffn_chunk(d_ff)),
        grid=(t // tm,),
        in_specs=mix_specs + [row(d), gain, gain, w_in, w_in, w_out],
        out_specs=row(d),
        out_shape=jax.ShapeDtypeStruct((t, d), F32),
        compiler_params=_params("arbitrary"),
        name="ffn_block",
    )(*mix_args, x, g_pre, g_post, wg_all, wu_all, wd_all)


def _causal_conv(u, taps_ref, ext_ref, overrides, seq_len, y_ref=None, col0=0):
    tm = u.shape[0]
    n_taps = taps_ref.shape[0]
    if overrides is None:
        stride = CONV_ROW_STRIDE
        group = SUBLANES * stride
        assert n_taps <= stride + 1 <= SUBLANES + 1
        blocks = range(col0 // LANES, (col0 + u.shape[1]) // LANES)
        for lb in blocks:
            lanes = slice(lb * LANES, (lb + 1) * LANES)
            ext_ref[lb, pl.ds(SUBLANES, tm), :] = u[:, lb * LANES - col0:(lb + 1) * LANES - col0]
            taps = [taps_ref[n_taps - 1 - j:n_taps - j, lanes] for j in range(n_taps)]
            for g0 in range(0, tm, group):
                win = {d: ext_ref[lb, pl.ds(SUBLANES + g0 + d, SUBLANES, stride=stride), :]
                       for d in range(1 - n_taps, stride)}
                for i in range(stride):
                    acc = win[i] * taps[0]
                    for j in range(1, n_taps):
                        acc = acc + win[i - j] * taps[j]
                    y_ref[lb, pl.ds(g0 + i, SUBLANES, stride=stride), :] = acc
        return jnp.concatenate([y_ref[lb] for lb in blocks], axis=1)
    ext_ref[pl.ds(SUBLANES, tm), :] = u
    out = u * taps_ref[n_taps - 1:n_taps, :]
    pos = lax.broadcasted_iota(jnp.int32, u.shape, 0) % seq_len
    for j in range(1, n_taps):
        prev = jnp.where(pos >= j, ext_ref[pl.ds(SUBLANES - j, tm), :], overrides[j - 1][...])
        out = out + prev * taps_ref[n_taps - 1 - j:n_taps - j, :]
    return out


def _conv_overrides(state, seq_len):
    b, wm1, c = state.shape
    assert seq_len >= wm1
    outs = []
    for j in range(1, wm1 + 1):
        rows = jnp.pad(state[:, wm1 - j:, :], ((0, 0), (0, seq_len - j), (0, 0)))
        outs.append(rows.reshape(b * seq_len, c))
    return outs


def _conv_carry(ext_ref, tail, col0=0):
    if len(ext_ref.shape) == 2:
        ext_ref[pl.ds(0, SUBLANES), :] = tail
    else:
        for lb in range(tail.shape[1] // LANES):
            ext_ref[col0 // LANES + lb, pl.ds(0, SUBLANES), :] = tail[:, lb * LANES:(lb + 1) * LANES]


def _conv_prologue(ext_ref, first, width):
    @pl.when(first)
    def _():
        _conv_carry(ext_ref, jnp.zeros((SUBLANES, width), F32))


def _conv_scratch(tm, width, blocked):
    if not blocked:
        return [pltpu.VMEM((SUBLANES + tm, width), F32)]
    assert tm % (SUBLANES * CONV_ROW_STRIDE) == 0 and width % LANES == 0
    return [pltpu.VMEM((width // LANES, SUBLANES + tm, LANES), F32), pltpu.VMEM((width // LANES, tm, LANES), F32)]


def _log_sigmoid(z):
    return -_softplus(-z)


def _ab_proj_prompt_kernel(x_ref, g_ref, wq_ref, wkvt_ref, wft_ref, bf_ref, wb_ref, taps_ref,
                           kt_ref, vt_ref, ktb_ref, vtb_ref, lft_ref, qb_ref, ob_ref, tail_ref, ext_ref, y_ref,
                           *, head_dim, tiles_per_seq):
    a_width = wq_ref.shape[1]
    b_width = wb_ref.shape[1] // 3
    h = _rms(x_ref[...], g_ref[...]).astype(BF16)
    qb_ref[...] = (_dot(h, wq_ref[...]) * (head_dim ** -0.5 * LOG2_E)).astype(BF16)
    kvt = _dot_nt(wkvt_ref[...], h)
    kt = kvt[:a_width]
    vt = kvt[a_width:]
    kt_ref[...] = kt
    vt_ref[...] = vt
    ktb_ref[...] = kt.astype(BF16)
    vtb_ref[...] = vt.astype(BF16)
    lft_ref[...] = _log_sigmoid(_dot_nt(wft_ref[...], h) + bf_ref[...])

    gbu = _dot(h, wb_ref[...])
    gate_b = gbu[:, :b_width]
    gcu = gbu[:, b_width:2 * b_width] * gbu[:, 2 * b_width:]
    _conv_prologue(ext_ref, pl.program_id(0) % tiles_per_seq == 0, b_width)
    y = _causal_conv(gcu, taps_ref, ext_ref, None, None, y_ref)
    ob_ref[...] = (gate_b * y).astype(BF16)
    tail = gcu[gcu.shape[0] - SUBLANES:, :]
    tail_ref[...] = tail
    _conv_carry(ext_ref, tail)


def _ab_proj_prompt(x, gain, wq, wkvt, wft, bf_col, wb, taps, *, batch, seq_len, head_dim, tm):
    t, d = x.shape
    a_width = wq.shape[1]
    n_heads = wft.shape[0]
    b_width = wb.shape[1] // 3
    tiles_per_seq = seq_len // tm
    row = lambda c: pl.BlockSpec((tm, c), lambda i: (i, 0))
    feat = lambda c: pl.BlockSpec((None, c, tm), lambda i: (i // tiles_per_seq, 0, i % tiles_per_seq))
    feat_shape = lambda c, dt: jax.ShapeDtypeStruct((batch, c, seq_len), dt)
    return pl.pallas_call(
        functools.partial(_ab_proj_prompt_kernel, head_dim=head_dim, tiles_per_seq=tiles_per_seq),
        grid=(t // tm,),
        in_specs=[row(d)] + [_const_spec(a) for a in (gain, wq, wkvt, wft, bf_col, wb, taps)],
        out_specs=[feat(a_width), feat(a_width), feat(a_width), feat(a_width), feat(n_heads),
                   row(a_width), row(b_width),
                   pl.BlockSpec((None, SUBLANES, b_width), lambda i: (i, 0, 0))],
        out_shape=[feat_shape(a_width, F32), feat_shape(a_width, F32), feat_shape(a_width, BF16),
                   feat_shape(a_width, BF16), feat_shape(n_heads, F32),
                   jax.ShapeDtypeStruct((t, a_width), BF16), jax.ShapeDtypeStruct((t, b_width), BF16),
                   jax.ShapeDtypeStruct((t // tm, SUBLANES, b_width), F32)],
        scratch_shapes=_conv_scratch(tm, b_width, True),
        compiler_params=_params("arbitrary"),
        name="ab_proj_prompt",
    )(x, gain, wq, wkvt, wft, bf_col, wb, taps)


def _ab_proj_sample_kernel(*refs, n_heads, head_dim, seq_len, n_over):
    x_ref, g_ref, wqkv_ref, wf_ref, bf_ref, wb_ref, taps_ref = refs[:7]
    over = list(refs[7:7 + n_over])
    q_ref, k_ref, v_ref, logf_ref, ob_ref, u_ref, ext_ref = refs[7 + n_over:]
    a_width = n_heads * head_dim
    b_width = wb_ref.shape[1] // 3
    h = _rms(x_ref[...], g_ref[...]).astype(BF16)
    qkv = _dot(h, wqkv_ref[...])
    q_ref[...] = qkv[:, :a_width] * head_dim ** -0.5
    k_ref[...] = qkv[:, a_width:2 * a_width]
    v_ref[...] = qkv[:, 2 * a_width:]
    logf_ref[...] = _log_sigmoid(_dot(h, wf_ref[...]) + bf_ref[...])[:, :n_heads]

    gbu = _dot(h, wb_ref[...])
    gate_b = gbu[:, :b_width]
    gcu = gbu[:, b_width:2 * b_width] * gbu[:, 2 * b_width:]
    _conv_prologue(ext_ref, pl.program_id(0) == 0, b_width)
    y = _causal_conv(gcu, taps_ref, ext_ref, over, seq_len)
    ob_ref[...] = (gate_b * y).astype(BF16)
    u_ref[...] = gcu


def _ab_proj_sample(x, gain, wqkv, wf, bf, wb, taps, overrides, *, n_heads, head_dim, seq_len):
    t, d = x.shape
    a_width = n_heads * head_dim
    b_width = wb.shape[1] // 3
    full = lambda c: pl.BlockSpec((t, c), lambda i: (0, 0))
    f32 = lambda c: jax.ShapeDtypeStruct((t, c), F32)
    return pl.pallas_call(
        functools.partial(_ab_proj_sample_kernel, n_heads=n_heads, head_dim=head_dim, seq_len=seq_len,
                          n_over=len(overrides)),
        grid=(1,),
        in_specs=[full(d)] + [_const_spec(a) for a in (gain, wqkv, wf, bf, wb, taps)]
        + [full(b_width)] * len(overrides),
        out_specs=[full(a_width), full(a_width), full(a_width), full(n_heads), full(b_width), full(b_width)],
        out_shape=[f32(a_width), f32(a_width), f32(a_width), f32(n_heads),
                   jax.ShapeDtypeStruct((t, b_width), BF16), f32(b_width)],
        scratch_shapes=[pltpu.VMEM((SUBLANES + t, b_width), F32)],
        compiler_params=_params("arbitrary"),
        name="ab_proj_sample",
    )(x, gain, wqkv, wf, bf, wb, taps, *overrides)


def _scan_kernel(x_ref, o_ref, *, seg):
    x = x_ref[...]
    pos = lax.broadcasted_iota(jnp.int32, x.shape, 1) % seg
    d = 1
    while d < seg:
        x = x + jnp.where(pos >= d, pltpu.roll(x, d, axis=1), 0.0)
        d *= 2
    o_ref[...] = x


def _segment_cumsum(x, seg):
    b, h, length = x.shape
    spec = pl.BlockSpec((b * h, length), lambda i: (0, 0))
    return pl.pallas_call(
        functools.partial(_scan_kernel, seg=seg),
        grid=(1,),
        in_specs=[spec],
        out_specs=spec,
        out_shape=jax.ShapeDtypeStruct((b * h, length), F32),
        compiler_params=_params("arbitrary"),
        name="segment_cumsum",
    )(x.reshape(b * h, length)).reshape(x.shape)


def _split3(x):
    hi = x.astype(BF16)
    r = x - hi.astype(F32)
    mid = r.astype(BF16)
    lo = (r - mid.astype(F32)).astype(BF16)
    return hi, mid, lo


def _forget_bias_kernel(lf_ref, o_ref, *, head_dim):
    x = lf_ref[...]
    n_heads, length = x.shape
    pos = lax.broadcasted_iota(jnp.int32, x.shape, 1)
    d = 1
    while d < length:
        x = x + jnp.where(pos >= d, pltpu.roll(x, d, axis=1), 0.0)
        d *= 2
    pieces = [p.astype(F32) for p in _split3(-LOG2_E * x)]
    pad = jnp.zeros((head_dim - len(pieces), length), F32)
    for h in range(n_heads):
        blk = jnp.concatenate([p[h:h + 1] for p in pieces] + [pad], axis=0)
        o_ref[pl.ds(h * head_dim, head_dim), :] = blk.astype(BF16)


def _forget_bias_rows(lft, head_dim):
    batch, n_heads, length = lft.shape
    return pl.pallas_call(
        functools.partial(_forget_bias_kernel, head_dim=head_dim),
        grid=(batch,),
        in_specs=[pl.BlockSpec((None, n_heads, length), lambda b: (b, 0, 0))],
        out_specs=pl.BlockSpec((None, n_heads * head_dim, length), lambda b: (b, 0, 0)),
        out_shape=jax.ShapeDtypeStruct((batch, n_heads * head_dim, length), BF16),
        compiler_params=_params("arbitrary"),
        name="forget_bias_rows",
    )(lft)


def _fox_kernel(q_ref, kt_ref, vt_ref, cb_ref, o_ref, *, tile, kblock, head_dim):
    qi = pl.program_id(2)
    heads = q_ref.shape[1] // head_dim
    sub = tile // kblock
    row = lax.broadcasted_iota(jnp.int32, (kblock, kblock), 0)
    col = lax.broadcasted_iota(jnp.int32, (kblock, kblock), 1)
    head_of_lane = lax.broadcasted_iota(jnp.int32, q_ref.shape, 1) // head_dim
    q_pair = q_ref[...]
    ones_rows = jnp.ones((head_dim, kblock), BF16)
    owns = [head_of_lane == j for j in range(heads)]
    q_augs = [jnp.where(own, q_pair, jnp.ones_like(q_pair)) for own in owns]

    def block(ki, carry, q_rows=slice(None), triangular=False):
        off = pl.multiple_of(ki * kblock, kblock)
        feats = [slice(j * head_dim, (j + 1) * head_dim) for j in range(heads)]
        order = lambda j, data, aux: [data, aux] if j == 0 else [aux, data]
        scores = []
        for j in range(heads):
            kt_aug = jnp.concatenate(
                order(j, kt_ref[feats[j], pl.ds(off, kblock)], cb_ref[feats[j], pl.ds(off, kblock)]), axis=0)
            scores.append(_dot(q_augs[j][q_rows], kt_aug))
        new = []
        for j, (m, acc) in enumerate(carry):
            s = jnp.where(col <= row, scores[j], -jnp.inf) if triangular else scores[j]
            m_new = jnp.maximum(m, jnp.max(s, axis=1, keepdims=True))
            p = jnp.exp2(s - m_new).astype(BF16)
            vt_aug = jnp.concatenate(order(j, vt_ref[feats[j], pl.ds(off, kblock)], ones_rows), axis=0)
            new.append((m_new, jnp.exp2(m - m_new) * acc + _dot_nt(p, vt_aug)))
        return tuple(new)

    init = tuple((jnp.full((tile, 1), -jnp.inf, F32), jnp.zeros(q_ref.shape, F32)) for _ in range(heads))
    carry = lax.fori_loop(0, qi * sub, block, init)
    pieces = []
    for i in range(sub):
        q_rows = slice(i * kblock, (i + 1) * kblock)
        piece = tuple((m[q_rows], acc[q_rows]) for m, acc in carry)
        for r in range(i + 1):
            piece = block(qi * sub + r, piece, q_rows, triangular=r == i)
        pieces.append(piece)
    carry = tuple((None, jnp.concatenate([piece[j][1] for piece in pieces], axis=0)) for j in range(heads))
    out = None
    for own, (_, acc) in zip(owns, carry):
        o = acc / pltpu.roll(acc, head_dim, axis=1)
        out = o if out is None else jnp.where(own, o, out)
    o_ref[...] = out.astype(BF16)


def _fox_attention(qb, ktb, vtb, cb, *, head_dim, tile, kblock):
    t, a_width = qb.shape
    batch, _, seq_len = ktb.shape
    nq = seq_len // tile
    q_spec = pl.BlockSpec((tile, LANES), lambda b, hp, qi: (b * nq + qi, hp))
    kv_spec = pl.BlockSpec((None, LANES, seq_len), lambda b, hp, qi: (b, hp, 0))
    return pl.pallas_call(
        functools.partial(_fox_kernel, tile=tile, kblock=kblock, head_dim=head_dim),
        grid=(batch, a_width // LANES, nq),
        in_specs=[q_spec, kv_spec, kv_spec, kv_spec],
        out_specs=q_spec,
        out_shape=jax.ShapeDtypeStruct((t, a_width), BF16),
        compiler_params=_params("arbitrary", "arbitrary", "arbitrary"),
        name="fox_attention",
    )(qb, ktb, vtb, cb)


def _paged_attn_kernel(pt_ref, q_ref, knew_ref, vnew_ref, lfnew_ref, tri_ref, kpool_ref, vpool_ref, lfpool_ref,
                       o_ref, qbd_ref, m_ref, l_ref, acc_ref, pref_ref, kbuf, vbuf, lfbuf, sems,
                       *, n_pages, n_heads, head_dim):
    step = pl.program_id(1)
    steps = pl.num_programs(1)
    lin = pl.program_id(0) * steps + step
    total = pl.num_programs(0) * steps
    n_slots = kbuf.shape[0]
    nq, width = q_ref.shape
    rows = nq * n_heads
    page = kbuf.shape[3]

    def gather(lin_step, slot, pages=None):
        out = []
        for j in range(n_pages):
            pg = 0 if pages is None else pages[j]
            out += [pltpu.make_async_copy(kpool_ref.at[pg], kbuf.at[slot, j], sems.at[0, slot]),
                    pltpu.make_async_copy(vpool_ref.at[pg], vbuf.at[slot, j], sems.at[1, slot]),
                    pltpu.make_async_copy(lfpool_ref.at[pg], lfbuf.at[slot, j], sems.at[2, slot])]
        return out

    def start_gather(lin_step, slot):
        bi = lin_step // steps
        p0 = (lin_step % steps) * n_pages
        for c in gather(lin_step, slot, [pt_ref[bi, p0 + j] for j in range(n_pages)]):
            c.start()

    @pl.when(lin == 0)
    def _():
        for d in range(n_slots - 1):
            start_gather(d, d)

    @pl.when(lin + (n_slots - 1) < total)
    def _():
        start_gather(lin + (n_slots - 1), (lin + (n_slots - 1)) % n_slots)

    slot = lin % n_slots
    for c in gather(lin, slot):
        c.wait()
    k_refs = [kbuf.at[slot, j] for j in range(n_pages)]
    v_refs = [vbuf.at[slot, j] for j in range(n_pages)]
    lf_refs = [lfbuf.at[slot, j] for j in range(n_pages)]
    head_of_lane = lax.broadcasted_iota(jnp.int32, (n_heads, width), 1) // head_dim
    own = head_of_lane == lax.broadcasted_iota(jnp.int32, (n_heads, width), 0)

    @pl.when(step == 0)
    def _():
        q = q_ref[...]
        for qq in range(nq):
            blk = jnp.where(own, jnp.broadcast_to(q[qq:qq + 1, :], (n_heads, width)), 0.0)
            qbd_ref[pl.ds(qq * n_heads, n_heads), :] = blk
        m_ref[...] = jnp.full(m_ref.shape, -jnp.inf, F32)
        l_ref[...] = jnp.zeros(l_ref.shape, F32)
        acc_ref[...] = jnp.zeros(acc_ref.shape, F32)
        pref_ref[...] = jnp.zeros(pref_ref.shape, F32)

    def attend(k_list, v_list, lf_list, causal):
        n = len(k_list)
        ys = [lf[...] for lf in lf_list]
        y = jnp.concatenate(ys, axis=0) if n > 1 else ys[0]
        pieces = jnp.concatenate(_split3(y), axis=0)
        cw = _dot(pieces, tri_ref[...])
        nh = n * n_heads
        cw = cw[:nh] + cw[nh:2 * nh] + cw[2 * nh:]
        qbd = qbd_ref[...].astype(BF16)
        pref = pref_ref[...]
        s_list = []
        for g in range(n):
            cg = cw[g * n_heads:(g + 1) * n_heads]
            c = cg + pref
            pref = pref + cg[:, page - 1:page]
            s = _dot(qbd, k_list[g][...].astype(BF16))
            s = s - jnp.concatenate([c] * nq, axis=0)
            if causal:
                key = lax.broadcasted_iota(jnp.int32, (rows, page), 1)
                qpos = lax.broadcasted_iota(jnp.int32, (rows, page), 0) // n_heads
                s = jnp.where(key <= qpos, s, -jnp.inf)
            s_list.append(s)
        pref_ref[...] = pref
        m, l, acc = m_ref[...], l_ref[...], acc_ref[...]
        group = max(n // PAGE_GROUPS, 1)
        for g0 in range(0, n, group):
            s_grp = jnp.concatenate(s_list[g0:g0 + group], axis=1) if group > 1 else s_list[g0]
            m_new = jnp.maximum(m, jnp.max(s_grp, axis=1, keepdims=True))
            alpha = jnp.exp(m - m_new)
            p = jnp.exp(s_grp - m_new)
            l = alpha * l + jnp.sum(p, axis=1, keepdims=True)
            pv = None
            for g in range(group):
                part = _dot_nt(p[:, g * page:(g + 1) * page].astype(BF16), v_list[g0 + g][...].astype(BF16))
                pv = part if pv is None else pv + part
            acc = alpha * acc + pv
            m = m_new
        m_ref[...], l_ref[...], acc_ref[...] = m, l, acc

    attend(k_refs, v_refs, lf_refs, causal=False)

    @pl.when(step == pl.num_programs(1) - 1)
    def _():
        attend([knew_ref], [vnew_ref], [lfnew_ref], causal=True)
        o = acc_ref[...] / l_ref[...]
        out_rows = []
        for qq in range(nq):
            blk = jnp.where(own, o[qq * n_heads:(qq + 1) * n_heads, :], 0.0)
            out_rows.append(jnp.sum(blk, axis=0, keepdims=True))
        o_ref[...] = jnp.concatenate(out_rows, axis=0).astype(o_ref.dtype)


def _paged_attention(q, kt_new, vt_new, lf_new, kt_pool, vt_pool, lf_pool, page_table, *, head_dim):
    batch, nq, width = q.shape
    page = kt_pool.shape[2]
    n_heads = lf_pool.shape[1]
    n_pages = PAGES_PER_STEP
    steps = page_table.shape[1] // n_pages
    rows = nq * n_heads
    tri = (jnp.arange(page)[:, None] <= jnp.arange(page)[None, :]).astype(BF16)

    assert batch * steps >= PAGE_SLOTS - 1
    per_batch = lambda shape: pl.BlockSpec((None,) + shape, lambda b, s, pt: (b, 0, 0))
    in_hbm = pl.BlockSpec(memory_space=pl.ANY)
    grid_spec = pltpu.PrefetchScalarGridSpec(
        num_scalar_prefetch=1,
        grid=(batch, steps),
        in_specs=[per_batch((nq, width)), per_batch((width, page)), per_batch((width, page)),
                  per_batch((n_heads, page)), pl.BlockSpec(tri.shape, lambda b, s, pt: (0, 0)),
                  in_hbm, in_hbm, in_hbm],
        out_specs=per_batch((nq, width)),
        scratch_shapes=[pltpu.VMEM((rows, width), F32), pltpu.VMEM((rows, 1), F32),
                        pltpu.VMEM((rows, 1), F32), pltpu.VMEM((rows, width), F32),
                        pltpu.VMEM((n_heads, 1), F32),
                        pltpu.VMEM((PAGE_SLOTS, n_pages, width, page), F32),
                        pltpu.VMEM((PAGE_SLOTS, n_pages, width, page), F32),
                        pltpu.VMEM((PAGE_SLOTS, n_pages, n_heads, page), F32),
                        pltpu.SemaphoreType.DMA((3, PAGE_SLOTS))],
    )
    return pl.pallas_call(
        functools.partial(_paged_attn_kernel, n_pages=n_pages, n_heads=n_heads, head_dim=head_dim),
        grid_spec=grid_spec,
        out_shape=jax.ShapeDtypeStruct((batch, nq, width), BF16),
        compiler_params=_params("arbitrary", "arbitrary"),
        name="paged_attention",
    )(page_table, q, kt_new, vt_new, lf_new, tri, kt_pool, vt_pool, lf_pool)


def _c_proj_kernel(*refs, n_heads, key_dim, seq_len, tiles_per_seq, n_over):
    x_ref, g_ref, wqkv_ref, wz_ref, wba_ref, taps_ref, alog_ref, dtb_ref = refs[:8]
    over = list(refs[8:8 + n_over]) if n_over else None
    (q_ref, k_ref, v_ref, z_ref, beta_ref, gdec_ref, u_ref, ext_ref) = refs[8 + n_over:8 + n_over + 8]
    y_ref = None if over else refs[-1]
    qk_width = n_heads * key_dim
    conv_width = wqkv_ref.shape[1]

    h = _rms(x_ref[...], g_ref[...]).astype(BF16)
    tm = h.shape[0]
    first = pl.program_id(0) % tiles_per_seq == 0 if over is None else pl.program_id(0) == 0
    _conv_prologue(ext_ref, first, conv_width)

    cw = CONV_CHUNK if over is None else conv_width
    nxt = _dot(h, wqkv_ref[:, :cw])
    for c0 in range(0, conv_width, cw):
        cur = nxt
        if c0 + cw < conv_width:
            nxt = _dot(h, wqkv_ref[:, c0 + cw:c0 + 2 * cw])
        y = _silu(_causal_conv(cur, taps_ref, ext_ref, over, seq_len, y_ref, c0))
        if over is None:
            tail = cur[tm - SUBLANES:, :]
            u_ref[:, c0:c0 + cw] = tail
            _conv_carry(ext_ref, tail, c0)
        else:
            u_ref[...] = cur
        for b0 in range(0, cw, key_dim):
            col = c0 + b0
            blk = y[:, b0:b0 + key_dim]
            if col < 2 * qk_width:
                blk = blk * lax.rsqrt(jnp.sum(blk * blk, axis=-1, keepdims=True) + NORM_EPS)
            if col < qk_width:
                q_ref[:, col:col + key_dim] = (blk * key_dim ** -0.5).astype(BF16)
            elif col < 2 * qk_width:
                k_ref[:, col - qk_width:col - qk_width + key_dim] = blk.astype(BF16)
            else:
                v_ref[:, col - 2 * qk_width:col - 2 * qk_width + key_dim] = blk.astype(BF16)

    z_ref[...] = _dot(h, wz_ref[...]).astype(BF16)
    ba = _dot(h, wba_ref[...])
    beta_ref[...] = jax.nn.sigmoid(ba)[:, :n_heads]
    gdec = -jnp.exp(alog_ref[...]) * _softplus(ba + dtb_ref[...])
    gdec_ref[...] = gdec[:, n_heads:2 * n_heads]


def _c_proj(x, gain, wqkv, wz, wba, taps, alog, dtb, overrides, *, n_heads, key_dim, seq_len, tm):
    t, d = x.shape
    conv_width = wqkv.shape[1]
    qk_width = n_heads * key_dim
    v_width = conv_width - 2 * qk_width
    n_tiles = t // tm
    n_over = 0 if overrides is None else len(overrides)
    row = lambda c: pl.BlockSpec((tm, c), lambda i: (i, 0))
    in_specs = [row(d)] + [_const_spec(a) for a in (gain, wqkv, wz, wba, taps, alog, dtb)]
    in_specs += [row(conv_width)] * n_over
    if overrides is None:
        u_spec = pl.BlockSpec((None, SUBLANES, conv_width), lambda i: (i, 0, 0))
        u_shape = jax.ShapeDtypeStruct((n_tiles, SUBLANES, conv_width), F32)
    else:
        u_spec = row(conv_width)
        u_shape = jax.ShapeDtypeStruct((t, conv_width), F32)
    out_specs = [row(qk_width), row(qk_width), row(v_width), row(v_width), row(n_heads), row(n_heads), u_spec]
    out_shape = [jax.ShapeDtypeStruct((t, qk_width), BF16), jax.ShapeDtypeStruct((t, qk_width), BF16),
                 jax.ShapeDtypeStruct((t, v_width), BF16), jax.ShapeDtypeStruct((t, v_width), BF16),
                 jax.ShapeDtypeStruct((t, n_heads), F32), jax.ShapeDtypeStruct((t, n_heads), F32), u_shape]
    return pl.pallas_call(
        functools.partial(_c_proj_kernel, n_heads=n_heads, key_dim=key_dim, seq_len=seq_len,
                          tiles_per_seq=max(seq_len // tm, 1), n_over=n_over),
        grid=(n_tiles,),
        in_specs=in_specs,
        out_specs=out_specs,
        out_shape=out_shape,
        scratch_shapes=_conv_scratch(tm, conv_width, overrides is None),
        compiler_params=_params("arbitrary"),
        name="c_proj",
    )(x, gain, wqkv, wz, wba, taps, alog, dtb, *(overrides or []))


def _delta_kernel(q_ref, k_ref, v_ref, z_ref, gcol_ref, beta_ref, grow_ref, s0_ref, nw_ref,
                  o_ref, s_out_ref, m_ref, *, n_heads, chunk):
    step = pl.program_id(1)
    n_batch = s0_ref.shape[0]
    rows = q_ref.shape[0] // n_batch
    dk = q_ref.shape[1] // n_heads
    dv = v_ref.shape[1] // n_heads
    n_chunks = rows // chunk
    seqs = [(b, h) for b in range(n_batch) for h in range(n_heads)]
    ri = lax.broadcasted_iota(jnp.int32, (chunk, chunk), 0)
    ci = lax.broadcasted_iota(jnp.int32, (chunk, chunk), 1)
    incl = (ci <= ri)[None]
    strict = (ci < ri)[None]

    @pl.when(step == 0)
    def _():
        for i, (b, h) in enumerate(seqs):
            m_ref[i] = s0_ref[b, h].T

    probs = [(c, b, h) for c in range(n_chunks) for b, h in seqs]
    rs = lambda c, b: slice(b * rows + c * chunk, b * rows + (c + 1) * chunk)
    stack = lambda f: jnp.stack([f(c, b, h) for c, b, h in probs])
    kb = stack(lambda c, b, h: k_ref[rs(c, b), h * dk:(h + 1) * dk])
    qb = stack(lambda c, b, h: q_ref[rs(c, b), h * dk:(h + 1) * dk])
    vf = stack(lambda c, b, h: v_ref[rs(c, b), h * dv:(h + 1) * dv]).astype(F32)
    gc = stack(lambda c, b, h: gcol_ref[rs(c, b), h:h + 1])
    gr = stack(lambda c, b, h: grow_ref[b, h:h + 1, rs(c, 0)])
    bc = stack(lambda c, b, h: beta_ref[rs(c, b), h:h + 1])
    kf = kb.astype(F32)
    eg = jnp.exp(gc)
    decay = jnp.where(incl, jnp.exp(jnp.where(incl, gc - gr, 0.0)), 0.0)
    kqk = _bdot_nt(jnp.concatenate([kb, qb], axis=1), kb)
    a = jnp.where(strict, bc * kqk[:, :chunk] * decay, 0.0)
    nil = -a
    pw = a
    span = 2
    while span < chunk:
        pwb = pw.astype(BF16)
        pw = _bdot(pwb, pwb)
        nil = nil + pw + _bdot(nil.astype(BF16), pw.astype(BF16))
        span *= 2
    rhs = jnp.concatenate([bc * vf, (bc * eg) * kf], axis=2)
    w = rhs + _bdot(nil.astype(BF16), rhs.astype(BF16))
    w_v = w[:, :, :dv]
    p = (kqk[:, chunk:] * decay).astype(BF16)
    g_last = gc[:, chunk - 1:chunk, :]
    lhs = jnp.concatenate([w[:, :, dv:], eg * qb.astype(F32)], axis=1).astype(BF16)
    k_dec = (jnp.exp(g_last - gc) * kf).astype(BF16)
    g_chunk = jnp.exp(g_last)

    for c in range(n_chunks):
        hs = slice(c * len(seqs), (c + 1) * len(seqs))
        m = m_ref[...]
        r = _bdot(lhs[hs], m.astype(BF16))
        ub = (w_v[hs] - r[:, :chunk]).astype(BF16)
        o = r[:, chunk:] + _bdot(p[hs], ub)
        m_ref[...] = g_chunk[hs] * m + _bdot_tn(k_dec[hs], ub)
        on = o * lax.rsqrt(jnp.mean(o * o, axis=-1, keepdims=True) + NORM_EPS) * nw_ref[...]
        for i, (b, h) in enumerate(seqs):
            vl = slice(h * dv, (h + 1) * dv)
            o_ref[rs(c, b), vl] = (on[i] * _silu(z_ref[rs(c, b), vl].astype(F32))).astype(BF16)

    @pl.when(step == pl.num_programs(1) - 1)
    def _():
        for i, (b, h) in enumerate(seqs):
            s_out_ref[b, h] = m_ref[i].T


def _delta_rule(q, k, v, z, gcol, beta, grow, s0, norm_w, *, batch, seq_len, n_heads, chunk, rows, n_batch):
    t = q.shape[0]
    steps = seq_len // rows
    dk = q.shape[1] // n_heads
    dv = v.shape[1] // n_heads
    assert n_batch == 1 or steps == 1
    row = lambda c: pl.BlockSpec((n_batch * rows, c), lambda b, s: (b * steps + s, 0))
    state = pl.BlockSpec((n_batch, n_heads, dv, dk), lambda b, s: (b, 0, 0, 0))
    return pl.pallas_call(
        functools.partial(_delta_kernel, n_heads=n_heads, chunk=chunk),
        grid=(batch // n_batch, steps),
        in_specs=[row(q.shape[1]), row(k.shape[1]), row(v.shape[1]), row(z.shape[1]),
                  row(n_heads), row(n_heads),
                  pl.BlockSpec((n_batch, n_heads, rows), lambda b, s: (b, 0, s)),
                  state, pl.BlockSpec(norm_w.shape, lambda b, s: (0, 0))],
        out_specs=[row(v.shape[1]), state],
        out_shape=[jax.ShapeDtypeStruct((t, v.shape[1]), BF16),
                   jax.ShapeDtypeStruct((batch, n_heads, dv, dk), F32)],
        scratch_shapes=[pltpu.VMEM((n_batch * n_heads, dk, dv), F32)],
        compiler_params=_params("arbitrary", "arbitrary"),
        name="gated_delta_rule",
    )(q, k, v, z, gcol, beta, grow, s0, norm_w)


def _pad_lanes(a):
    return jnp.pad(a, ((0, 0), (0, LANES - a.shape[1])))


def _rows_to_heads(a, batch, seq_len):
    return a.reshape(batch, seq_len, a.shape[1]).transpose(0, 2, 1)


def _heads_to_rows(a):
    b, h, length = a.shape
    return a.transpose(0, 2, 1).reshape(b * length, h)


def _feature_major_pool(pool):
    n_l, n_p, page = pool.shape[:3]
    perm = (0, 1) + tuple(range(3, pool.ndim)) + (2,)
    return pool.transpose(perm).reshape(n_l * n_p, -1, page)


def _mixer_ab(x, past, w, i, gain, *, batch, seq_len, tm):
    t = x.shape[0]
    hd = w["a_head_dim"]
    nh = w["ab_b_f"].shape[1]
    aw = nh * hd
    w_in = w["ab_w_in"][i]
    bw = (w_in.shape[1] - 3 * aw - nh) // 3
    wb = w_in[:, 3 * aw + nh:].astype(BF16)
    wf = w_in[:, 3 * aw:3 * aw + nh]
    taps = w["ab_conv_w"][i]
    n_keep = taps.shape[0] - 1
    if past is None:
        wq = w_in[:, :aw].astype(BF16)
        wkvt = w_in[:, aw:3 * aw].T.astype(BF16)
        kt, vt, ktb, vtb, lft, qb, ob, tails = _ab_proj_prompt(
            x, gain, wq, wkvt, wf.T.astype(BF16), w["ab_b_f"][i][:, None], wb, taps,
            batch=batch, seq_len=seq_len, head_dim=hd, tm=tm)
        o_a = _fox_attention(qb, ktb, vtb, _forget_bias_rows(lft, hd), head_dim=hd,
                             tile=min(1024, seq_len), kblock=min(512, seq_len))
        tails = tails.reshape(batch, seq_len // tm, SUBLANES, bw)[:, -1]
        new_buf = tails[:, SUBLANES - n_keep:, :]
        to_tokens = lambda a: a.reshape(batch, nh, hd, seq_len).transpose(0, 3, 1, 2)
        state = (to_tokens(kt), to_tokens(vt), lft.transpose(0, 2, 1), new_buf)
    else:
        overrides = _conv_overrides(past["state_b_conv"][i], seq_len)
        q_s, k32, v32, logf, ob, gcu = _ab_proj_sample(
            x, gain, w_in[:, :3 * aw].astype(BF16), _pad_lanes(wf).astype(BF16),
            _pad_lanes(w["ab_b_f"][i][None, :]), wb, taps, overrides,
            n_heads=nh, head_dim=hd, seq_len=seq_len)
        n_phys, page = past["cache_a_k"].shape[1:3]
        pad_keys = lambda a: jnp.pad(a.reshape(batch, seq_len, -1).transpose(0, 2, 1),
                                     ((0, 0), (0, 0), (0, page - seq_len)))
        o_a = _paged_attention(
            q_s.reshape(batch, seq_len, aw), pad_keys(k32), pad_keys(v32), pad_keys(logf),
            _feature_major_pool(past["cache_a_k"]), _feature_major_pool(past["cache_a_v"]),
            _feature_major_pool(past["cache_a_logf"]), past["page_table"] + i * n_phys, head_dim=hd)
        o_a = o_a.reshape(t, aw)
        new_buf = gcu.reshape(batch, seq_len, bw)[:, seq_len - n_keep:, :]
        state = (k32.reshape(batch, seq_len, nh, hd), v32.reshape(batch, seq_len, nh, hd),
                 logf.reshape(batch, seq_len, nh), new_buf)
    return [o_a, ob], w["ab_w_out"][i].astype(BF16), state


def _mixer_c(x, past, w, i, gain, *, batch, seq_len, tm):
    t = x.shape[0]
    nh = w["c_a_log"].shape[1]
    dv = w["c_norm_w"].shape[1]
    w_in = w["c_w_in"][i]
    v_width = nh * dv
    conv_width = w_in.shape[1] - v_width - 2 * nh
    dk = (conv_width - v_width) // 2 // nh
    wqkv = w_in[:, :conv_width].astype(BF16)
    wz = w_in[:, conv_width:conv_width + v_width].astype(BF16)
    wba = _pad_lanes(w_in[:, conv_width + v_width:]).astype(BF16)
    zeros_h = jnp.zeros((1, nh), F32)
    alog = _pad_lanes(jnp.concatenate([zeros_h, w["c_a_log"][i][None, :]], axis=1))
    dtb = _pad_lanes(jnp.concatenate([zeros_h, w["c_dt_bias"][i][None, :]], axis=1))
    taps = w["c_conv_w"][i]
    n_keep = taps.shape[0] - 1
    tm_c = min(tm, 512)
    if past is None:
        overrides = None
        s0 = jnp.zeros((batch, nh, dv, dk), F32)
    else:
        overrides = _conv_overrides(past["state_c_conv"][i], seq_len)
        s0 = past["state_c_S"][i]
    qn, kn, vn, zb, beta, gdec, u_out = _c_proj(
        x, gain, wqkv, wz, wba, taps, alog, dtb, overrides,
        n_heads=nh, key_dim=dk, seq_len=seq_len, tm=tm_c)
    if past is None:
        tails = u_out.reshape(batch, seq_len // tm_c, SUBLANES, conv_width)[:, -1]
        new_buf = tails[:, SUBLANES - n_keep:, :]
        len_pad = seq_len
    else:
        new_buf = u_out.reshape(batch, seq_len, conv_width)[:, seq_len - n_keep:, :]
        len_pad = -(-seq_len // DELTA_CHUNK) * DELTA_CHUNK
        pad3 = lambda a: jnp.pad(a.reshape(batch, seq_len, a.shape[1]),
                                 ((0, 0), (0, len_pad - seq_len), (0, 0))).reshape(batch * len_pad, a.shape[1])
        qn, kn, vn, zb, beta, gdec = [pad3(a) for a in (qn, kn, vn, zb, beta, gdec)]
    n_batch = DELTA_BATCH if len_pad <= DELTA_ROWS and batch % DELTA_BATCH == 0 else 1
    scan_len = -(-len_pad // LANES) * LANES
    g_heads = jnp.pad(_rows_to_heads(gdec, batch, len_pad), ((0, 0), (0, 0), (0, scan_len - len_pad)))
    grow = _segment_cumsum(g_heads, DELTA_CHUNK)[:, :, :len_pad]
    og, s_new = _delta_rule(qn, kn, vn, zb, _heads_to_rows(grow), beta, grow, s0, w["c_norm_w"][i][None, :],
                            batch=batch, seq_len=len_pad, n_heads=nh, chunk=DELTA_CHUNK,
                            rows=min(DELTA_ROWS, len_pad), n_batch=n_batch)
    if len_pad != seq_len:
        og = og.reshape(batch, len_pad, v_width)[:, :seq_len].reshape(t, v_width)
    return [og], w["c_w_out"][i].astype(BF16), (new_buf, s_new)


def _run_trunk(x3, past, w, *, tm):
    batch, seq_len, d = x3.shape
    x = x3.reshape(batch * seq_len, d)
    ab_states, c_states = [], []
    ffn = lambda x, g0, g1, layer, j, mixer=None: _ffn_block(
        x, g0, g1, w["ffn_w_gate"], w["ffn_w_up"], w["ffn_w_down"], layer, j, tm=tm, mixer=mixer)
    for layer in range(w["norm_g"].shape[0]):
        gains = [w["norm_g"][layer, j][None, :] for j in range(6)]
        x = ffn(x, gains[0], gains[1], layer, 0)
        mixer, states = (_mixer_ab, ab_states) if layer % 2 == 0 else (_mixer_c, c_states)
        parts, w_out, state = mixer(x, past, w, layer // 2, gains[2], batch=batch, seq_len=seq_len, tm=tm)
        states.append(state)
        x = ffn(x, gains[4], gains[5], layer, 1, mixer=(parts, w_out, gains[3]))
    ab_new = [jnp.stack(ts) for ts in zip(*ab_states)]
    c_new = [jnp.stack(ts) for ts in zip(*c_states)]
    return x.reshape(batch, seq_len, d), ab_new, c_new


def kernel(x_prompt, x_sample, cache_a_k, cache_a_v, cache_a_logf, page_table, state_b_conv, state_c_conv, state_c_S, norm_g, ffn_w_gate, ffn_w_up, ffn_w_down, ab_w_in, ab_b_f, ab_conv_w, ab_w_out, c_w_in, c_conv_w, c_a_log, c_dt_bias, c_norm_w, c_w_out):
    w = dict(norm_g=norm_g,
             ffn_w_gate=ffn_w_gate, ffn_w_up=ffn_w_up, ffn_w_down=ffn_w_down,
             ab_w_in=ab_w_in, ab_b_f=ab_b_f, ab_conv_w=ab_conv_w, ab_w_out=ab_w_out,
             c_w_in=c_w_in, c_conv_w=c_conv_w, c_a_log=c_a_log, c_dt_bias=c_dt_bias,
             c_norm_w=c_norm_w, c_w_out=c_w_out, a_head_dim=cache_a_k.shape[-1])
    past = dict(cache_a_k=cache_a_k, cache_a_v=cache_a_v, cache_a_logf=cache_a_logf, page_table=page_table,
                state_b_conv=state_b_conv, state_c_conv=state_c_conv, state_c_S=state_c_S)
    prompt_rows = x_prompt.shape[0] * x_prompt.shape[1]
    sample_rows = x_sample.shape[0] * x_sample.shape[1]
    y_prompt, ab_p, c_p = _run_trunk(x_prompt, None, w, tm=min(512, prompt_rows))
    y_sample, ab_s, c_s = _run_trunk(x_sample, past, w, tm=min(512, sample_rows))
    a_k_prompt, a_v_prompt, a_logf_prompt, b_conv_prompt = ab_p
    a_k_sample, a_v_sample, a_logf_sample, b_conv_sample = ab_s
    c_conv_prompt, c_S_prompt = c_p
    c_conv_sample, c_S_sample = c_s
    return (y_prompt, y_sample,
            a_k_prompt, a_v_prompt, a_logf_prompt,
            a_k_sample, a_v_sample, a_logf_sample,
            b_conv_prompt, b_conv_sample,
            c_conv_prompt, c_conv_sample,
            c_S_prompt, c_S_sample)
```

```python
import functools

import jax
import jax.numpy as jnp
from jax import lax
from jax.experimental import pallas as pl
from jax.experimental.pallas import tpu as pltpu

F32 = jnp.float32
BF16 = jnp.bfloat16
NORM_EPS = 1e-6
LOG2_E = 1.4426950408889634

LANES = 128
SUBLANES = 8
VMEM_LIMIT_BYTES = 56 * 1024 * 1024
DELTA_CHUNK = 64
DELTA_ROWS = 512
DELTA_BATCH = 4
CONV_ROW_STRIDE = 4
CONV_CHUNK = 512
PAGES_PER_STEP = 16
PAGE_GROUPS = 2
PAGE_SLOTS = 3


def _params(*semantics):
    return pltpu.CompilerParams(dimension_semantics=semantics, vmem_limit_bytes=VMEM_LIMIT_BYTES)


def _resident(shape, index_map):
    return pl.BlockSpec(shape, index_map, pipeline_mode=pl.Buffered(1))


def _const_spec(a):
    return _resident(a.shape, lambda *_: (0,) * a.ndim)


def _rms(x, g):
    return x * lax.rsqrt(jnp.mean(x * x, axis=-1, keepdims=True) + NORM_EPS) * g


def _softplus(x):
    return jnp.maximum(x, 0.0) + jnp.log1p(jnp.exp(-jnp.abs(x)))


def _silu(x):
    return x * jax.nn.sigmoid(x)


def _dot(a, b):
    return jnp.dot(a, b, preferred_element_type=F32)


def _dot_nt(a, b):
    return lax.dot_general(a, b, (((1,), (1,)), ((), ())), preferred_element_type=F32)


def _dot_tn(a, b):
    return lax.dot_general(a, b, (((0,), (0,)), ((), ())), preferred_element_type=F32)


def _bdot(a, b):
    return lax.dot_general(a, b, (((2,), (1,)), ((0,), (0,))), preferred_element_type=F32)


def _bdot_nt(a, b):
    return lax.dot_general(a, b, (((2,), (2,)), ((0,), (0,))), preferred_element_type=F32)


def _bdot_tn(a, b):
    return lax.dot_general(a, b, (((1,), (1,)), ((0,), (0,))), preferred_element_type=F32)


def _mix_rows(part_refs, w_ref):
    mix = None
    k0 = 0
    for a_ref in part_refs:
        kw = a_ref.shape[1]
        part = _dot(a_ref[...], w_ref[k0:k0 + kw, :])
        mix = part if mix is None else mix + part
        k0 += kw
    return mix


def _ffn_kernel(*refs, n_parts, ff_chunk):
    part_refs = refs[:n_parts]
    if n_parts:
        w_mix_ref, g_mix_ref = refs[n_parts:n_parts + 2]
        refs = refs[n_parts + 2:]
    x_ref, gpre_ref, gpost_ref, wg_ref, wu_ref, wd_ref, o_ref = refs
    x = x_ref[...]
    if n_parts:
        x = x + _rms(_mix_rows(part_refs, w_mix_ref), g_mix_ref[...])
    h = _rms(x, gpre_ref[...]).astype(BF16)
    d_ff = wg_ref.shape[1]
    proj = lambda c0: (_dot(h, wg_ref[:, c0:c0 + ff_chunk].astype(BF16)),
                       _dot(h, wu_ref[:, c0:c0 + ff_chunk].astype(BF16)))
    y = None
    nxt = proj(0)
    for c0 in range(0, d_ff, ff_chunk):
        gate, up = nxt
        if c0 + ff_chunk < d_ff:
            nxt = proj(c0 + ff_chunk)
        act = (_silu(gate) * up).astype(BF16)
        part = _dot(act, wd_ref[c0:c0 + ff_chunk, :].astype(BF16))
        y = part if y is None else y + part
    o_ref[...] = x + 0.5 * _rms(y, gpost_ref[...])


def _ffn_chunk(d_ff):
    for c in range(4 * LANES, 0, -LANES):
        if d_ff % c == 0:
            return c
    return d_ff


def _ffn_block(x, g_pre, g_post, wg_all, wu_all, wd_all, layer, j, *, tm, mixer=None):
    t, d = x.shape
    d_ff = wg_all.shape[-1]
    row = lambda c: pl.BlockSpec((tm, c), lambda i: (i, 0))
    gain = _resident((1, d), lambda i: (0, 0))
    w_in = _resident((None, None, d, d_ff), lambda i: (layer, j, 0, 0))
    w_out = _resident((None, None, d_ff, d), lambda i: (layer, j, 0, 0))
    parts, w_mix, g_mix = mixer if mixer is not None else ([], None, None)
    mix_args = list(parts) + ([w_mix, g_mix] if parts else [])
    mix_specs = [row(p.shape[1]) for p in parts] + ([_const_spec(w_mix), gain] if parts else [])
    return pl.pallas_call(
        functools.partial(_ffn_kernel, n_parts=len(parts), ff_chunk=_ffn_chunk(d_ff)),
        grid=(t // tm,),
        in_specs=mix_specs + [row(d), gain, gain, w_in, w_in, w_out],
        out_specs=row(d),
        out_shape=jax.ShapeDtypeStruct((t, d), F32),
        compiler_params=_params("arbitrary"),
        name="ffn_block",
    )(*mix_args, x, g_pre, g_post, wg_all, wu_all, wd_all)


def _causal_conv(u, taps_ref, ext_ref, overrides, seq_len, y_ref=None, col0=0):
    tm = u.shape[0]
    n_taps = taps_ref.shape[0]
    if overrides is None:
        stride = CONV_ROW_STRIDE
        group = SUBLANES * stride
        assert n_taps <= stride + 1 <= SUBLANES + 1
        blocks = range(col0 // LANES, (col0 + u.shape[1]) // LANES)
        for lb in blocks:
            lanes = slice(lb * LANES, (lb + 1) * LANES)
            ext_ref[lb, pl.ds(SUBLANES, tm), :] = u[:, lb * LANES - col0:(lb + 1) * LANES - col0]
            taps = [taps_ref[n_taps - 1 - j:n_taps - j, lanes] for j in range(n_taps)]
            for g0 in range(0, tm, group):
                win = {d: ext_ref[lb, pl.ds(SUBLANES + g0 + d, SUBLANES, stride=stride), :]
                       for d in range(1 - n_taps, stride)}
                for i in range(stride):
                    acc = win[i] * taps[0]
                    for j in range(1, n_taps):
                        acc = acc + win[i - j] * taps[j]
                    y_ref[lb, pl.ds(g0 + i, SUBLANES, stride=stride), :] = acc
        return jnp.concatenate([y_ref[lb] for lb in blocks], axis=1)
    ext_ref[pl.ds(SUBLANES, tm), :] = u
    out = u * taps_ref[n_taps - 1:n_taps, :]
    pos = lax.broadcasted_iota(jnp.int32, u.shape, 0) % seq_len
    for j in range(1, n_taps):
        prev = jnp.where(pos >= j, ext_ref[pl.ds(SUBLANES - j, tm), :], overrides[j - 1][...])
        out = out + prev * taps_ref[n_taps - 1 - j:n_taps - j, :]
    return out


def _conv_overrides(state, seq_len):
    b, wm1, c = state.shape
    assert seq_len >= wm1
    outs = []
    for j in range(1, wm1 + 1):
        rows = jnp.pad(state[:, wm1 - j:, :], ((0, 0), (0, seq_len - j), (0, 0)))
        outs.append(rows.reshape(b * seq_len, c))
    return outs


def _conv_carry(ext_ref, tail, col0=0):
    if len(ext_ref.shape) == 2:
        ext_ref[pl.ds(0, SUBLANES), :] = tail
    else:
        for lb in range(tail.shape[1] // LANES):
            ext_ref[col0 // LANES + lb, pl.ds(0, SUBLANES), :] = tail[:, lb * LANES:(lb + 1) * LANES]


def _conv_prologue(ext_ref, first, width):
    @pl.when(first)
    def _():
        _conv_carry(ext_ref, jnp.zeros((SUBLANES, width), F32))


def _conv_scratch(tm, width, blocked):
    if not blocked:
        return [pltpu.VMEM((SUBLANES + tm, width), F32)]
    assert tm % (SUBLANES * CONV_ROW_STRIDE) == 0 and width % LANES == 0
    return [pltpu.VMEM((width // LANES, SUBLANES + tm, LANES), F32), pltpu.VMEM((width // LANES, tm, LANES), F32)]


def _log_sigmoid(z):
    return -_softplus(-z)


def _ab_proj_prompt_kernel(x_ref, g_ref, wq_ref, wkvt_ref, wft_ref, bf_ref, wb_ref, taps_ref,
                           kt_ref, vt_ref, ktb_ref, vtb_ref, lft_ref, qb_ref, ob_ref, tail_ref, ext_ref, y_ref,
                           *, head_dim, tiles_per_seq):
    a_width = wq_ref.shape[1]
    b_width = wb_ref.shape[1] // 3
    h = _rms(x_ref[...], g_ref[...]).astype(BF16)
    qb_ref[...] = (_dot(h, wq_ref[...]) * (head_dim ** -0.5 * LOG2_E)).astype(BF16)
    kvt = _dot_nt(wkvt_ref[...], h)
    kt = kvt[:a_width]
    vt = kvt[a_width:]
    kt_ref[...] = kt
    vt_ref[...] = vt
    ktb_ref[...] = kt.astype(BF16)
    vtb_ref[...] = vt.astype(BF16)
    lft_ref[...] = _log_sigmoid(_dot_nt(wft_ref[...], h) + bf_ref[...])

    gbu = _dot(h, wb_ref[...])
    gate_b = gbu[:, :b_width]
    gcu = gbu[:, b_width:2 * b_width] * gbu[:, 2 * b_width:]
    _conv_prologue(ext_ref, pl.program_id(0) % tiles_per_seq == 0, b_width)
    y = _causal_conv(gcu, taps_ref, ext_ref, None, None, y_ref)
    ob_ref[...] = (gate_b * y).astype(BF16)
    tail = gcu[gcu.shape[0] - SUBLANES:, :]
    tail_ref[...] = tail
    _conv_carry(ext_ref, tail)


def _ab_proj_prompt(x, gain, wq, wkvt, wft, bf_col, wb, taps, *, batch, seq_len, head_dim, tm):
    t, d = x.shape
    a_width = wq.shape[1]
    n_heads = wft.shape[0]
    b_width = wb.shape[1] // 3
    tiles_per_seq = seq_len // tm
    row = lambda c: pl.BlockSpec((tm, c), lambda i: (i, 0))
    feat = lambda c: pl.BlockSpec((None, c, tm), lambda i: (i // tiles_per_seq, 0, i % tiles_per_seq))
    feat_shape = lambda c, dt: jax.ShapeDtypeStruct((batch, c, seq_len), dt)
    return pl.pallas_call(
        functools.partial(_ab_proj_prompt_kernel, head_dim=head_dim, tiles_per_seq=tiles_per_seq),
        grid=(t // tm,),
        in_specs=[row(d)] + [_const_spec(a) for a in (gain, wq, wkvt, wft, bf_col, wb, taps)],
        out_specs=[feat(a_width), feat(a_width), feat(a_width), feat(a_width), feat(n_heads),
                   row(a_width), row(b_width),
                   pl.BlockSpec((None, SUBLANES, b_width), lambda i: (i, 0, 0))],
        out_shape=[feat_shape(a_width, F32), feat_shape(a_width, F32), feat_shape(a_width, BF16),
                   feat_shape(a_width, BF16), feat_shape(n_heads, F32),
                   jax.ShapeDtypeStruct((t, a_width), BF16), jax.ShapeDtypeStruct((t, b_width), BF16),
                   jax.ShapeDtypeStruct((t // tm, SUBLANES, b_width), F32)],
        scratch_shapes=_conv_scratch(tm, b_width, True),
        compiler_params=_params("arbitrary"),
        name="ab_proj_prompt",
    )(x, gain, wq, wkvt, wft, bf_col, wb, taps)


def _ab_proj_sample_kernel(*refs, n_heads, head_dim, seq_len, n_over):
    x_ref, g_ref, wqkv_ref, wf_ref, bf_ref, wb_ref, taps_ref = refs[:7]
    over = list(refs[7:7 + n_over])
    q_ref, k_ref, v_ref, logf_ref, ob_ref, u_ref, ext_ref = refs[7 + n_over:]
    a_width = n_heads * head_dim
    b_width = wb_ref.shape[1] // 3
    h = _rms(x_ref[...], g_ref[...]).astype(BF16)
    qkv = _dot(h, wqkv_ref[...])
    q_ref[...] = qkv[:, :a_width] * head_dim ** -0.5
    k_ref[...] = qkv[:, a_width:2 * a_width]
    v_ref[...] = qkv[:, 2 * a_width:]
    logf_ref[...] = _log_sigmoid(_dot(h, wf_ref[...]) + bf_ref[...])[:, :n_heads]

    gbu = _dot(h, wb_ref[...])
    gate_b = gbu[:, :b_width]
    gcu = gbu[:, b_width:2 * b_width] * gbu[:, 2 * b_width:]
    _conv_prologue(ext_ref, pl.program_id(0) == 0, b_width)
    y = _causal_conv(gcu, taps_ref, ext_ref, over, seq_len)
    ob_ref[...] = (gate_b * y).astype(BF16)
    u_ref[...] = gcu


def _ab_proj_sample(x, gain, wqkv, wf, bf, wb, taps, overrides, *, n_heads, head_dim, seq_len):
    t, d = x.shape
    a_width = n_heads * head_dim
    b_width = wb.shape[1] // 3
    full = lambda c: pl.BlockSpec((t, c), lambda i: (0, 0))
    f32 = lambda c: jax.ShapeDtypeStruct((t, c), F32)
    return pl.pallas_call(
        functools.partial(_ab_proj_sample_kernel, n_heads=n_heads, head_dim=head_dim, seq_len=seq_len,
                          n_over=len(overrides)),
        grid=(1,),
        in_specs=[full(d)] + [_const_spec(a) for a in (gain, wqkv, wf, bf, wb, taps)]
        + [full(b_width)] * len(overrides),
        out_specs=[full(a_width), full(a_width), full(a_width), full(n_heads), full(b_width), full(b_width)],
        out_shape=[f32(a_width), f32(a_width), f32(a_width), f32(n_heads),
                   jax.ShapeDtypeStruct((t, b_width), BF16), f32(b_width)],
        scratch_shapes=[pltpu.VMEM((SUBLANES + t, b_width), F32)],
        compiler_params=_params("arbitrary"),
        name="ab_proj_sample",
    )(x, gain, wqkv, wf, bf, wb, taps, *overrides)


def _scan_kernel(x_ref, o_ref, *, seg):
    x = x_ref[...]
    pos = lax.broadcasted_iota(jnp.int32, x.shape, 1) % seg
    d = 1
    while d < seg:
        x = x + jnp.where(pos >= d, pltpu.roll(x, d, axis=1), 0.0)
        d *= 2
    o_ref[...] = x


def _segment_cumsum(x, seg):
    b, h, length = x.shape
    spec = pl.BlockSpec((b * h, length), lambda i: (0, 0))
    return pl.pallas_call(
        functools.partial(_scan_kernel, seg=seg),
        grid=(1,),
        in_specs=[spec],
        out_specs=spec,
        out_shape=jax.ShapeDtypeStruct((b * h, length), F32),
        compiler_params=_params("arbitrary"),
        name="segment_cumsum",
    )(x.reshape(b * h, length)).reshape(x.shape)


def _split3(x):
    hi = x.astype(BF16)
    r = x - hi.astype(F32)
    mid = r.astype(BF16)
    lo = (r - mid.astype(F32)).astype(BF16)
    return hi, mid, lo


def _forget_bias_kernel(lf_ref, o_ref, *, head_dim):
    x = lf_ref[...]
    n_heads, length = x.shape
    pos = lax.broadcasted_iota(jnp.int32, x.shape, 1)
    d = 1
    while d < length:
        x = x + jnp.where(pos >= d, pltpu.roll(x, d, axis=1), 0.0)
        d *= 2
    pieces = [p.astype(F32) for p in _split3(-LOG2_E * x)]
    pad = jnp.zeros((head_dim - len(pieces), length), F32)
    for h in range(n_heads):
        blk = jnp.concatenate([p[h:h + 1] for p in pieces] + [pad], axis=0)
        o_ref[pl.ds(h * head_dim, head_dim), :] = blk.astype(BF16)


def _forget_bias_rows(lft, head_dim):
    batch, n_heads, length = lft.shape
    return pl.pallas_call(
        functools.partial(_forget_bias_kernel, head_dim=head_dim),
        grid=(batch,),
        in_specs=[pl.BlockSpec((None, n_heads, length), lambda b: (b, 0, 0))],
        out_specs=pl.BlockSpec((None, n_heads * head_dim, length), lambda b: (b, 0, 0)),
        out_shape=jax.ShapeDtypeStruct((batch, n_heads * head_dim, length), BF16),
        compiler_params=_params("arbitrary"),
        name="forget_bias_rows",
    )(lft)


def _fox_kernel(q_ref, kt_ref, vt_ref, cb_ref, o_ref, *, tile, kblock, head_dim):
    qi = pl.program_id(2)
    heads = q_ref.shape[1] // head_dim
    sub = tile // kblock
    row = lax.broadcasted_iota(jnp.int32, (kblock, kblock), 0)
    col = lax.broadcasted_iota(jnp.int32, (kblock, kblock), 1)
    head_of_lane = lax.broadcasted_iota(jnp.int32, q_ref.shape, 1) // head_dim
    q_pair = q_ref[...]
    ones_rows = jnp.ones((head_dim, kblock), BF16)
    owns = [head_of_lane == j for j in range(heads)]
    q_augs = [jnp.where(own, q_pair, jnp.ones_like(q_pair)) for own in owns]

    def blocks(kis, carry, q_rows=slice(None), triangular=False):
        offs = [pl.multiple_of(ki * kblock, kblock) for ki in kis]
        feats = [slice(j * head_dim, (j + 1) * head_dim) for j in range(heads)]
        order = lambda j, data, aux: [data, aux] if j == 0 else [aux, data]
        scores = [[_dot(q_augs[j][q_rows], jnp.concatenate(
            order(j, kt_ref[feats[j], pl.ds(off, kblock)], cb_ref[feats[j], pl.ds(off, kblock)]), axis=0))
            for j in range(heads)] for off in offs]
        carry = list(carry)
        for off, score in zip(offs, scores):
            for j, (m, acc) in enumerate(carry):
                s = jnp.where(col <= row, score[j], -jnp.inf) if triangular else score[j]
                m_new = jnp.maximum(m, jnp.max(s, axis=1, keepdims=True))
                p = jnp.exp2(s - m_new).astype(BF16)
                vt_aug = jnp.concatenate(order(j, vt_ref[feats[j], pl.ds(off, kblock)], ones_rows), axis=0)
                carry[j] = (m_new, jnp.exp2(m - m_new) * acc + _dot_nt(p, vt_aug))
        return tuple(carry)

    init = tuple((jnp.full((tile, 1), -jnp.inf, F32), jnp.zeros(q_ref.shape, F32)) for _ in range(heads))
    carry = lax.fori_loop(0, qi, lambda i, c: blocks([i * sub + r for r in range(sub)], c), init)
    pieces = []
    for i in range(sub):
        q_rows = slice(i * kblock, (i + 1) * kblock)
        piece = tuple((m[q_rows], acc[q_rows]) for m, acc in carry)
        for r in range(i + 1):
            piece = blocks([qi * sub + r], piece, q_rows, triangular=r == i)
        pieces.append(piece)
    carry = tuple((None, jnp.concatenate([piece[j][1] for piece in pieces], axis=0)) for j in range(heads))
    out = None
    for own, (_, acc) in zip(owns, carry):
        o = acc / pltpu.roll(acc, head_dim, axis=1)
        out = o if out is None else jnp.where(own, o, out)
    o_ref[...] = out.astype(BF16)


def _fox_attention(qb, ktb, vtb, cb, *, head_dim, tile, kblock):
    t, a_width = qb.shape
    batch, _, seq_len = ktb.shape
    nq = seq_len // tile
    q_spec = pl.BlockSpec((tile, LANES), lambda b, hp, qi: (b * nq + qi, hp))
    kv_spec = pl.BlockSpec((None, LANES, seq_len), lambda b, hp, qi: (b, hp, 0))
    return pl.pallas_call(
        functools.partial(_fox_kernel, tile=tile, kblock=kblock, head_dim=head_dim),
        grid=(batch, a_width // LANES, nq),
        in_specs=[q_spec, kv_spec, kv_spec, kv_spec],
        out_specs=q_spec,
        out_shape=jax.ShapeDtypeStruct((t, a_width), BF16),
        compiler_params=_params("arbitrary", "arbitrary", "arbitrary"),
        name="fox_attention",
    )(qb, ktb, vtb, cb)


def _paged_attn_kernel(pt_ref, q_ref, knew_ref, vnew_ref, lfnew_ref, tri_ref, kpool_ref, vpool_ref, lfpool_ref,
                       o_ref, qbd_ref, m_ref, l_ref, acc_ref, pref_ref, kbuf, vbuf, lfbuf, sems,
                       *, n_pages, n_heads, head_dim):
    step = pl.program_id(1)
    steps = pl.num_programs(1)
    lin = pl.program_id(0) * steps + step
    total = pl.num_programs(0) * steps
    n_slots = kbuf.shape[0]
    nq, width = q_ref.shape
    rows = nq * n_heads
    page = kbuf.shape[3]

    def gather(lin_step, slot, pages=None):
        out = []
        for j in range(n_pages):
            pg = 0 if pages is None else pages[j]
            out += [pltpu.make_async_copy(kpool_ref.at[pg], kbuf.at[slot, j], sems.at[0, slot]),
                    pltpu.make_async_copy(vpool_ref.at[pg], vbuf.at[slot, j], sems.at[1, slot]),
                    pltpu.make_async_copy(lfpool_ref.at[pg], lfbuf.at[slot, j], sems.at[2, slot])]
        return out

    def start_gather(lin_step, slot):
        bi = lin_step // steps
        p0 = (lin_step % steps) * n_pages
        for c in gather(lin_step, slot, [pt_ref[bi, p0 + j] for j in range(n_pages)]):
            c.start()

    @pl.when(lin == 0)
    def _():
        for d in range(n_slots - 1):
            start_gather(d, d)

    @pl.when(lin + (n_slots - 1) < total)
    def _():
        start_gather(lin + (n_slots - 1), (lin + (n_slots - 1)) % n_slots)

    slot = lin % n_slots
    for c in gather(lin, slot):
        c.wait()
    k_refs = [kbuf.at[slot, j] for j in range(n_pages)]
    v_refs = [vbuf.at[slot, j] for j in range(n_pages)]
    lf_refs = [lfbuf.at[slot, j] for j in range(n_pages)]
    head_of_lane = lax.broadcasted_iota(jnp.int32, (n_heads, width), 1) // head_dim
    own = head_of_lane == lax.broadcasted_iota(jnp.int32, (n_heads, width), 0)

    @pl.when(step == 0)
    def _():
        q = q_ref[...]
        for qq in range(nq):
            blk = jnp.where(own, jnp.broadcast_to(q[qq:qq + 1, :], (n_heads, width)), 0.0)
            qbd_ref[pl.ds(qq * n_heads, n_heads), :] = blk
        m_ref[...] = jnp.full(m_ref.shape, -jnp.inf, F32)
        l_ref[...] = jnp.zeros(l_ref.shape, F32)
        acc_ref[...] = jnp.zeros(acc_ref.shape, F32)
        pref_ref[...] = jnp.zeros(pref_ref.shape, F32)

    def attend(k_list, v_list, lf_list, causal):
        n = len(k_list)
        ys = [lf[...] for lf in lf_list]
        y = jnp.concatenate(ys, axis=0) if n > 1 else ys[0]
        pieces = jnp.concatenate(_split3(y), axis=0)
        cw = _dot(pieces, tri_ref[...])
        nh = n * n_heads
        cw = cw[:nh] + cw[nh:2 * nh] + cw[2 * nh:]
        qbd = qbd_ref[...].astype(BF16)
        pref = pref_ref[...]
        s_list = []
        for g in range(n):
            cg = cw[g * n_heads:(g + 1) * n_heads]
            c = cg + pref
            pref = pref + cg[:, page - 1:page]
            s = _dot(qbd, k_list[g][...].astype(BF16))
            s = s - jnp.concatenate([c] * nq, axis=0)
            if causal:
                key = lax.broadcasted_iota(jnp.int32, (rows, page), 1)
                qpos = lax.broadcasted_iota(jnp.int32, (rows, page), 0) // n_heads
                s = jnp.where(key <= qpos, s, -jnp.inf)
            s_list.append(s)
        pref_ref[...] = pref
        m, l, acc = m_ref[...], l_ref[...], acc_ref[...]
        group = max(n // PAGE_GROUPS, 1)
        for g0 in range(0, n, group):
            s_grp = jnp.concatenate(s_list[g0:g0 + group], axis=1) if group > 1 else s_list[g0]
            m_new = jnp.maximum(m, jnp.max(s_grp, axis=1, keepdims=True))
            alpha = jnp.exp(m - m_new)
            p = jnp.exp(s_grp - m_new)
            l = alpha * l + jnp.sum(p, axis=1, keepdims=True)
            pv = None
            for g in range(group):
                part = _dot_nt(p[:, g * page:(g + 1) * page].astype(BF16), v_list[g0 + g][...].astype(BF16))
                pv = part if pv is None else pv + part
            acc = alpha * acc + pv
            m = m_new
        m_ref[...], l_ref[...], acc_ref[...] = m, l, acc

    attend(k_refs, v_refs, lf_refs, causal=False)

    @pl.when(step == pl.num_programs(1) - 1)
    def _():
        attend([knew_ref], [vnew_ref], [lfnew_ref], causal=True)
        o = acc_ref[...] / l_ref[...]
        out_rows = []
        for qq in range(nq):
            blk = jnp.where(own, o[qq * n_heads:(qq + 1) * n_heads, :], 0.0)
            out_rows.append(jnp.sum(blk, axis=0, keepdims=True))
        o_ref[...] = jnp.concatenate(out_rows, axis=0).astype(o_ref.dtype)


def _paged_attention(q, kt_new, vt_new, lf_new, kt_pool, vt_pool, lf_pool, page_table, *, head_dim):
    batch, nq, width = q.shape
    page = kt_pool.shape[2]
    n_heads = lf_pool.shape[1]
    n_pages = PAGES_PER_STEP
    steps = page_table.shape[1] // n_pages
    rows = nq * n_heads
    tri = (jnp.arange(page)[:, None] <= jnp.arange(page)[None, :]).astype(BF16)

    assert batch * steps >= PAGE_SLOTS - 1
    per_batch = lambda shape: pl.BlockSpec((None,) + shape, lambda b, s, pt: (b, 0, 0))
    in_hbm = pl.BlockSpec(memory_space=pl.ANY)
    grid_spec = pltpu.PrefetchScalarGridSpec(
        num_scalar_prefetch=1,
        grid=(batch, steps),
        in_specs=[per_batch((nq, width)), per_batch((width, page)), per_batch((width, page)),
                  per_batch((n_heads, page)), pl.BlockSpec(tri.shape, lambda b, s, pt: (0, 0)),
                  in_hbm, in_hbm, in_hbm],
        out_specs=per_batch((nq, width)),
        scratch_shapes=[pltpu.VMEM((rows, width), F32), pltpu.VMEM((rows, 1), F32),
                        pltpu.VMEM((rows, 1), F32), pltpu.VMEM((rows, width), F32),
                        pltpu.VMEM((n_heads, 1), F32),
                        pltpu.VMEM((PAGE_SLOTS, n_pages, width, page), F32),
                        pltpu.VMEM((PAGE_SLOTS, n_pages, width, page), F32),
                        pltpu.VMEM((PAGE_SLOTS, n_pages, n_heads, page), F32),
                        pltpu.SemaphoreType.DMA((3, PAGE_SLOTS))],
    )
    return pl.pallas_call(
        functools.partial(_paged_attn_kernel, n_pages=n_pages, n_heads=n_heads, head_dim=head_dim),
        grid_spec=grid_spec,
        out_shape=jax.ShapeDtypeStruct((batch, nq, width), BF16),
        compiler_params=_params("arbitrary", "arbitrary"),
        name="paged_attention",
    )(page_table, q, kt_new, vt_new, lf_new, tri, kt_pool, vt_pool, lf_pool)


def _c_proj_kernel(*refs, n_heads, key_dim, seq_len, tiles_per_seq, n_over):
    x_ref, g_ref, wqkv_ref, wz_ref, wba_ref, taps_ref, alog_ref, dtb_ref = refs[:8]
    over = list(refs[8:8 + n_over]) if n_over else None
    (q_ref, k_ref, v_ref, z_ref, beta_ref, gdec_ref, u_ref, ext_ref) = refs[8 + n_over:8 + n_over + 8]
    y_ref = None if over else refs[-1]
    qk_width = n_heads * key_dim
    conv_width = wqkv_ref.shape[1]

    h = _rms(x_ref[...], g_ref[...]).astype(BF16)
    tm = h.shape[0]
    first = pl.program_id(0) % tiles_per_seq == 0 if over is None else pl.program_id(0) == 0
    _conv_prologue(ext_ref, first, conv_width)

    cw = CONV_CHUNK if over is None else conv_width
    n_conv_chunks = conv_width // cw
    v_width = wz_ref.shape[1]
    zw = -(-v_width // (n_conv_chunks * LANES)) * LANES
    assert v_width % zw == 0
    nxt = _dot(h, wqkv_ref[:, :cw])
    for c0 in range(0, conv_width, cw):
        cur = nxt
        if c0 + cw < conv_width:
            nxt = _dot(h, wqkv_ref[:, c0 + cw:c0 + 2 * cw])
        z0 = (c0 // cw) * zw
        if z0 < v_width:
            z_ref[:, z0:z0 + zw] = _dot(h, wz_ref[:, z0:z0 + zw]).astype(BF16)
        y = _silu(_causal_conv(cur, taps_ref, ext_ref, over, seq_len, y_ref, c0))
        if over is None:
            tail = cur[tm - SUBLANES:, :]
            u_ref[:, c0:c0 + cw] = tail
            _conv_carry(ext_ref, tail, c0)
        else:
            u_ref[...] = cur
        for b0 in range(0, cw, key_dim):
            col = c0 + b0
            blk = y[:, b0:b0 + key_dim]
            if col < 2 * qk_width:
                blk = blk * lax.rsqrt(jnp.sum(blk * blk, axis=-1, keepdims=True) + NORM_EPS)
            if col < qk_width:
                q_ref[:, col:col + key_dim] = (blk * key_dim ** -0.5).astype(BF16)
            elif col < 2 * qk_width:
                k_ref[:, col - qk_width:col - qk_width + key_dim] = blk.astype(BF16)
            else:
                v_ref[:, col - 2 * qk_width:col - 2 * qk_width + key_dim] = blk.astype(BF16)

    ba = _dot(h, wba_ref[...])
    beta_ref[...] = jax.nn.sigmoid(ba)[:, :n_heads]
    gdec = -jnp.exp(alog_ref[...]) * _softplus(ba + dtb_ref[...])
    gdec_ref[...] = gdec[:, n_heads:2 * n_heads]


def _c_proj(x, gain, wqkv, wz, wba, taps, alog, dtb, overrides, *, n_heads, key_dim, seq_len, tm):
    t, d = x.shape
    conv_width = wqkv.shape[1]
    qk_width = n_heads * key_dim
    v_width = conv_width - 2 * qk_width
    n_tiles = t // tm
    n_over = 0 if overrides is None else len(overrides)
    row = lambda c: pl.BlockSpec((tm, c), lambda i: (i, 0))
    in_specs = [row(d)] + [_const_spec(a) for a in (gain, wqkv, wz, wba, taps, alog, dtb)]
    in_specs += [row(conv_width)] * n_over
    if overrides is None:
        u_spec = pl.BlockSpec((None, SUBLANES, conv_width), lambda i: (i, 0, 0))
        u_shape = jax.ShapeDtypeStruct((n_tiles, SUBLANES, conv_width), F32)
    else:
        u_spec = row(conv_width)
        u_shape = jax.ShapeDtypeStruct((t, conv_width), F32)
    out_specs = [row(qk_width), row(qk_width), row(v_width), row(v_width), row(n_heads), row(n_heads), u_spec]
    out_shape = [jax.ShapeDtypeStruct((t, qk_width), BF16), jax.ShapeDtypeStruct((t, qk_width), BF16),
                 jax.ShapeDtypeStruct((t, v_width), BF16), jax.ShapeDtypeStruct((t, v_width), BF16),
                 jax.ShapeDtypeStruct((t, n_heads), F32), jax.ShapeDtypeStruct((t, n_heads), F32), u_shape]
    return pl.pallas_call(
        functools.partial(_c_proj_kernel, n_heads=n_heads, key_dim=key_dim, seq_len=seq_len,
                          tiles_per_seq=max(seq_len // tm, 1), n_over=n_over),
        grid=(n_tiles,),
        in_specs=in_specs,
        out_specs=out_specs,
        out_shape=out_shape,
        scratch_shapes=_conv_scratch(tm, conv_width, overrides is None),
        compiler_params=_params("arbitrary"),
        name="c_proj",
    )(x, gain, wqkv, wz, wba, taps, alog, dtb, *(overrides or []))


def _delta_kernel(q_ref, k_ref, v_ref, z_ref, gcol_ref, beta_ref, grow_ref, s0_ref, nw_ref,
                  o_ref, s_out_ref, m_ref, *, n_heads, chunk):
    step = pl.program_id(1)
    n_batch = s0_ref.shape[0]
    rows = q_ref.shape[0] // n_batch
    dk = q_ref.shape[1] // n_heads
    dv = v_ref.shape[1] // n_heads
    n_chunks = rows // chunk
    seqs = [(b, h) for b in range(n_batch) for h in range(n_heads)]
    ri = lax.broadcasted_iota(jnp.int32, (chunk, chunk), 0)
    ci = lax.broadcasted_iota(jnp.int32, (chunk, chunk), 1)
    incl = (ci <= ri)[None]
    strict = (ci < ri)[None]

    @pl.when(step == 0)
    def _():
        for i, (b, h) in enumerate(seqs):
            m_ref[i] = s0_ref[b, h].T

    probs = [(c, b, h) for c in range(n_chunks) for b, h in seqs]
    rs = lambda c, b: slice(b * rows + c * chunk, b * rows + (c + 1) * chunk)
    stack = lambda f: jnp.stack([f(c, b, h) for c, b, h in probs])
    kb = stack(lambda c, b, h: k_ref[rs(c, b), h * dk:(h + 1) * dk])
    qb = stack(lambda c, b, h: q_ref[rs(c, b), h * dk:(h + 1) * dk])
    vf = stack(lambda c, b, h: v_ref[rs(c, b), h * dv:(h + 1) * dv]).astype(F32)
    gc = stack(lambda c, b, h: gcol_ref[rs(c, b), h:h + 1])
    gr = stack(lambda c, b, h: grow_ref[b, h:h + 1, rs(c, 0)])
    bc = stack(lambda c, b, h: beta_ref[rs(c, b), h:h + 1])
    kf = kb.astype(F32)
    eg = jnp.exp(gc)
    decay = jnp.where(incl, jnp.exp(jnp.where(incl, gc - gr, 0.0)), 0.0)
    kqk = _bdot_nt(jnp.concatenate([kb, qb], axis=1), kb)
    a = jnp.where(strict, bc * kqk[:, :chunk] * decay, 0.0)
    nil = -a
    pw = a
    span = 2
    while span < chunk:
        pwb = pw.astype(BF16)
        pw = _bdot(pwb, pwb)
        nil = nil + pw + _bdot(nil.astype(BF16), pw.astype(BF16))
        span *= 2
    rhs = jnp.concatenate([bc * vf, (bc * eg) * kf], axis=2)
    w = rhs + _bdot(nil.astype(BF16), rhs.astype(BF16))
    w_v = w[:, :, :dv]
    p = (kqk[:, chunk:] * decay).astype(BF16)
    g_last = gc[:, chunk - 1:chunk, :]
    lhs = jnp.concatenate([w[:, :, dv:], eg * qb.astype(F32)], axis=1).astype(BF16)
    k_dec = (jnp.exp(g_last - gc) * kf).astype(BF16)
    g_chunk = jnp.exp(g_last)

    for c in range(n_chunks):
        hs = slice(c * len(seqs), (c + 1) * len(seqs))
        m = m_ref[...]
        r = _bdot(lhs[hs], m.astype(BF16))
        ub = (w_v[hs] - r[:, :chunk]).astype(BF16)
        o = r[:, chunk:] + _bdot(p[hs], ub)
        m_ref[...] = g_chunk[hs] * m + _bdot_tn(k_dec[hs], ub)
        on = o * lax.rsqrt(jnp.mean(o * o, axis=-1, keepdims=True) + NORM_EPS) * nw_ref[...]
        for i, (b, h) in enumerate(seqs):
            vl = slice(h * dv, (h + 1) * dv)
            o_ref[rs(c, b), vl] = (on[i] * _silu(z_ref[rs(c, b), vl].astype(F32))).astype(BF16)

    @pl.when(step == pl.num_programs(1) - 1)
    def _():
        for i, (b, h) in enumerate(seqs):
            s_out_ref[b, h] = m_ref[i].T


def _delta_rule(q, k, v, z, gcol, beta, grow, s0, norm_w, *, batch, seq_len, n_heads, chunk, rows, n_batch):
    t = q.shape[0]
    steps = seq_len // rows
    dk = q.shape[1] // n_heads
    dv = v.shape[1] // n_heads
    assert n_batch == 1 or steps == 1
    row = lambda c: pl.BlockSpec((n_batch * rows, c), lambda b, s: (b * steps + s, 0))
    state = pl.BlockSpec((n_batch, n_heads, dv, dk), lambda b, s: (b, 0, 0, 0))
    return pl.pallas_call(
        functools.partial(_delta_kernel, n_heads=n_heads, chunk=chunk),
        grid=(batch // n_batch, steps),
        in_specs=[row(q.shape[1]), row(k.shape[1]), row(v.shape[1]), row(z.shape[1]),
                  row(n_heads), row(n_heads),
                  pl.BlockSpec((n_batch, n_heads, rows), lambda b, s: (b, 0, s)),
                  state, pl.BlockSpec(norm_w.shape, lambda b, s: (0, 0))],
        out_specs=[row(v.shape[1]), state],
        out_shape=[jax.ShapeDtypeStruct((t, v.shape[1]), BF16),
                   jax.ShapeDtypeStruct((batch, n_heads, dv, dk), F32)],
        scratch_shapes=[pltpu.VMEM((n_batch * n_heads, dk, dv), F32)],
        compiler_params=_params("arbitrary", "arbitrary"),
        name="gated_delta_rule",
    )(q, k, v, z, gcol, beta, grow, s0, norm_w)


def _pad_lanes(a):
    return jnp.pad(a, ((0, 0), (0, LANES - a.shape[1])))


def _rows_to_heads(a, batch, seq_len):
    return a.reshape(batch, seq_len, a.shape[1]).transpose(0, 2, 1)


def _heads_to_rows(a):
    b, h, length = a.shape
    return a.transpose(0, 2, 1).reshape(b * length, h)


def _feature_major_pool(pool):
    n_l, n_p, page = pool.shape[:3]
    perm = (0, 1) + tuple(range(3, pool.ndim)) + (2,)
    return pool.transpose(perm).reshape(n_l * n_p, -1, page)


def _mixer_ab(x, past, w, i, gain, *, batch, seq_len, tm):
    t = x.shape[0]
    hd = w["a_head_dim"]
    nh = w["ab_b_f"].shape[1]
    aw = nh * hd
    w_in = w["ab_w_in"][i]
    bw = (w_in.shape[1] - 3 * aw - nh) // 3
    wb = w_in[:, 3 * aw + nh:].astype(BF16)
    wf = w_in[:, 3 * aw:3 * aw + nh]
    taps = w["ab_conv_w"][i]
    n_keep = taps.shape[0] - 1
    if past is None:
        wq = w_in[:, :aw].astype(BF16)
        wkvt = w_in[:, aw:3 * aw].T.astype(BF16)
        kt, vt, ktb, vtb, lft, qb, ob, tails = _ab_proj_prompt(
            x, gain, wq, wkvt, wf.T.astype(BF16), w["ab_b_f"][i][:, None], wb, taps,
            batch=batch, seq_len=seq_len, head_dim=hd, tm=tm)
        o_a = _fox_attention(qb, ktb, vtb, _forget_bias_rows(lft, hd), head_dim=hd,
                             tile=min(1024, seq_len), kblock=min(512, seq_len))
        tails = tails.reshape(batch, seq_len // tm, SUBLANES, bw)[:, -1]
        new_buf = tails[:, SUBLANES - n_keep:, :]
        to_tokens = lambda a: a.reshape(batch, nh, hd, seq_len).transpose(0, 3, 1, 2)
        state = (to_tokens(kt), to_tokens(vt), lft.transpose(0, 2, 1), new_buf)
    else:
        overrides = _conv_overrides(past["state_b_conv"][i], seq_len)
        q_s, k32, v32, logf, ob, gcu = _ab_proj_sample(
            x, gain, w_in[:, :3 * aw].astype(BF16), _pad_lanes(wf).astype(BF16),
            _pad_lanes(w["ab_b_f"][i][None, :]), wb, taps, overrides,
            n_heads=nh, head_dim=hd, seq_len=seq_len)
        n_phys, page = past["cache_a_k"].shape[1:3]
        pad_keys = lambda a: jnp.pad(a.reshape(batch, seq_len, -1).transpose(0, 2, 1),
                                     ((0, 0), (0, 0), (0, page - seq_len)))
        o_a = _paged_attention(
            q_s.reshape(batch, seq_len, aw), pad_keys(k32), pad_keys(v32), pad_keys(logf),
            _feature_major_pool(past["cache_a_k"]), _feature_major_pool(past["cache_a_v"]),
            _feature_major_pool(past["cache_a_logf"]), past["page_table"] + i * n_phys, head_dim=hd)
        o_a = o_a.reshape(t, aw)
        new_buf = gcu.reshape(batch, seq_len, bw)[:, seq_len - n_keep:, :]
        state = (k32.reshape(batch, seq_len, nh, hd), v32.reshape(batch, seq_len, nh, hd),
                 logf.reshape(batch, seq_len, nh), new_buf)
    return [o_a, ob], w["ab_w_out"][i].astype(BF16), state


def _mixer_c(x, past, w, i, gain, *, batch, seq_len, tm):
    t = x.shape[0]
    nh = w["c_a_log"].shape[1]
    dv = w["c_norm_w"].shape[1]
    w_in = w["c_w_in"][i]
    v_width = nh * dv
    conv_width = w_in.shape[1] - v_width - 2 * nh
    dk = (conv_width - v_width) // 2 // nh
    wqkv = w_in[:, :conv_width].astype(BF16)
    wz = w_in[:, conv_width:conv_width + v_width].astype(BF16)
    wba = _pad_lanes(w_in[:, conv_width + v_width:]).astype(BF16)
    zeros_h = jnp.zeros((1, nh), F32)
    alog = _pad_lanes(jnp.concatenate([zeros_h, w["c_a_log"][i][None, :]], axis=1))
    dtb = _pad_lanes(jnp.concatenate([zeros_h, w["c_dt_bias"][i][None, :]], axis=1))
    taps = w["c_conv_w"][i]
    n_keep = taps.shape[0] - 1
    tm_c = min(tm, 512)
    if past is None:
        overrides = None
        s0 = jnp.zeros((batch, nh, dv, dk), F32)
    else:
        overrides = _conv_overrides(past["state_c_conv"][i], seq_len)
        s0 = past["state_c_S"][i]
    qn, kn, vn, zb, beta, gdec, u_out = _c_proj(
        x, gain, wqkv, wz, wba, taps, alog, dtb, overrides,
        n_heads=nh, key_dim=dk, seq_len=seq_len, tm=tm_c)
    if past is None:
        tails = u_out.reshape(batch, seq_len // tm_c, SUBLANES, conv_width)[:, -1]
        new_buf = tails[:, SUBLANES - n_keep:, :]
        len_pad = seq_len
    else:
        new_buf = u_out.reshape(batch, seq_len, conv_width)[:, seq_len - n_keep:, :]
        len_pad = -(-seq_len // DELTA_CHUNK) * DELTA_CHUNK
        pad3 = lambda a: jnp.pad(a.reshape(batch, seq_len, a.shape[1]),
                                 ((0, 0), (0, len_pad - seq_len), (0, 0))).reshape(batch * len_pad, a.shape[1])
        qn, kn, vn, zb, beta, gdec = [pad3(a) for a in (qn, kn, vn, zb, beta, gdec)]
    n_batch = DELTA_BATCH if len_pad <= DELTA_ROWS and batch % DELTA_BATCH == 0 else 1
    scan_len = -(-len_pad // LANES) * LANES
    g_heads = jnp.pad(_rows_to_heads(gdec, batch, len_pad), ((0, 0), (0, 0), (0, scan_len - len_pad)))
    grow = _segment_cumsum(g_heads, DELTA_CHUNK)[:, :, :len_pad]
    og, s_new = _delta_rule(qn, kn, vn, zb, _heads_to_rows(grow), beta, grow, s0, w["c_norm_w"][i][None, :],
                            batch=batch, seq_len=len_pad, n_heads=nh, chunk=DELTA_CHUNK,
                            rows=min(DELTA_ROWS, len_pad), n_batch=n_batch)
    if len_pad != seq_len:
        og = og.reshape(batch, len_pad, v_width)[:, :seq_len].reshape(t, v_width)
    return [og], w["c_w_out"][i].astype(BF16), (new_buf, s_new)


def _run_trunk(x3, past, w, *, tm):
    batch, seq_len, d = x3.shape
    x = x3.reshape(batch * seq_len, d)
    ab_states, c_states = [], []
    ffn = lambda x, g0, g1, layer, j, mixer=None: _ffn_block(
        x, g0, g1, w["ffn_w_gate"], w["ffn_w_up"], w["ffn_w_down"], layer, j, tm=tm, mixer=mixer)
    for layer in range(w["norm_g"].shape[0]):
        gains = [w["norm_g"][layer, j][None, :] for j in range(6)]
        x = ffn(x, gains[0], gains[1], layer, 0)
        mixer, states = (_mixer_ab, ab_states) if layer % 2 == 0 else (_mixer_c, c_states)
        parts, w_out, state = mixer(x, past, w, layer // 2, gains[2], batch=batch, seq_len=seq_len, tm=tm)
        states.append(state)
        x = ffn(x, gains[4], gains[5], layer, 1, mixer=(parts, w_out, gains[3]))
    ab_new = [jnp.stack(ts) for ts in zip(*ab_states)]
    c_new = [jnp.stack(ts) for ts in zip(*c_states)]
    return x.reshape(batch, seq_len, d), ab_new, c_new


def kernel(x_prompt, x_sample, cache_a_k, cache_a_v, cache_a_logf, page_table, state_b_conv, state_c_conv, state_c_S, norm_g, ffn_w_gate, ffn_w_up, ffn_w_down, ab_w_in, ab_b_f, ab_conv_w, ab_w_out, c_w_in, c_conv_w, c_a_log, c_dt_bias, c_norm_w, c_w_out):
    w = dict(norm_g=norm_g,
             ffn_w_gate=ffn_w_gate, ffn_w_up=ffn_w_up, ffn_w_down=ffn_w_down,
             ab_w_in=ab_w_in, ab_b_f=ab_b_f, ab_conv_w=ab_conv_w, ab_w_out=ab_w_out,
             c_w_in=c_w_in, c_conv_w=c_conv_w, c_a_log=c_a_log, c_dt_bias=c_dt_bias,
             c_norm_w=c_norm_w, c_w_out=c_w_out, a_head_dim=cache_a_k.shape[-1])
    past = dict(cache_a_k=cache_a_k, cache_a_v=cache_a_v, cache_a_logf=cache_a_logf, page_table=page_table,
                state_b_conv=state_b_conv, state_c_conv=state_c_conv, state_c_S=state_c_S)
    prompt_rows = x_prompt.shape[0] * x_prompt.shape[1]
    sample_rows = x_sample.shape[0] * x_sample.shape[1]
    y_prompt, ab_p, c_p = _run_trunk(x_prompt, None, w, tm=min(512, prompt_rows))
    y_sample, ab_s, c_s = _run_trunk(x_sample, past, w, tm=min(512, sample_rows))
    a_k_prompt, a_v_prompt, a_logf_prompt, b_conv_prompt = ab_p
    a_k_sample, a_v_sample, a_logf_sample, b_conv_sample = ab_s
    c_conv_prompt, c_S_prompt = c_p
    c_conv_sample, c_S_sample = c_s
    return (y_prompt, y_sample,
            a_k_prompt, a_v_prompt, a_logf_prompt,
            a_k_sample, a_v_sample, a_logf_sample,
            b_conv_prompt, b_conv_sample,
            c_conv_prompt, c_conv_sample,
            c_S_prompt, c_S_sample)
```

```python
import functools

import jax
import jax.numpy as jnp
from jax import lax
from jax.experimental import pallas as pl
from jax.experimental.pallas import tpu as pltpu

F32 = jnp.float32
BF16 = jnp.bfloat16
NORM_EPS = 1e-6
LOG2_E = 1.4426950408889634

LANES = 128
SUBLANES = 8
VMEM_LIMIT_BYTES = 56 * 1024 * 1024
DELTA_CHUNK = 64
DELTA_ROWS = 512
DELTA_BATCH = 4
CONV_ROW_STRIDE = 4
CONV_CHUNK = 512
PAGES_PER_STEP = 16
PAGE_GROUPS = 2
PAGE_SLOTS = 3


def _params(*semantics):
    return pltpu.CompilerParams(dimension_semantics=semantics, vmem_limit_bytes=VMEM_LIMIT_BYTES)


def _resident(shape, index_map):
    return pl.BlockSpec(shape, index_map, pipeline_mode=pl.Buffered(1))


def _const_spec(a):
    return _resident(a.shape, lambda *_: (0,) * a.ndim)


def _rms(x, g):
    return x * lax.rsqrt(jnp.mean(x * x, axis=-1, keepdims=True) + NORM_EPS) * g


def _softplus(x):
    return jnp.maximum(x, 0.0) + jnp.log1p(jnp.exp(-jnp.abs(x)))


def _silu(x):
    return x * jax.nn.sigmoid(x)


def _dot(a, b):
    return jnp.dot(a, b, preferred_element_type=F32)


def _dot_nt(a, b):
    return lax.dot_general(a, b, (((1,), (1,)), ((), ())), preferred_element_type=F32)


def _dot_tn(a, b):
    return lax.dot_general(a, b, (((0,), (0,)), ((), ())), preferred_element_type=F32)


def _bdot(a, b):
    return lax.dot_general(a, b, (((2,), (1,)), ((0,), (0,))), preferred_element_type=F32)


def _bdot_nt(a, b):
    return lax.dot_general(a, b, (((2,), (2,)), ((0,), (0,))), preferred_element_type=F32)


def _bdot_tn(a, b):
    return lax.dot_general(a, b, (((1,), (1,)), ((0,), (0,))), preferred_element_type=F32)


def _mix_rows(part_refs, w_ref):
    mix = None
    k0 = 0
    for a_ref in part_refs:
        kw = a_ref.shape[1]
        part = _dot(a_ref[...], w_ref[k0:k0 + kw, :])
        mix = part if mix is None else mix + part
        k0 += kw
    return mix


def _ffn_kernel(*refs, n_parts, ff_chunk):
    part_refs = refs[:n_parts]
    if n_parts:
        w_mix_ref, g_mix_ref = refs[n_parts:n_parts + 2]
        refs = refs[n_parts + 2:]
    x_ref, gpre_ref, gpost_ref, wg_ref, wu_ref, wd_ref, o_ref = refs
    x = x_ref[...]
    if n_parts:
        x = x + _rms(_mix_rows(part_refs, w_mix_ref), g_mix_ref[...])
    h = _rms(x, gpre_ref[...]).astype(BF16)
    d_ff = wg_ref.shape[1]
    proj = lambda c0: (_dot(h, wg_ref[:, c0:c0 + ff_chunk].astype(BF16)),
                       _dot(h, wu_ref[:, c0:c0 + ff_chunk].astype(BF16)))
    y = None
    nxt = proj(0)
    for c0 in range(0, d_ff, ff_chunk):
        gate, up = nxt
        if c0 + ff_chunk < d_ff:
            nxt = proj(c0 + ff_chunk)
        act = (_silu(gate) * up).astype(BF16)
        part = _dot(act, wd_ref[c0:c0 + ff_chunk, :].astype(BF16))
        y = part if y is None else y + part
    o_ref[...] = x + 0.5 * _rms(y, gpost_ref[...])


def _ffn_chunk(d_ff):
    for c in range(4 * LANES, 0, -LANES):
        if d_ff % c == 0:
            return c
    return d_ff


def _ffn_block(x, g_pre, g_post, wg_all, wu_all, wd_all, layer, j, *, tm, mixer=None):
    t, d = x.shape
    d_ff = wg_all.shape[-1]
    row = lambda c: pl.BlockSpec((tm, c), lambda i: (i, 0))
    gain = _resident((1, d), lambda i: (0, 0))
    w_in = _resident((None, None, d, d_ff), lambda i: (layer, j, 0, 0))
    w_out = _resident((None, None, d_ff, d), lambda i: (layer, j, 0, 0))
    parts, w_mix, g_mix = mixer if mixer is not None else ([], None, None)
    mix_args = list(parts) + ([w_mix, g_mix] if parts else [])
    mix_specs = [row(p.shape[1]) for p in parts] + ([_const_spec(w_mix), gain] if parts else [])
    return pl.pallas_call(
        functools.partial(_ffn_kernel, n_parts=len(parts), ff_chunk=_ffn_chunk(d_ff)),
        grid=(t // tm,),
        in_specs=mix_specs + [row(d), gain, gain, w_in, w_in, w_out],
        out_specs=row(d),
        out_shape=jax.ShapeDtypeStruct((t, d), F32),
        compiler_params=_params("arbitrary"),
        name="ffn_block",
    )(*mix_args, x, g_pre, g_post, wg_all, wu_all, wd_all)


def _causal_conv(u, taps_ref, ext_ref, overrides, seq_len, y_ref=None, col0=0):
    tm = u.shape[0]
    n_taps = taps_ref.shape[0]
    if overrides is None:
        stride = CONV_ROW_STRIDE
        group = SUBLANES * stride
        assert n_taps <= stride + 1 <= SUBLANES + 1
        blocks = range(col0 // LANES, (col0 + u.shape[1]) // LANES)
        for lb in blocks:
            lanes = slice(lb * LANES, (lb + 1) * LANES)
            ext_ref[lb, pl.ds(SUBLANES, tm), :] = u[:, lb * LANES - col0:(lb + 1) * LANES - col0]
            taps = [taps_ref[n_taps - 1 - j:n_taps - j, lanes] for j in range(n_taps)]
            for g0 in range(0, tm, group):
                win = {d: ext_ref[lb, pl.ds(SUBLANES + g0 + d, SUBLANES, stride=stride), :]
                       for d in range(1 - n_taps, stride)}
                for i in range(stride):
                    acc = win[i] * taps[0]
                    for j in range(1, n_taps):
                        acc = acc + win[i - j] * taps[j]
                    y_ref[lb, pl.ds(g0 + i, SUBLANES, stride=stride), :] = acc
        return jnp.concatenate([y_ref[lb] for lb in blocks], axis=1)
    ext_ref[pl.ds(SUBLANES, tm), :] = u
    out = u * taps_ref[n_taps - 1:n_taps, :]
    pos = lax.broadcasted_iota(jnp.int32, u.shape, 0) % seq_len
    for j in range(1, n_taps):
        prev = jnp.where(pos >= j, ext_ref[pl.ds(SUBLANES - j, tm), :], overrides[j - 1][...])
        out = out + prev * taps_ref[n_taps - 1 - j:n_taps - j, :]
    return out


def _conv_overrides(state, seq_len):
    b, wm1, c = state.shape
    assert seq_len >= wm1
    outs = []
    for j in range(1, wm1 + 1):
        rows = jnp.pad(state[:, wm1 - j:, :], ((0, 0), (0, seq_len - j), (0, 0)))
        outs.append(rows.reshape(b * seq_len, c))
    return outs


def _conv_carry(ext_ref, tail, col0=0):
    if len(ext_ref.shape) == 2:
        ext_ref[pl.ds(0, SUBLANES), :] = tail
    else:
        for lb in range(tail.shape[1] // LANES):
            ext_ref[col0 // LANES + lb, pl.ds(0, SUBLANES), :] = tail[:, lb * LANES:(lb + 1) * LANES]


def _conv_prologue(ext_ref, first, width):
    @pl.when(first)
    def _():
        _conv_carry(ext_ref, jnp.zeros((SUBLANES, width), F32))


def _conv_scratch(tm, width, blocked):
    if not blocked:
        return [pltpu.VMEM((SUBLANES + tm, width), F32)]
    assert tm % (SUBLANES * CONV_ROW_STRIDE) == 0 and width % LANES == 0
    return [pltpu.VMEM((width // LANES, SUBLANES + tm, LANES), F32), pltpu.VMEM((width // LANES, tm, LANES), F32)]


def _log_sigmoid(z):
    return -_softplus(-z)


def _ab_proj_prompt_kernel(x_ref, g_ref, wq_ref, wkvt_ref, wft_ref, bf_ref, wb_ref, taps_ref,
                           kt_ref, vt_ref, ktb_ref, vtb_ref, lft_ref, qb_ref, ob_ref, tail_ref, ext_ref, y_ref,
                           *, head_dim, tiles_per_seq):
    a_width = wq_ref.shape[1]
    b_width = wb_ref.shape[1] // 3
    h = _rms(x_ref[...], g_ref[...]).astype(BF16)
    qb_ref[...] = (_dot(h, wq_ref[...]) * (head_dim ** -0.5 * LOG2_E)).astype(BF16)
    kvt = _dot_nt(wkvt_ref[...], h)
    kt = kvt[:a_width]
    vt = kvt[a_width:]
    kt_ref[...] = kt
    vt_ref[...] = vt
    ktb_ref[...] = kt.astype(BF16)
    vtb_ref[...] = vt.astype(BF16)
    lft_ref[...] = _log_sigmoid(_dot_nt(wft_ref[...], h) + bf_ref[...])

    gbu = _dot(h, wb_ref[...])
    gate_b = gbu[:, :b_width]
    gcu = gbu[:, b_width:2 * b_width] * gbu[:, 2 * b_width:]
    _conv_prologue(ext_ref, pl.program_id(0) % tiles_per_seq == 0, b_width)
    y = _causal_conv(gcu, taps_ref, ext_ref, None, None, y_ref)
    ob_ref[...] = (gate_b * y).astype(BF16)
    tail = gcu[gcu.shape[0] - SUBLANES:, :]
    tail_ref[...] = tail
    _conv_carry(ext_ref, tail)


def _ab_proj_prompt(x, gain, wq, wkvt, wft, bf_col, wb, taps, *, batch, seq_len, head_dim, tm):
    t, d = x.shape
    a_width = wq.shape[1]
    n_heads = wft.shape[0]
    b_width = wb.shape[1] // 3
    tiles_per_seq = seq_len // tm
    row = lambda c: pl.BlockSpec((tm, c), lambda i: (i, 0))
    feat = lambda c: pl.BlockSpec((None, c, tm), lambda i: (i // tiles_per_seq, 0, i % tiles_per_seq))
    feat_shape = lambda c, dt: jax.ShapeDtypeStruct((batch, c, seq_len), dt)
    return pl.pallas_call(
        functools.partial(_ab_proj_prompt_kernel, head_dim=head_dim, tiles_per_seq=tiles_per_seq),
        grid=(t // tm,),
        in_specs=[row(d)] + [_const_spec(a) for a in (gain, wq, wkvt, wft, bf_col, wb, taps)],
        out_specs=[feat(a_width), feat(a_width), feat(a_width), feat(a_width), feat(n_heads),
                   row(a_width), row(b_width),
                   pl.BlockSpec((None, SUBLANES, b_width), lambda i: (i, 0, 0))],
        out_shape=[feat_shape(a_width, F32), feat_shape(a_width, F32), feat_shape(a_width, BF16),
                   feat_shape(a_width, BF16), feat_shape(n_heads, F32),
                   jax.ShapeDtypeStruct((t, a_width), BF16), jax.ShapeDtypeStruct((t, b_width), BF16),
                   jax.ShapeDtypeStruct((t // tm, SUBLANES, b_width), F32)],
        scratch_shapes=_conv_scratch(tm, b_width, True),
        compiler_params=_params("arbitrary"),
        name="ab_proj_prompt",
    )(x, gain, wq, wkvt, wft, bf_col, wb, taps)


def _ab_proj_sample_kernel(*refs, n_heads, head_dim, seq_len, n_over):
    x_ref, g_ref, wqkv_ref, wf_ref, bf_ref, wb_ref, taps_ref = refs[:7]
    over = list(refs[7:7 + n_over])
    q_ref, k_ref, v_ref, logf_ref, ob_ref, u_ref, ext_ref = refs[7 + n_over:]
    a_width = n_heads * head_dim
    b_width = wb_ref.shape[1] // 3
    h = _rms(x_ref[...], g_ref[...]).astype(BF16)
    qkv = _dot(h, wqkv_ref[...])
    q_ref[...] = qkv[:, :a_width] * head_dim ** -0.5
    k_ref[...] = qkv[:, a_width:2 * a_width]
    v_ref[...] = qkv[:, 2 * a_width:]
    logf_ref[...] = _log_sigmoid(_dot(h, wf_ref[...]) + bf_ref[...])[:, :n_heads]

    gbu = _dot(h, wb_ref[...])
    gate_b = gbu[:, :b_width]
    gcu = gbu[:, b_width:2 * b_width] * gbu[:, 2 * b_width:]
    _conv_prologue(ext_ref, pl.program_id(0) == 0, b_width)
    y = _causal_conv(gcu, taps_ref, ext_ref, over, seq_len)
    ob_ref[...] = (gate_b * y).astype(BF16)
    u_ref[...] = gcu


def _ab_proj_sample(x, gain, wqkv, wf, bf, wb, taps, overrides, *, n_heads, head_dim, seq_len):
    t, d = x.shape
    a_width = n_heads * head_dim
    b_width = wb.shape[1] // 3
    full = lambda c: pl.BlockSpec((t, c), lambda i: (0, 0))
    f32 = lambda c: jax.ShapeDtypeStruct((t, c), F32)
    return pl.pallas_call(
        functools.partial(_ab_proj_sample_kernel, n_heads=n_heads, head_dim=head_dim, seq_len=seq_len,
                          n_over=len(overrides)),
        grid=(1,),
        in_specs=[full(d)] + [_const_spec(a) for a in (gain, wqkv, wf, bf, wb, taps)]
        + [full(b_width)] * len(overrides),
        out_specs=[full(a_width), full(a_width), full(a_width), full(n_heads), full(b_width), full(b_width)],
        out_shape=[f32(a_width), f32(a_width), f32(a_width), f32(n_heads),
                   jax.ShapeDtypeStruct((t, b_width), BF16), f32(b_width)],
        scratch_shapes=[pltpu.VMEM((SUBLANES + t, b_width), F32)],
        compiler_params=_params("arbitrary"),
        name="ab_proj_sample",
    )(x, gain, wqkv, wf, bf, wb, taps, *overrides)


def _scan_kernel(x_ref, o_ref, *, seg):
    x = x_ref[...]
    pos = lax.broadcasted_iota(jnp.int32, x.shape, 1) % seg
    d = 1
    while d < seg:
        x = x + jnp.where(pos >= d, pltpu.roll(x, d, axis=1), 0.0)
        d *= 2
    o_ref[...] = x


def _segment_cumsum(x, seg):
    b, h, length = x.shape
    spec = pl.BlockSpec((b * h, length), lambda i: (0, 0))
    return pl.pallas_call(
        functools.partial(_scan_kernel, seg=seg),
        grid=(1,),
        in_specs=[spec],
        out_specs=spec,
        out_shape=jax.ShapeDtypeStruct((b * h, length), F32),
        compiler_params=_params("arbitrary"),
        name="segment_cumsum",
    )(x.reshape(b * h, length)).reshape(x.shape)


def _split3(x):
    hi = x.astype(BF16)
    r = x - hi.astype(F32)
    mid = r.astype(BF16)
    lo = (r - mid.astype(F32)).astype(BF16)
    return hi, mid, lo


def _forget_bias_kernel(lf_ref, o_ref, *, head_dim):
    x = lf_ref[...]
    n_heads, length = x.shape
    pos = lax.broadcasted_iota(jnp.int32, x.shape, 1)
    d = 1
    while d < length:
        x = x + jnp.where(pos >= d, pltpu.roll(x, d, axis=1), 0.0)
        d *= 2
    pieces = [p.astype(F32) for p in _split3(-LOG2_E * x)]
    pad = jnp.zeros((head_dim - len(pieces), length), F32)
    for h in range(n_heads):
        blk = jnp.concatenate([p[h:h + 1] for p in pieces] + [pad], axis=0)
        o_ref[pl.ds(h * head_dim, head_dim), :] = blk.astype(BF16)


def _forget_bias_rows(lft, head_dim):
    batch, n_heads, length = lft.shape
    return pl.pallas_call(
        functools.partial(_forget_bias_kernel, head_dim=head_dim),
        grid=(batch,),
        in_specs=[pl.BlockSpec((None, n_heads, length), lambda b: (b, 0, 0))],
        out_specs=pl.BlockSpec((None, n_heads * head_dim, length), lambda b: (b, 0, 0)),
        out_shape=jax.ShapeDtypeStruct((batch, n_heads * head_dim, length), BF16),
        compiler_params=_params("arbitrary"),
        name="forget_bias_rows",
    )(lft)


def _fox_kernel(q_ref, kt_ref, vt_ref, cb_ref, o_ref, *, tile, kblock, head_dim):
    qi = pl.program_id(2)
    heads = q_ref.shape[1] // head_dim
    sub = tile // kblock
    row = lax.broadcasted_iota(jnp.int32, (kblock, kblock), 0)
    col = lax.broadcasted_iota(jnp.int32, (kblock, kblock), 1)
    head_of_lane = lax.broadcasted_iota(jnp.int32, q_ref.shape, 1) // head_dim
    q_pair = q_ref[...]
    ones_rows = jnp.ones((head_dim, kblock), BF16)
    owns = [head_of_lane == j for j in range(heads)]
    q_augs = [jnp.where(own, q_pair, jnp.ones_like(q_pair)) for own in owns]

    def blocks(kis, carry, q_rows=slice(None), triangular=False):
        offs = [pl.multiple_of(ki * kblock, kblock) for ki in kis]
        feats = [slice(j * head_dim, (j + 1) * head_dim) for j in range(heads)]
        order = lambda j, data, aux: [data, aux] if j == 0 else [aux, data]
        scores = [[_dot(q_augs[j][q_rows], jnp.concatenate(
            order(j, kt_ref[feats[j], pl.ds(off, kblock)], cb_ref[feats[j], pl.ds(off, kblock)]), axis=0))
            for j in range(heads)] for off in offs]
        carry = list(carry)
        for off, score in zip(offs, scores):
            for j, (m, acc) in enumerate(carry):
                s = jnp.where(col <= row, score[j], -jnp.inf) if triangular else score[j]
                m_new = jnp.maximum(m, jnp.max(s, axis=1, keepdims=True))
                p = jnp.exp2(s - m_new).astype(BF16)
                vt_aug = jnp.concatenate(order(j, vt_ref[feats[j], pl.ds(off, kblock)], ones_rows), axis=0)
                carry[j] = (m_new, jnp.exp2(m - m_new) * acc + _dot_nt(p, vt_aug))
        return tuple(carry)

    init = tuple((jnp.full((tile, 1), -jnp.inf, F32), jnp.zeros(q_ref.shape, F32)) for _ in range(heads))
    carry = lax.fori_loop(0, qi, lambda i, c: blocks([i * sub + r for r in range(sub)], c), init)
    pieces = []
    for i in range(sub):
        q_rows = slice(i * kblock, (i + 1) * kblock)
        piece = tuple((m[q_rows], acc[q_rows]) for m, acc in carry)
        for r in range(i + 1):
            piece = blocks([qi * sub + r], piece, q_rows, triangular=r == i)
        pieces.append(piece)
    carry = tuple((None, jnp.concatenate([piece[j][1] for piece in pieces], axis=0)) for j in range(heads))
    out = None
    for own, (_, acc) in zip(owns, carry):
        o = acc / pltpu.roll(acc, head_dim, axis=1)
        out = o if out is None else jnp.where(own, o, out)
    o_ref[...] = out.astype(BF16)


def _fox_attention(qb, ktb, vtb, cb, *, head_dim, tile, kblock):
    t, a_width = qb.shape
    batch, _, seq_len = ktb.shape
    nq = seq_len // tile
    q_spec = pl.BlockSpec((tile, LANES), lambda b, hp, qi: (b * nq + qi, hp))
    kv_spec = pl.BlockSpec((None, LANES, seq_len), lambda b, hp, qi: (b, hp, 0))
    return pl.pallas_call(
        functools.partial(_fox_kernel, tile=tile, kblock=kblock, head_dim=head_dim),
        grid=(batch, a_width // LANES, nq),
        in_specs=[q_spec, kv_spec, kv_spec, kv_spec],
        out_specs=q_spec,
        out_shape=jax.ShapeDtypeStruct((t, a_width), BF16),
        compiler_params=_params("arbitrary", "arbitrary", "arbitrary"),
        name="fox_attention",
    )(qb, ktb, vtb, cb)


def _paged_attn_kernel(pt_ref, q_ref, knew_ref, vnew_ref, lfnew_ref, tri_ref, kpool_ref, vpool_ref, lfpool_ref,
                       o_ref, qbd_ref, m_ref, l_ref, acc_ref, pref_ref, kbuf, vbuf, lfbuf, sems,
                       *, n_pages, n_heads, head_dim):
    step = pl.program_id(1)
    steps = pl.num_programs(1)
    lin = pl.program_id(0) * steps + step
    total = pl.num_programs(0) * steps
    n_slots = kbuf.shape[0]
    nq, width = q_ref.shape
    rows = nq * n_heads
    page = kbuf.shape[3]

    def gather(lin_step, slot, pages=None):
        out = []
        for j in range(n_pages):
            pg = 0 if pages is None else pages[j]
            out += [pltpu.make_async_copy(kpool_ref.at[pg], kbuf.at[slot, j], sems.at[0, slot]),
                    pltpu.make_async_copy(vpool_ref.at[pg], vbuf.at[slot, j], sems.at[1, slot]),
                    pltpu.make_async_copy(lfpool_ref.at[pg], lfbuf.at[slot, j], sems.at[2, slot])]
        return out

    def start_gather(lin_step, slot):
        bi = lin_step // steps
        p0 = (lin_step % steps) * n_pages
        for c in gather(lin_step, slot, [pt_ref[bi, p0 + j] for j in range(n_pages)]):
            c.start()

    @pl.when(lin == 0)
    def _():
        for d in range(n_slots - 1):
            start_gather(d, d)

    @pl.when(lin + (n_slots - 1) < total)
    def _():
        start_gather(lin + (n_slots - 1), (lin + (n_slots - 1)) % n_slots)

    slot = lin % n_slots
    for c in gather(lin, slot):
        c.wait()
    k_refs = [kbuf.at[slot, j] for j in range(n_pages)]
    v_refs = [vbuf.at[slot, j] for j in range(n_pages)]
    lf_refs = [lfbuf.at[slot, j] for j in range(n_pages)]
    head_of_lane = lax.broadcasted_iota(jnp.int32, (n_heads, width), 1) // head_dim
    own = head_of_lane == lax.broadcasted_iota(jnp.int32, (n_heads, width), 0)

    @pl.when(step == 0)
    def _():
        q = q_ref[...]
        for qq in range(nq):
            blk = jnp.where(own, jnp.broadcast_to(q[qq:qq + 1, :], (n_heads, width)), 0.0)
            qbd_ref[pl.ds(qq * n_heads, n_heads), :] = blk
        m_ref[...] = jnp.full(m_ref.shape, -jnp.inf, F32)
        l_ref[...] = jnp.zeros(l_ref.shape, F32)
        acc_ref[...] = jnp.zeros(acc_ref.shape, F32)
        pref_ref[...] = jnp.zeros(pref_ref.shape, F32)

    def attend(k_list, v_list, lf_list, causal):
        n = len(k_list)
        ys = [lf[...] for lf in lf_list]
        y = jnp.concatenate(ys, axis=0) if n > 1 else ys[0]
        pieces = jnp.concatenate(_split3(y), axis=0)
        cw = _dot(pieces, tri_ref[...])
        nh = n * n_heads
        cw = cw[:nh] + cw[nh:2 * nh] + cw[2 * nh:]
        qbd = qbd_ref[...].astype(BF16)
        pref = pref_ref[...]
        s_list = []
        for g in range(n):
            cg = cw[g * n_heads:(g + 1) * n_heads]
            c = cg + pref
            pref = pref + cg[:, page - 1:page]
            s = _dot(qbd, k_list[g][...].astype(BF16))
            s = s - jnp.concatenate([c] * nq, axis=0)
            if causal:
                key = lax.broadcasted_iota(jnp.int32, (rows, page), 1)
                qpos = lax.broadcasted_iota(jnp.int32, (rows, page), 0) // n_heads
                s = jnp.where(key <= qpos, s, -jnp.inf)
            s_list.append(s)
        pref_ref[...] = pref
        m, l, acc = m_ref[...], l_ref[...], acc_ref[...]
        group = max(n // PAGE_GROUPS, 1)
        for g0 in range(0, n, group):
            s_grp = jnp.concatenate(s_list[g0:g0 + group], axis=1) if group > 1 else s_list[g0]
            m_new = jnp.maximum(m, jnp.max(s_grp, axis=1, keepdims=True))
            alpha = jnp.exp(m - m_new)
            p = jnp.exp(s_grp - m_new)
            l = alpha * l + jnp.sum(p, axis=1, keepdims=True)
            pv = None
            for g in range(group):
                part = _dot_nt(p[:, g * page:(g + 1) * page].astype(BF16), v_list[g0 + g][...].astype(BF16))
                pv = part if pv is None else pv + part
            acc = alpha * acc + pv
            m = m_new
        m_ref[...], l_ref[...], acc_ref[...] = m, l, acc

    attend(k_refs, v_refs, lf_refs, causal=False)

    @pl.when(step == pl.num_programs(1) - 1)
    def _():
        attend([knew_ref], [vnew_ref], [lfnew_ref], causal=True)
        o = acc_ref[...] / l_ref[...]
        out_rows = []
        for qq in range(nq):
            blk = jnp.where(own, o[qq * n_heads:(qq + 1) * n_heads, :], 0.0)
            out_rows.append(jnp.sum(blk, axis=0, keepdims=True))
        o_ref[...] = jnp.concatenate(out_rows, axis=0).astype(o_ref.dtype)


def _paged_attention(q, kt_new, vt_new, lf_new, kt_pool, vt_pool, lf_pool, page_table, *, head_dim):
    batch, nq, width = q.shape
    page = kt_pool.shape[2]
    n_heads = lf_pool.shape[1]
    n_pages = PAGES_PER_STEP
    steps = page_table.shape[1] // n_pages
    rows = nq * n_heads
    tri = (jnp.arange(page)[:, None] <= jnp.arange(page)[None, :]).astype(BF16)

    assert batch * steps >= PAGE_SLOTS - 1
    per_batch = lambda shape: pl.BlockSpec((None,) + shape, lambda b, s, pt: (b, 0, 0))
    in_hbm = pl.BlockSpec(memory_space=pl.ANY)
    grid_spec = pltpu.PrefetchScalarGridSpec(
        num_scalar_prefetch=1,
        grid=(batch, steps),
        in_specs=[per_batch((nq, width)), per_batch((width, page)), per_batch((width, page)),
                  per_batch((n_heads, page)), pl.BlockSpec(tri.shape, lambda b, s, pt: (0, 0)),
                  in_hbm, in_hbm, in_hbm],
        out_specs=per_batch((nq, width)),
        scratch_shapes=[pltpu.VMEM((rows, width), F32), pltpu.VMEM((rows, 1), F32),
                        pltpu.VMEM((rows, 1), F32), pltpu.VMEM((rows, width), F32),
                        pltpu.VMEM((n_heads, 1), F32),
                        pltpu.VMEM((PAGE_SLOTS, n_pages, width, page), F32),
                        pltpu.VMEM((PAGE_SLOTS, n_pages, width, page), F32),
                        pltpu.VMEM((PAGE_SLOTS, n_pages, n_heads, page), F32),
                        pltpu.SemaphoreType.DMA((3, PAGE_SLOTS))],
    )
    return pl.pallas_call(
        functools.partial(_paged_attn_kernel, n_pages=n_pages, n_heads=n_heads, head_dim=head_dim),
        grid_spec=grid_spec,
        out_shape=jax.ShapeDtypeStruct((batch, nq, width), BF16),
        compiler_params=_params("arbitrary", "arbitrary"),
        name="paged_attention",
    )(page_table, q, kt_new, vt_new, lf_new, tri, kt_pool, vt_pool, lf_pool)


def _c_proj_kernel(*refs, n_heads, key_dim, seq_len, tiles_per_seq, n_over):
    x_ref, g_ref, wqkv_ref, wz_ref, wba_ref, taps_ref, alog_ref, dtb_ref = refs[:8]
    over = list(refs[8:8 + n_over]) if n_over else None
    (q_ref, k_ref, v_ref, z_ref, beta_ref, gdec_ref, u_ref, ext_ref) = refs[8 + n_over:8 + n_over + 8]
    y_ref = None if over else refs[-1]
    qk_width = n_heads * key_dim
    conv_width = wqkv_ref.shape[1]

    h = _rms(x_ref[...], g_ref[...]).astype(BF16)
    tm = h.shape[0]
    first = pl.program_id(0) % tiles_per_seq == 0 if over is None else pl.program_id(0) == 0
    _conv_prologue(ext_ref, first, conv_width)

    cw = CONV_CHUNK if over is None else conv_width
    n_conv_chunks = conv_width // cw
    v_width = wz_ref.shape[1]
    zw = -(-v_width // (n_conv_chunks * LANES)) * LANES
    assert v_width % zw == 0
    nxt = _dot(h, wqkv_ref[:, :cw])
    for c0 in range(0, conv_width, cw):
        cur = nxt
        if c0 + cw < conv_width:
            nxt = _dot(h, wqkv_ref[:, c0 + cw:c0 + 2 * cw])
        z0 = (c0 // cw) * zw
        if z0 < v_width:
            z_ref[:, z0:z0 + zw] = _dot(h, wz_ref[:, z0:z0 + zw]).astype(BF16)
        y = _silu(_causal_conv(cur, taps_ref, ext_ref, over, seq_len, y_ref, c0))
        if over is None:
            tail = cur[tm - SUBLANES:, :]
            u_ref[:, c0:c0 + cw] = tail
            _conv_carry(ext_ref, tail, c0)
        else:
            u_ref[...] = cur
        for b0 in range(0, cw, key_dim):
            col = c0 + b0
            blk = y[:, b0:b0 + key_dim]
            if col < 2 * qk_width:
                blk = blk * lax.rsqrt(jnp.sum(blk * blk, axis=-1, keepdims=True) + NORM_EPS)
            if col < qk_width:
                q_ref[:, col:col + key_dim] = (blk * key_dim ** -0.5).astype(BF16)
            elif col < 2 * qk_width:
                k_ref[:, col - qk_width:col - qk_width + key_dim] = blk.astype(BF16)
            else:
                v_ref[:, col - 2 * qk_width:col - 2 * qk_width + key_dim] = blk.astype(BF16)

    ba = _dot(h, wba_ref[...])
    beta_ref[...] = jax.nn.sigmoid(ba)
    gdec_ref[...] = -jnp.exp(alog_ref[...]) * _softplus(ba + dtb_ref[...])


def _c_proj(x, gain, wqkv, wz, wba, taps, alog, dtb, overrides, *, n_heads, key_dim, seq_len, tm):
    t, d = x.shape
    conv_width = wqkv.shape[1]
    qk_width = n_heads * key_dim
    v_width = conv_width - 2 * qk_width
    n_tiles = t // tm
    n_over = 0 if overrides is None else len(overrides)
    row = lambda c: pl.BlockSpec((tm, c), lambda i: (i, 0))
    in_specs = [row(d)] + [_const_spec(a) for a in (gain, wqkv, wz, wba, taps, alog, dtb)]
    in_specs += [row(conv_width)] * n_over
    if overrides is None:
        u_spec = pl.BlockSpec((None, SUBLANES, conv_width), lambda i: (i, 0, 0))
        u_shape = jax.ShapeDtypeStruct((n_tiles, SUBLANES, conv_width), F32)
    else:
        u_spec = row(conv_width)
        u_shape = jax.ShapeDtypeStruct((t, conv_width), F32)
    out_specs = [row(qk_width), row(qk_width), row(v_width), row(v_width), row(LANES), row(LANES), u_spec]
    out_shape = [jax.ShapeDtypeStruct((t, qk_width), BF16), jax.ShapeDtypeStruct((t, qk_width), BF16),
                 jax.ShapeDtypeStruct((t, v_width), BF16), jax.ShapeDtypeStruct((t, v_width), BF16),
                 jax.ShapeDtypeStruct((t, LANES), F32), jax.ShapeDtypeStruct((t, LANES), F32), u_shape]
    return pl.pallas_call(
        functools.partial(_c_proj_kernel, n_heads=n_heads, key_dim=key_dim, seq_len=seq_len,
                          tiles_per_seq=max(seq_len // tm, 1), n_over=n_over),
        grid=(n_tiles,),
        in_specs=in_specs,
        out_specs=out_specs,
        out_shape=out_shape,
        scratch_shapes=_conv_scratch(tm, conv_width, overrides is None),
        compiler_params=_params("arbitrary"),
        name="c_proj",
    )(x, gain, wqkv, wz, wba, taps, alog, dtb, *(overrides or []))


def _delta_kernel(q_ref, k_ref, v_ref, z_ref, gcol_ref, beta_ref, grow_ref, s0_ref, nw_ref,
                  o_ref, s_out_ref, m_ref, *, n_heads, chunk):
    step = pl.program_id(1)
    n_batch = s0_ref.shape[0]
    rows = q_ref.shape[0] // n_batch
    dk = q_ref.shape[1] // n_heads
    dv = v_ref.shape[1] // n_heads
    n_chunks = rows // chunk
    seqs = [(b, h) for b in range(n_batch) for h in range(n_heads)]
    ri = lax.broadcasted_iota(jnp.int32, (chunk, chunk), 0)
    ci = lax.broadcasted_iota(jnp.int32, (chunk, chunk), 1)
    incl = (ci <= ri)[None]
    strict = (ci < ri)[None]

    @pl.when(step == 0)
    def _():
        for i, (b, h) in enumerate(seqs):
            m_ref[i] = s0_ref[b, h].T

    probs = [(c, b, h) for c in range(n_chunks) for b, h in seqs]
    rs = lambda c, b: slice(b * rows + c * chunk, b * rows + (c + 1) * chunk)
    stack = lambda f: jnp.stack([f(c, b, h) for c, b, h in probs])
    kb = stack(lambda c, b, h: k_ref[rs(c, b), h * dk:(h + 1) * dk])
    qb = stack(lambda c, b, h: q_ref[rs(c, b), h * dk:(h + 1) * dk])
    vf = stack(lambda c, b, h: v_ref[rs(c, b), h * dv:(h + 1) * dv]).astype(F32)
    gc = stack(lambda c, b, h: gcol_ref[rs(c, b), h:h + 1])
    gr = stack(lambda c, b, h: grow_ref[b, h:h + 1, rs(c, 0)])
    bc = stack(lambda c, b, h: beta_ref[rs(c, b), h:h + 1])
    kf = kb.astype(F32)
    eg = jnp.exp(gc)
    decay = jnp.where(incl, jnp.exp(jnp.where(incl, gc - gr, 0.0)), 0.0)
    kqk = _bdot_nt(jnp.concatenate([kb, qb], axis=1), kb)
    a = jnp.where(strict, bc * kqk[:, :chunk] * decay, 0.0)
    nil = -a
    pw = a
    span = 2
    while span < chunk:
        pwb = pw.astype(BF16)
        pw = _bdot(pwb, pwb)
        nil = nil + pw + _bdot(nil.astype(BF16), pw.astype(BF16))
        span *= 2
    rhs = jnp.concatenate([bc * vf, (bc * eg) * kf], axis=2)
    w = rhs + _bdot(nil.astype(BF16), rhs.astype(BF16))
    w_v = w[:, :, :dv]
    p = (kqk[:, chunk:] * decay).astype(BF16)
    g_last = gc[:, chunk - 1:chunk, :]
    lhs = jnp.concatenate([w[:, :, dv:], eg * qb.astype(F32)], axis=1).astype(BF16)
    k_dec = (jnp.exp(g_last - gc) * kf).astype(BF16)
    g_chunk = jnp.exp(g_last)

    for c in range(n_chunks):
        hs = slice(c * len(seqs), (c + 1) * len(seqs))
        m = m_ref[...]
        r = _bdot(lhs[hs], m.astype(BF16))
        ub = (w_v[hs] - r[:, :chunk]).astype(BF16)
        o = r[:, chunk:] + _bdot(p[hs], ub)
        m_ref[...] = g_chunk[hs] * m + _bdot_tn(k_dec[hs], ub)
        on = o * lax.rsqrt(jnp.mean(o * o, axis=-1, keepdims=True) + NORM_EPS) * nw_ref[...]
        for i, (b, h) in enumerate(seqs):
            vl = slice(h * dv, (h + 1) * dv)
            o_ref[rs(c, b), vl] = (on[i] * _silu(z_ref[rs(c, b), vl].astype(F32))).astype(BF16)

    @pl.when(step == pl.num_programs(1) - 1)
    def _():
        for i, (b, h) in enumerate(seqs):
            s_out_ref[b, h] = m_ref[i].T


def _delta_rule(q, k, v, z, gcol, beta, grow, s0, norm_w, *, batch, seq_len, n_heads, chunk, rows, n_batch):
    t = q.shape[0]
    steps = seq_len // rows
    dk = q.shape[1] // n_heads
    dv = v.shape[1] // n_heads
    assert n_batch == 1 or steps == 1
    row = lambda c: pl.BlockSpec((n_batch * rows, c), lambda b, s: (b * steps + s, 0))
    state = pl.BlockSpec((n_batch, n_heads, dv, dk), lambda b, s: (b, 0, 0, 0))
    return pl.pallas_call(
        functools.partial(_delta_kernel, n_heads=n_heads, chunk=chunk),
        grid=(batch // n_batch, steps),
        in_specs=[row(q.shape[1]), row(k.shape[1]), row(v.shape[1]), row(z.shape[1]),
                  row(gcol.shape[1]), row(beta.shape[1]),
                  pl.BlockSpec((n_batch, n_heads, rows), lambda b, s: (b, 0, s)),
                  state, pl.BlockSpec(norm_w.shape, lambda b, s: (0, 0))],
        out_specs=[row(v.shape[1]), state],
        out_shape=[jax.ShapeDtypeStruct((t, v.shape[1]), BF16),
                   jax.ShapeDtypeStruct((batch, n_heads, dv, dk), F32)],
        scratch_shapes=[pltpu.VMEM((n_batch * n_heads, dk, dv), F32)],
        compiler_params=_params("arbitrary", "arbitrary"),
        name="gated_delta_rule",
    )(q, k, v, z, gcol, beta, grow, s0, norm_w)


def _pad_lanes(a):
    return jnp.pad(a, ((0, 0), (0, LANES - a.shape[1])))


def _rows_to_heads(a, batch, seq_len):
    return a.reshape(batch, seq_len, a.shape[1]).transpose(0, 2, 1)


def _heads_to_rows(a):
    b, h, length = a.shape
    return a.transpose(0, 2, 1).reshape(b * length, h)


def _feature_major_pool(pool):
    n_l, n_p, page = pool.shape[:3]
    perm = (0, 1) + tuple(range(3, pool.ndim)) + (2,)
    return pool.transpose(perm).reshape(n_l * n_p, -1, page)


def _mixer_ab(x, past, w, i, gain, *, batch, seq_len, tm):
    t = x.shape[0]
    hd = w["a_head_dim"]
    nh = w["ab_b_f"].shape[1]
    aw = nh * hd
    w_in = w["ab_w_in"][i]
    bw = (w_in.shape[1] - 3 * aw - nh) // 3
    wb = w_in[:, 3 * aw + nh:].astype(BF16)
    wf = w_in[:, 3 * aw:3 * aw + nh]
    taps = w["ab_conv_w"][i]
    n_keep = taps.shape[0] - 1
    if past is None:
        wq = w_in[:, :aw].astype(BF16)
        wkvt = w_in[:, aw:3 * aw].T.astype(BF16)
        kt, vt, ktb, vtb, lft, qb, ob, tails = _ab_proj_prompt(
            x, gain, wq, wkvt, wf.T.astype(BF16), w["ab_b_f"][i][:, None], wb, taps,
            batch=batch, seq_len=seq_len, head_dim=hd, tm=tm)
        o_a = _fox_attention(qb, ktb, vtb, _forget_bias_rows(lft, hd), head_dim=hd,
                             tile=min(1024, seq_len), kblock=min(512, seq_len))
        tails = tails.reshape(batch, seq_len // tm, SUBLANES, bw)[:, -1]
        new_buf = tails[:, SUBLANES - n_keep:, :]
        to_tokens = lambda a: a.reshape(batch, nh, hd, seq_len).transpose(0, 3, 1, 2)
        state = (to_tokens(kt), to_tokens(vt), lft.transpose(0, 2, 1), new_buf)
    else:
        overrides = _conv_overrides(past["state_b_conv"][i], seq_len)
        q_s, k32, v32, logf, ob, gcu = _ab_proj_sample(
            x, gain, w_in[:, :3 * aw].astype(BF16), _pad_lanes(wf).astype(BF16),
            _pad_lanes(w["ab_b_f"][i][None, :]), wb, taps, overrides,
            n_heads=nh, head_dim=hd, seq_len=seq_len)
        n_phys, page = past["cache_a_k"].shape[1:3]
        pad_keys = lambda a: jnp.pad(a.reshape(batch, seq_len, -1).transpose(0, 2, 1),
                                     ((0, 0), (0, 0), (0, page - seq_len)))
        o_a = _paged_attention(
            q_s.reshape(batch, seq_len, aw), pad_keys(k32), pad_keys(v32), pad_keys(logf),
            _feature_major_pool(past["cache_a_k"]), _feature_major_pool(past["cache_a_v"]),
            _feature_major_pool(past["cache_a_logf"]), past["page_table"] + i * n_phys, head_dim=hd)
        o_a = o_a.reshape(t, aw)
        new_buf = gcu.reshape(batch, seq_len, bw)[:, seq_len - n_keep:, :]
        state = (k32.reshape(batch, seq_len, nh, hd), v32.reshape(batch, seq_len, nh, hd),
                 logf.reshape(batch, seq_len, nh), new_buf)
    return [o_a, ob], w["ab_w_out"][i].astype(BF16), state


def _mixer_c(x, past, w, i, gain, *, batch, seq_len, tm):
    t = x.shape[0]
    nh = w["c_a_log"].shape[1]
    dv = w["c_norm_w"].shape[1]
    w_in = w["c_w_in"][i]
    v_width = nh * dv
    conv_width = w_in.shape[1] - v_width - 2 * nh
    dk = (conv_width - v_width) // 2 // nh
    wqkv = w_in[:, :conv_width].astype(BF16)
    wz = w_in[:, conv_width:conv_width + v_width].astype(BF16)
    wba = _pad_lanes(w_in[:, conv_width + v_width:]).astype(BF16)
    zeros_h = jnp.zeros((1, nh), F32)
    alog = _pad_lanes(jnp.concatenate([zeros_h, w["c_a_log"][i][None, :]], axis=1))
    dtb = _pad_lanes(jnp.concatenate([zeros_h, w["c_dt_bias"][i][None, :]], axis=1))
    taps = w["c_conv_w"][i]
    n_keep = taps.shape[0] - 1
    tm_c = min(tm, 512)
    if past is None:
        overrides = None
        s0 = jnp.zeros((batch, nh, dv, dk), F32)
    else:
        overrides = _conv_overrides(past["state_c_conv"][i], seq_len)
        s0 = past["state_c_S"][i]
    qn, kn, vn, zb, beta, gdec, u_out = _c_proj(
        x, gain, wqkv, wz, wba, taps, alog, dtb, overrides,
        n_heads=nh, key_dim=dk, seq_len=seq_len, tm=tm_c)
    if past is None:
        tails = u_out.reshape(batch, seq_len // tm_c, SUBLANES, conv_width)[:, -1]
        new_buf = tails[:, SUBLANES - n_keep:, :]
        len_pad = seq_len
    else:
        new_buf = u_out.reshape(batch, seq_len, conv_width)[:, seq_len - n_keep:, :]
        len_pad = -(-seq_len // DELTA_CHUNK) * DELTA_CHUNK
        pad3 = lambda a: jnp.pad(a.reshape(batch, seq_len, a.shape[1]),
                                 ((0, 0), (0, len_pad - seq_len), (0, 0))).reshape(batch * len_pad, a.shape[1])
        qn, kn, vn, zb, beta, gdec = [pad3(a) for a in (qn, kn, vn, zb, beta, gdec)]
    n_batch = DELTA_BATCH if len_pad <= DELTA_ROWS and batch % DELTA_BATCH == 0 else 1
    scan_len = -(-len_pad // LANES) * LANES
    g_heads = jnp.pad(_rows_to_heads(gdec[:, nh:2 * nh], batch, len_pad), ((0, 0), (0, 0), (0, scan_len - len_pad)))
    grow = _segment_cumsum(g_heads, DELTA_CHUNK)[:, :, :len_pad]
    og, s_new = _delta_rule(qn, kn, vn, zb, _pad_lanes(_heads_to_rows(grow)), beta, grow, s0, w["c_norm_w"][i][None, :],
                            batch=batch, seq_len=len_pad, n_heads=nh, chunk=DELTA_CHUNK,
                            rows=min(DELTA_ROWS, len_pad), n_batch=n_batch)
    if len_pad != seq_len:
        og = og.reshape(batch, len_pad, v_width)[:, :seq_len].reshape(t, v_width)
    return [og], w["c_w_out"][i].astype(BF16), (new_buf, s_new)


def _run_trunk(x3, past, w, *, tm):
    batch, seq_len, d = x3.shape
    x = x3.reshape(batch * seq_len, d)
    ab_states, c_states = [], []
    ffn = lambda x, g0, g1, layer, j, mixer=None: _ffn_block(
        x, g0, g1, w["ffn_w_gate"], w["ffn_w_up"], w["ffn_w_down"], layer, j, tm=tm, mixer=mixer)
    for layer in range(w["norm_g"].shape[0]):
        gains = [w["norm_g"][layer, j][None, :] for j in range(6)]
        x = ffn(x, gains[0], gains[1], layer, 0)
        mixer, states = (_mixer_ab, ab_states) if layer % 2 == 0 else (_mixer_c, c_states)
        parts, w_out, state = mixer(x, past, w, layer // 2, gains[2], batch=batch, seq_len=seq_len, tm=tm)
        states.append(state)
        x = ffn(x, gains[4], gains[5], layer, 1, mixer=(parts, w_out, gains[3]))
    ab_new = [jnp.stack(ts) for ts in zip(*ab_states)]
    c_new = [jnp.stack(ts) for ts in zip(*c_states)]
    return x.reshape(batch, seq_len, d), ab_new, c_new


def kernel(x_prompt, x_sample, cache_a_k, cache_a_v, cache_a_logf, page_table, state_b_conv, state_c_conv, state_c_S, norm_g, ffn_w_gate, ffn_w_up, ffn_w_down, ab_w_in, ab_b_f, ab_conv_w, ab_w_out, c_w_in, c_conv_w, c_a_log, c_dt_bias, c_norm_w, c_w_out):
    w = dict(norm_g=norm_g,
             ffn_w_gate=ffn_w_gate, ffn_w_up=ffn_w_up, ffn_w_down=ffn_w_down,
             ab_w_in=ab_w_in, ab_b_f=ab_b_f, ab_conv_w=ab_conv_w, ab_w_out=ab_w_out,
             c_w_in=c_w_in, c_conv_w=c_conv_w, c_a_log=c_a_log, c_dt_bias=c_dt_bias,
             c_norm_w=c_norm_w, c_w_out=c_w_out, a_head_dim=cache_a_k.shape[-1])
    past = dict(cache_a_k=cache_a_k, cache_a_v=cache_a_v, cache_a_logf=cache_a_logf, page_table=page_table,
                state_b_conv=state_b_conv, state_c_conv=state_c_conv, state_c_S=state_c_S)
    prompt_rows = x_prompt.shape[0] * x_prompt.shape[1]
    sample_rows = x_sample.shape[0] * x_sample.shape[1]
    y_prompt, ab_p, c_p = _run_trunk(x_prompt, None, w, tm=min(512, prompt_rows))
    y_sample, ab_s, c_s = _run_trunk(x_sample, past, w, tm=min(512, sample_rows))
    a_k_prompt, a_v_prompt, a_logf_prompt, b_conv_prompt = ab_p
    a_k_sample, a_v_sample, a_logf_sample, b_conv_sample = ab_s
    c_conv_prompt, c_S_prompt = c_p
    c_conv_sample, c_S_sample = c_s
    return (y_prompt, y_sample,
            a_k_prompt, a_v_prompt, a_logf_prompt,
            a_k_sample, a_v_sample, a_logf_sample,
            b_conv_prompt, b_conv_sample,
            c_conv_prompt, c_conv_sample,
            c_S_prompt, c_S_sample)
```

```python
import functools

import jax
import jax.numpy as jnp
from jax import lax
from jax.experimental import pallas as pl
from jax.experimental.pallas import tpu as pltpu

F32 = jnp.float32
BF16 = jnp.bfloat16
NORM_EPS = 1e-6
LOG2_E = 1.4426950408889634

LANES = 128
SUBLANES = 8
VMEM_LIMIT_BYTES = 56 * 1024 * 1024
DELTA_CHUNK = 64
DELTA_ROWS = 512
DELTA_BATCH = 4
CONV_ROW_STRIDE = 4
CONV_CHUNK = 512
PAGES_PER_STEP = 16
PAGE_GROUPS = 2
PAGE_SLOTS = 3


def _params(*semantics):
    return pltpu.CompilerParams(dimension_semantics=semantics, vmem_limit_bytes=VMEM_LIMIT_BYTES)


def _resident(shape, index_map):
    return pl.BlockSpec(shape, index_map, pipeline_mode=pl.Buffered(1))


def _const_spec(a):
    return _resident(a.shape, lambda *_: (0,) * a.ndim)


def _rms(x, g):
    return x * lax.rsqrt(jnp.mean(x * x, axis=-1, keepdims=True) + NORM_EPS) * g


def _softplus(x):
    return jnp.maximum(x, 0.0) + jnp.log1p(jnp.exp(-jnp.abs(x)))


def _silu(x):
    return x * jax.nn.sigmoid(x)


def _dot(a, b):
    return jnp.dot(a, b, preferred_element_type=F32)


def _dot_nt(a, b):
    return lax.dot_general(a, b, (((1,), (1,)), ((), ())), preferred_element_type=F32)


def _dot_tn(a, b):
    return lax.dot_general(a, b, (((0,), (0,)), ((), ())), preferred_element_type=F32)


def _bdot(a, b):
    return lax.dot_general(a, b, (((2,), (1,)), ((0,), (0,))), preferred_element_type=F32)


def _bdot_nt(a, b):
    return lax.dot_general(a, b, (((2,), (2,)), ((0,), (0,))), preferred_element_type=F32)


def _bdot_tn(a, b):
    return lax.dot_general(a, b, (((1,), (1,)), ((0,), (0,))), preferred_element_type=F32)


def _mix_rows(part_refs, w_ref):
    mix = None
    k0 = 0
    for a_ref in part_refs:
        kw = a_ref.shape[1]
        part = _dot(a_ref[...], w_ref[k0:k0 + kw, :])
        mix = part if mix is None else mix + part
        k0 += kw
    return mix


def _ffn_kernel(*refs, n_parts, ff_chunk):
    part_refs = refs[:n_parts]
    if n_parts:
        w_mix_ref, g_mix_ref = refs[n_parts:n_parts + 2]
        refs = refs[n_parts + 2:]
    x_ref, gpre_ref, gpost_ref, wg_ref, wu_ref, wd_ref, o_ref = refs
    x = x_ref[...]
    if n_parts:
        x = x + _rms(_mix_rows(part_refs, w_mix_ref), g_mix_ref[...])
    h = _rms(x, gpre_ref[...]).astype(BF16)
    d_ff = wg_ref.shape[1]
    proj = lambda c0: (_dot(h, wg_ref[:, c0:c0 + ff_chunk].astype(BF16)),
                       _dot(h, wu_ref[:, c0:c0 + ff_chunk].astype(BF16)))
    y = None
    nxt = proj(0)
    for c0 in range(0, d_ff, ff_chunk):
        gate, up = nxt
        if c0 + ff_chunk < d_ff:
            nxt = proj(c0 + ff_chunk)
        act = (_silu(gate) * up).astype(BF16)
        part = _dot(act, wd_ref[c0:c0 + ff_chunk, :].astype(BF16))
        y = part if y is None else y + part
    o_ref[...] = x + 0.5 * _rms(y, gpost_ref[...])


def _ffn_chunk(d_ff):
    for c in range(4 * LANES, 0, -LANES):
        if d_ff % c == 0:
            return c
    return d_ff


def _ffn_block(x, g_pre, g_post, wg_all, wu_all, wd_all, layer, j, *, tm, mixer=None):
    t, d = x.shape
    d_ff = wg_all.shape[-1]
    row = lambda c: pl.BlockSpec((tm, c), lambda i: (i, 0))
    gain = _resident((1, d), lambda i: (0, 0))
    w_in = _resident((None, None, d, d_ff), lambda i: (layer, j, 0, 0))
    w_out = _resident((None, None, d_ff, d), lambda i: (layer, j, 0, 0))
    parts, w_mix, g_mix = mixer if mixer is not None else ([], None, None)
    mix_args = list(parts) + ([w_mix, g_mix] if parts else [])
    mix_specs = [row(p.shape[1]) for p in parts] + ([_const_spec(w_mix), gain] if parts else [])
    return pl.pallas_call(
        functools.partial(_ffn_kernel, n_parts=len(parts), ff_chunk=_ffn_chunk(d_ff)),
        grid=(t // tm,),
        in_specs=mix_specs + [row(d), gain, gain, w_in, w_in, w_out],
        out_specs=row(d),
        out_shape=jax.ShapeDtypeStruct((t, d), F32),
        compiler_params=_params("arbitrary"),
        name="ffn_block",
    )(*mix_args, x, g_pre, g_post, wg_all, wu_all, wd_all)


def _causal_conv(u, taps_ref, ext_ref, overrides, seq_len, y_ref=None, col0=0):
    tm = u.shape[0]
    n_taps = taps_ref.shape[0]
    if overrides is None:
        stride = CONV_ROW_STRIDE
        group = SUBLANES * stride
        assert n_taps <= stride + 1 <= SUBLANES + 1
        blocks = range(col0 // LANES, (col0 + u.shape[1]) // LANES)
        for lb in blocks:
            lanes = slice(lb * LANES, (lb + 1) * LANES)
            ext_ref[lb, pl.ds(SUBLANES, tm), :] = u[:, lb * LANES - col0:(lb + 1) * LANES - col0]
            taps = [taps_ref[n_taps - 1 - j:n_taps - j, lanes] for j in range(n_taps)]
            for g0 in range(0, tm, group):
                win = {d: ext_ref[lb, pl.ds(SUBLANES + g0 + d, SUBLANES, stride=stride), :]
                       for d in range(1 - n_taps, stride)}
                for i in range(stride):
                    acc = win[i] * taps[0]
                    for j in range(1, n_taps):
                        acc = acc + win[i - j] * taps[j]
                    y_ref[lb, pl.ds(g0 + i, SUBLANES, stride=stride), :] = acc
        return jnp.concatenate([y_ref[lb] for lb in blocks], axis=1)
    ext_ref[pl.ds(SUBLANES, tm), :] = u
    out = u * taps_ref[n_taps - 1:n_taps, :]
    pos = lax.broadcasted_iota(jnp.int32, u.shape, 0) % seq_len
    for j in range(1, n_taps):
        prev = jnp.where(pos >= j, ext_ref[pl.ds(SUBLANES - j, tm), :], overrides[j - 1][...])
        out = out + prev * taps_ref[n_taps - 1 - j:n_taps - j, :]
    return out


def _conv_overrides(state, seq_len):
    b, wm1, c = state.shape
    assert seq_len >= wm1
    outs = []
    for j in range(1, wm1 + 1):
        rows = jnp.pad(state[:, wm1 - j:, :], ((0, 0), (0, seq_len - j), (0, 0)))
        outs.append(rows.reshape(b * seq_len, c))
    return outs


def _conv_carry(ext_ref, tail, col0=0):
    if len(ext_ref.shape) == 2:
        ext_ref[pl.ds(0, SUBLANES), :] = tail
    else:
        for lb in range(tail.shape[1] // LANES):
            ext_ref[col0 // LANES + lb, pl.ds(0, SUBLANES), :] = tail[:, lb * LANES:(lb + 1) * LANES]


def _conv_prologue(ext_ref, first, width):
    @pl.when(first)
    def _():
        _conv_carry(ext_ref, jnp.zeros((SUBLANES, width), F32))


def _conv_scratch(tm, width, blocked):
    if not blocked:
        return [pltpu.VMEM((SUBLANES + tm, width), F32)]
    assert tm % (SUBLANES * CONV_ROW_STRIDE) == 0 and width % LANES == 0
    return [pltpu.VMEM((width // LANES, SUBLANES + tm, LANES), F32), pltpu.VMEM((width // LANES, tm, LANES), F32)]


def _log_sigmoid(z):
    return -_softplus(-z)


def _ab_proj_prompt_kernel(x_ref, g_ref, wq_ref, wkvt_ref, wft_ref, bf_ref, wb_ref, taps_ref,
                           kt_ref, vt_ref, ktb_ref, vtb_ref, lft_ref, qb_ref, ob_ref, tail_ref, ext_ref, y_ref,
                           *, head_dim, tiles_per_seq):
    a_width = wq_ref.shape[1]
    b_width = wb_ref.shape[1] // 3
    h = _rms(x_ref[...], g_ref[...]).astype(BF16)
    qb_ref[...] = (_dot(h, wq_ref[...]) * (head_dim ** -0.5 * LOG2_E)).astype(BF16)
    kvt = _dot_nt(wkvt_ref[...], h)
    kt = kvt[:a_width]
    vt = kvt[a_width:]
    kt_ref[...] = kt
    vt_ref[...] = vt
    ktb_ref[...] = kt.astype(BF16)
    vtb_ref[...] = vt.astype(BF16)
    lft_ref[...] = _log_sigmoid(_dot_nt(wft_ref[...], h) + bf_ref[...])

    gbu = _dot(h, wb_ref[...])
    gate_b = gbu[:, :b_width]
    gcu = gbu[:, b_width:2 * b_width] * gbu[:, 2 * b_width:]
    _conv_prologue(ext_ref, pl.program_id(0) % tiles_per_seq == 0, b_width)
    y = _causal_conv(gcu, taps_ref, ext_ref, None, None, y_ref)
    ob_ref[...] = (gate_b * y).astype(BF16)
    tail = gcu[gcu.shape[0] - SUBLANES:, :]
    tail_ref[...] = tail
    _conv_carry(ext_ref, tail)


def _ab_proj_prompt(x, gain, wq, wkvt, wft, bf_col, wb, taps, *, batch, seq_len, head_dim, tm):
    t, d = x.shape
    a_width = wq.shape[1]
    n_heads = wft.shape[0]
    b_width = wb.shape[1] // 3
    tiles_per_seq = seq_len // tm
    row = lambda c: pl.BlockSpec((tm, c), lambda i: (i, 0))
    feat = lambda c: pl.BlockSpec((None, c, tm), lambda i: (i // tiles_per_seq, 0, i % tiles_per_seq))
    feat_shape = lambda c, dt: jax.ShapeDtypeStruct((batch, c, seq_len), dt)
    return pl.pallas_call(
        functools.partial(_ab_proj_prompt_kernel, head_dim=head_dim, tiles_per_seq=tiles_per_seq),
        grid=(t // tm,),
        in_specs=[row(d)] + [_const_spec(a) for a in (gain, wq, wkvt, wft, bf_col, wb, taps)],
        out_specs=[feat(a_width), feat(a_width), feat(a_width), feat(a_width), feat(n_heads),
                   row(a_width), row(b_width),
                   pl.BlockSpec((None, SUBLANES, b_width), lambda i: (i, 0, 0))],
        out_shape=[feat_shape(a_width, F32), feat_shape(a_width, F32), feat_shape(a_width, BF16),
                   feat_shape(a_width, BF16), feat_shape(n_heads, F32),
                   jax.ShapeDtypeStruct((t, a_width), BF16), jax.ShapeDtypeStruct((t, b_width), BF16),
                   jax.ShapeDtypeStruct((t // tm, SUBLANES, b_width), F32)],
        scratch_shapes=_conv_scratch(tm, b_width, True),
        compiler_params=_params("arbitrary"),
        name="ab_proj_prompt",
    )(x, gain, wq, wkvt, wft, bf_col, wb, taps)


def _ab_proj_sample_kernel(*refs, n_heads, head_dim, seq_len, n_over):
    x_ref, g_ref, wqkv_ref, wf_ref, bf_ref, wb_ref, taps_ref = refs[:7]
    over = list(refs[7:7 + n_over])
    q_ref, k_ref, v_ref, logf_ref, ob_ref, u_ref, ext_ref = refs[7 + n_over:]
    a_width = n_heads * head_dim
    b_width = wb_ref.shape[1] // 3
    h = _rms(x_ref[...], g_ref[...]).astype(BF16)
    qkv = _dot(h, wqkv_ref[...])
    q_ref[...] = qkv[:, :a_width] * head_dim ** -0.5
    k_ref[...] = qkv[:, a_width:2 * a_width]
    v_ref[...] = qkv[:, 2 * a_width:]
    logf_ref[...] = _log_sigmoid(_dot(h, wf_ref[...]) + bf_ref[...])[:, :n_heads]

    gbu = _dot(h, wb_ref[...])
    gate_b = gbu[:, :b_width]
    gcu = gbu[:, b_width:2 * b_width] * gbu[:, 2 * b_width:]
    _conv_prologue(ext_ref, pl.program_id(0) == 0, b_width)
    y = _causal_conv(gcu, taps_ref, ext_ref, over, seq_len)
    ob_ref[...] = (gate_b * y).astype(BF16)
    u_ref[...] = gcu


def _ab_proj_sample(x, gain, wqkv, wf, bf, wb, taps, overrides, *, n_heads, head_dim, seq_len):
    t, d = x.shape
    a_width = n_heads * head_dim
    b_width = wb.shape[1] // 3
    full = lambda c: pl.BlockSpec((t, c), lambda i: (0, 0))
    f32 = lambda c: jax.ShapeDtypeStruct((t, c), F32)
    return pl.pallas_call(
        functools.partial(_ab_proj_sample_kernel, n_heads=n_heads, head_dim=head_dim, seq_len=seq_len,
                          n_over=len(overrides)),
        grid=(1,),
        in_specs=[full(d)] + [_const_spec(a) for a in (gain, wqkv, wf, bf, wb, taps)]
        + [full(b_width)] * len(overrides),
        out_specs=[full(a_width), full(a_width), full(a_width), full(n_heads), full(b_width), full(b_width)],
        out_shape=[f32(a_width), f32(a_width), f32(a_width), f32(n_heads),
                   jax.ShapeDtypeStruct((t, b_width), BF16), f32(b_width)],
        scratch_shapes=[pltpu.VMEM((SUBLANES + t, b_width), F32)],
        compiler_params=_params("arbitrary"),
        name="ab_proj_sample",
    )(x, gain, wqkv, wf, bf, wb, taps, *overrides)


def _scan_kernel(x_ref, o_ref, *, seg):
    x = x_ref[...]
    pos = lax.broadcasted_iota(jnp.int32, x.shape, 1) % seg
    d = 1
    while d < seg:
        x = x + jnp.where(pos >= d, pltpu.roll(x, d, axis=1), 0.0)
        d *= 2
    o_ref[...] = x


def _segment_cumsum(x, seg):
    b, h, length = x.shape
    spec = pl.BlockSpec((b * h, length), lambda i: (0, 0))
    return pl.pallas_call(
        functools.partial(_scan_kernel, seg=seg),
        grid=(1,),
        in_specs=[spec],
        out_specs=spec,
        out_shape=jax.ShapeDtypeStruct((b * h, length), F32),
        compiler_params=_params("arbitrary"),
        name="segment_cumsum",
    )(x.reshape(b * h, length)).reshape(x.shape)


def _split3(x):
    hi = x.astype(BF16)
    r = x - hi.astype(F32)
    mid = r.astype(BF16)
    lo = (r - mid.astype(F32)).astype(BF16)
    return hi, mid, lo


def _forget_bias_kernel(lf_ref, o_ref, *, head_dim):
    x = lf_ref[...]
    n_heads, length = x.shape
    pos = lax.broadcasted_iota(jnp.int32, x.shape, 1)
    d = 1
    while d < length:
        x = x + jnp.where(pos >= d, pltpu.roll(x, d, axis=1), 0.0)
        d *= 2
    pieces = [p.astype(F32) for p in _split3(-LOG2_E * x)]
    pad = jnp.zeros((head_dim - len(pieces), length), F32)
    for h in range(n_heads):
        blk = jnp.concatenate([p[h:h + 1] for p in pieces] + [pad], axis=0)
        o_ref[pl.ds(h * head_dim, head_dim), :] = blk.astype(BF16)


def _forget_bias_rows(lft, head_dim):
    batch, n_heads, length = lft.shape
    return pl.pallas_call(
        functools.partial(_forget_bias_kernel, head_dim=head_dim),
        grid=(batch,),
        in_specs=[pl.BlockSpec((None, n_heads, length), lambda b: (b, 0, 0))],
        out_specs=pl.BlockSpec((None, n_heads * head_dim, length), lambda b: (b, 0, 0)),
        out_shape=jax.ShapeDtypeStruct((batch, n_heads * head_dim, length), BF16),
        compiler_params=_params("arbitrary"),
        name="forget_bias_rows",
    )(lft)


def _fox_kernel(q_ref, kt_ref, vt_ref, cb_ref, o_ref, *, tile, kblock, head_dim):
    qi = pl.program_id(2)
    heads = q_ref.shape[1] // head_dim
    sub = tile // kblock
    row = lax.broadcasted_iota(jnp.int32, (kblock, kblock), 0)
    col = lax.broadcasted_iota(jnp.int32, (kblock, kblock), 1)
    head_of_lane = lax.broadcasted_iota(jnp.int32, q_ref.shape, 1) // head_dim
    q_pair = q_ref[...]
    ones_rows = jnp.ones((head_dim, kblock), BF16)
    owns = [head_of_lane == j for j in range(heads)]
    q_augs = [jnp.where(own, q_pair, jnp.ones_like(q_pair)) for own in owns]

    def blocks(kis, carry, q_rows=slice(None), triangular=False):
        offs = [pl.multiple_of(ki * kblock, kblock) for ki in kis]
        feats = [slice(j * head_dim, (j + 1) * head_dim) for j in range(heads)]
        order = lambda j, data, aux: [data, aux] if j == 0 else [aux, data]
        scores = [[_dot(q_augs[j][q_rows], jnp.concatenate(
            order(j, kt_ref[feats[j], pl.ds(off, kblock)], cb_ref[feats[j], pl.ds(off, kblock)]), axis=0))
            for j in range(heads)] for off in offs]
        carry = list(carry)
        for off, score in zip(offs, scores):
            for j, (m, acc) in enumerate(carry):
                s = jnp.where(col <= row, score[j], -jnp.inf) if triangular else score[j]
                m_new = jnp.maximum(m, jnp.max(s, axis=1, keepdims=True))
                p = jnp.exp2(s - m_new).astype(BF16)
                vt_aug = jnp.concatenate(order(j, vt_ref[feats[j], pl.ds(off, kblock)], ones_rows), axis=0)
                carry[j] = (m_new, jnp.exp2(m - m_new) * acc + _dot_nt(p, vt_aug))
        return tuple(carry)

    init = tuple((jnp.full((tile, 1), -jnp.inf, F32), jnp.zeros(q_ref.shape, F32)) for _ in range(heads))
    carry = lax.fori_loop(0, qi, lambda i, c: blocks([i * sub + r for r in range(sub)], c), init)
    pieces = []
    for i in range(sub):
        q_rows = slice(i * kblock, (i + 1) * kblock)
        piece = tuple((m[q_rows], acc[q_rows]) for m, acc in carry)
        for r in range(i + 1):
            piece = blocks([qi * sub + r], piece, q_rows, triangular=r == i)
        pieces.append(piece)
    carry = tuple((None, jnp.concatenate([piece[j][1] for piece in pieces], axis=0)) for j in range(heads))
    out = None
    for own, (_, acc) in zip(owns, carry):
        o = acc / pltpu.roll(acc, head_dim, axis=1)
        out = o if out is None else jnp.where(own, o, out)
    o_ref[...] = out.astype(BF16)


def _fox_attention(qb, ktb, vtb, cb, *, head_dim, tile, kblock):
    t, a_width = qb.shape
    batch, _, seq_len = ktb.shape
    nq = seq_len // tile
    q_spec = pl.BlockSpec((tile, LANES), lambda b, hp, qi: (b * nq + qi, hp))
    kv_spec = pl.BlockSpec((None, LANES, seq_len), lambda b, hp, qi: (b, hp, 0))
    return pl.pallas_call(
        functools.partial(_fox_kernel, tile=tile, kblock=kblock, head_dim=head_dim),
        grid=(batch, a_width // LANES, nq),
        in_specs=[q_spec, kv_spec, kv_spec, kv_spec],
        out_specs=q_spec,
        out_shape=jax.ShapeDtypeStruct((t, a_width), BF16),
        compiler_params=_params("arbitrary", "arbitrary", "arbitrary"),
        name="fox_attention",
    )(qb, ktb, vtb, cb)


def _paged_attn_kernel(pt_ref, q_ref, knew_ref, vnew_ref, lfnew_ref, tri_ref, kpool_ref, vpool_ref, lfpool_ref,
                       o_ref, qbd_ref, m_ref, l_ref, acc_ref, pref_ref, kbuf, vbuf, lfbuf, sems,
                       *, n_pages, n_heads, head_dim):
    step = pl.program_id(1)
    steps = pl.num_programs(1)
    lin = pl.program_id(0) * steps + step
    total = pl.num_programs(0) * steps
    n_slots = kbuf.shape[0]
    nq, width = q_ref.shape
    rows = nq * n_heads
    page = kbuf.shape[3]

    def gather(lin_step, slot, pages=None):
        out = []
        for j in range(n_pages):
            pg = 0 if pages is None else pages[j]
            out += [pltpu.make_async_copy(kpool_ref.at[pg], kbuf.at[slot, j], sems.at[0, slot]),
                    pltpu.make_async_copy(vpool_ref.at[pg], vbuf.at[slot, j], sems.at[1, slot]),
                    pltpu.make_async_copy(lfpool_ref.at[pg], lfbuf.at[slot, j], sems.at[2, slot])]
        return out

    def start_gather(lin_step, slot):
        bi = lin_step // steps
        p0 = (lin_step % steps) * n_pages
        for n, c in enumerate(gather(lin_step, slot, [pt_ref[bi, p0 + j] for j in range(n_pages)])):
            c.start(priority=n % 2)

    @pl.when(lin == 0)
    def _():
        for d in range(n_slots - 1):
            start_gather(d, d)

    @pl.when(lin + (n_slots - 1) < total)
    def _():
        start_gather(lin + (n_slots - 1), (lin + (n_slots - 1)) % n_slots)

    slot = lin % n_slots
    for c in gather(lin, slot):
        c.wait()
    k_refs = [kbuf.at[slot, j] for j in range(n_pages)]
    v_refs = [vbuf.at[slot, j] for j in range(n_pages)]
    lf_refs = [lfbuf.at[slot, j] for j in range(n_pages)]
    head_of_lane = lax.broadcasted_iota(jnp.int32, (n_heads, width), 1) // head_dim
    own = head_of_lane == lax.broadcasted_iota(jnp.int32, (n_heads, width), 0)

    @pl.when(step == 0)
    def _():
        q = q_ref[...]
        for qq in range(nq):
            blk = jnp.where(own, jnp.broadcast_to(q[qq:qq + 1, :], (n_heads, width)), 0.0)
            qbd_ref[pl.ds(qq * n_heads, n_heads), :] = blk
        m_ref[...] = jnp.full(m_ref.shape, -jnp.inf, F32)
        l_ref[...] = jnp.zeros(l_ref.shape, F32)
        acc_ref[...] = jnp.zeros(acc_ref.shape, F32)
        pref_ref[...] = jnp.zeros(pref_ref.shape, F32)

    def attend(k_list, v_list, lf_list, causal):
        n = len(k_list)
        ys = [lf[...] for lf in lf_list]
        y = jnp.concatenate(ys, axis=0) if n > 1 else ys[0]
        pieces = jnp.concatenate(_split3(y), axis=0)
        cw = _dot(pieces, tri_ref[...])
        nh = n * n_heads
        cw = cw[:nh] + cw[nh:2 * nh] + cw[2 * nh:]
        qbd = qbd_ref[...].astype(BF16)
        pref = pref_ref[...]
        s_list = []
        for g in range(n):
            cg = cw[g * n_heads:(g + 1) * n_heads]
            c = cg + pref
            pref = pref + cg[:, page - 1:page]
            s = _dot(qbd, k_list[g][...].astype(BF16))
            s = s - jnp.concatenate([c] * nq, axis=0)
            if causal:
                key = lax.broadcasted_iota(jnp.int32, (rows, page), 1)
                qpos = lax.broadcasted_iota(jnp.int32, (rows, page), 0) // n_heads
                s = jnp.where(key <= qpos, s, -jnp.inf)
            s_list.append(s)
        pref_ref[...] = pref
        m, l, acc = m_ref[...], l_ref[...], acc_ref[...]
        group = max(n // PAGE_GROUPS, 1)
        for g0 in range(0, n, group):
            s_grp = jnp.concatenate(s_list[g0:g0 + group], axis=1) if group > 1 else s_list[g0]
            m_new = jnp.maximum(m, jnp.max(s_grp, axis=1, keepdims=True))
            alpha = jnp.exp(m - m_new)
            p = jnp.exp(s_grp - m_new)
            l = alpha * l + jnp.sum(p, axis=1, keepdims=True)
            pv = None
            for g in range(group):
                part = _dot_nt(p[:, g * page:(g + 1) * page].astype(BF16), v_list[g0 + g][...].astype(BF16))
                pv = part if pv is None else pv + part
            acc = alpha * acc + pv
            m = m_new
        m_ref[...], l_ref[...], acc_ref[...] = m, l, acc

    attend(k_refs, v_refs, lf_refs, causal=False)

    @pl.when(step == pl.num_programs(1) - 1)
    def _():
        attend([knew_ref], [vnew_ref], [lfnew_ref], causal=True)
        o = acc_ref[...] / l_ref[...]
        out_rows = []
        for qq in range(nq):
            blk = jnp.where(own, o[qq * n_heads:(qq + 1) * n_heads, :], 0.0)
            out_rows.append(jnp.sum(blk, axis=0, keepdims=True))
        o_ref[...] = jnp.concatenate(out_rows, axis=0).astype(o_ref.dtype)


def _paged_attention(q, kt_new, vt_new, lf_new, kt_pool, vt_pool, lf_pool, page_table, *, head_dim):
    batch, nq, width = q.shape
    page = kt_pool.shape[2]
    n_heads = lf_pool.shape[1]
    n_pages = PAGES_PER_STEP
    steps = page_table.shape[1] // n_pages
    rows = nq * n_heads
    tri = (jnp.arange(page)[:, None] <= jnp.arange(page)[None, :]).astype(BF16)

    assert batch * steps >= PAGE_SLOTS - 1
    per_batch = lambda shape: pl.BlockSpec((None,) + shape, lambda b, s, pt: (b, 0, 0))
    in_hbm = pl.BlockSpec(memory_space=pl.ANY)
    grid_spec = pltpu.PrefetchScalarGridSpec(
        num_scalar_prefetch=1,
        grid=(batch, steps),
        in_specs=[per_batch((nq, width)), per_batch((width, page)), per_batch((width, page)),
                  per_batch((n_heads, page)), pl.BlockSpec(tri.shape, lambda b, s, pt: (0, 0)),
                  in_hbm, in_hbm, in_hbm],
        out_specs=per_batch((nq, width)),
        scratch_shapes=[pltpu.VMEM((rows, width), F32), pltpu.VMEM((rows, 1), F32),
                        pltpu.VMEM((rows, 1), F32), pltpu.VMEM((rows, width), F32),
                        pltpu.VMEM((n_heads, 1), F32),
                        pltpu.VMEM((PAGE_SLOTS, n_pages, width, page), F32),
                        pltpu.VMEM((PAGE_SLOTS, n_pages, width, page), F32),
                        pltpu.VMEM((PAGE_SLOTS, n_pages, n_heads, page), F32),
                        pltpu.SemaphoreType.DMA((3, PAGE_SLOTS))],
    )
    return pl.pallas_call(
        functools.partial(_paged_attn_kernel, n_pages=n_pages, n_heads=n_heads, head_dim=head_dim),
        grid_spec=grid_spec,
        out_shape=jax.ShapeDtypeStruct((batch, nq, width), BF16),
        compiler_params=_params("arbitrary", "arbitrary"),
        name="paged_attention",
    )(page_table, q, kt_new, vt_new, lf_new, tri, kt_pool, vt_pool, lf_pool)


def _c_proj_kernel(*refs, n_heads, key_dim, seq_len, tiles_per_seq, n_over):
    x_ref, g_ref, wqkv_ref, wz_ref, wba_ref, taps_ref, alog_ref, dtb_ref = refs[:8]
    over = list(refs[8:8 + n_over]) if n_over else None
    (q_ref, k_ref, v_ref, z_ref, beta_ref, gdec_ref, u_ref, ext_ref) = refs[8 + n_over:8 + n_over + 8]
    y_ref = None if over else refs[-1]
    qk_width = n_heads * key_dim
    conv_width = wqkv_ref.shape[1]

    h = _rms(x_ref[...], g_ref[...]).astype(BF16)
    tm = h.shape[0]
    first = pl.program_id(0) % tiles_per_seq == 0 if over is None else pl.program_id(0) == 0
    _conv_prologue(ext_ref, first, conv_width)

    cw = CONV_CHUNK if over is None else conv_width
    n_conv_chunks = conv_width // cw
    v_width = wz_ref.shape[1]
    zw = -(-v_width // (n_conv_chunks * LANES)) * LANES
    assert v_width % zw == 0
    nxt = _dot(h, wqkv_ref[:, :cw])
    for c0 in range(0, conv_width, cw):
        cur = nxt
        if c0 + cw < conv_width:
            nxt = _dot(h, wqkv_ref[:, c0 + cw:c0 + 2 * cw])
        z0 = (c0 // cw) * zw
        if z0 < v_width:
            z_ref[:, z0:z0 + zw] = _dot(h, wz_ref[:, z0:z0 + zw]).astype(BF16)
        y = _silu(_causal_conv(cur, taps_ref, ext_ref, over, seq_len, y_ref, c0))
        if over is None:
            tail = cur[tm - SUBLANES:, :]
            u_ref[:, c0:c0 + cw] = tail
            _conv_carry(ext_ref, tail, c0)
        else:
            u_ref[...] = cur
        for b0 in range(0, cw, key_dim):
            col = c0 + b0
            blk = y[:, b0:b0 + key_dim]
            if col < 2 * qk_width:
                blk = blk * lax.rsqrt(jnp.sum(blk * blk, axis=-1, keepdims=True) + NORM_EPS)
            if col < qk_width:
                q_ref[:, col:col + key_dim] = (blk * key_dim ** -0.5).astype(BF16)
            elif col < 2 * qk_width:
                k_ref[:, col - qk_width:col - qk_width + key_dim] = blk.astype(BF16)
            else:
                v_ref[:, col - 2 * qk_width:col - 2 * qk_width + key_dim] = blk.astype(BF16)

    ba = _dot(h, wba_ref[...])
    beta_ref[...] = jax.nn.sigmoid(ba)
    gdec_ref[...] = -jnp.exp(alog_ref[...]) * _softplus(ba + dtb_ref[...])


def _c_proj(x, gain, wqkv, wz, wba, taps, alog, dtb, overrides, *, n_heads, key_dim, seq_len, tm):
    t, d = x.shape
    conv_width = wqkv.shape[1]
    qk_width = n_heads * key_dim
    v_width = conv_width - 2 * qk_width
    n_tiles = t // tm
    n_over = 0 if overrides is None else len(overrides)
    row = lambda c: pl.BlockSpec((tm, c), lambda i: (i, 0))
    in_specs = [row(d)] + [_const_spec(a) for a in (gain, wqkv, wz, wba, taps, alog, dtb)]
    in_specs += [row(conv_width)] * n_over
    if overrides is None:
        u_spec = pl.BlockSpec((None, SUBLANES, conv_width), lambda i: (i, 0, 0))
        u_shape = jax.ShapeDtypeStruct((n_tiles, SUBLANES, conv_width), F32)
    else:
        u_spec = row(conv_width)
        u_shape = jax.ShapeDtypeStruct((t, conv_width), F32)
    out_specs = [row(qk_width), row(qk_width), row(v_width), row(v_width), row(LANES), row(LANES), u_spec]
    out_shape = [jax.ShapeDtypeStruct((t, qk_width), BF16), jax.ShapeDtypeStruct((t, qk_width), BF16),
                 jax.ShapeDtypeStruct((t, v_width), BF16), jax.ShapeDtypeStruct((t, v_width), BF16),
                 jax.ShapeDtypeStruct((t, LANES), F32), jax.ShapeDtypeStruct((t, LANES), F32), u_shape]
    return pl.pallas_call(
        functools.partial(_c_proj_kernel, n_heads=n_heads, key_dim=key_dim, seq_len=seq_len,
                          tiles_per_seq=max(seq_len // tm, 1), n_over=n_over),
        grid=(n_tiles,),
        in_specs=in_specs,
        out_specs=out_specs,
        out_shape=out_shape,
        scratch_shapes=_conv_scratch(tm, conv_width, overrides is None),
        compiler_params=_params("arbitrary"),
        name="c_proj",
    )(x, gain, wqkv, wz, wba, taps, alog, dtb, *(overrides or []))


def _delta_kernel(q_ref, k_ref, v_ref, z_ref, gcol_ref, beta_ref, grow_ref, s0_ref, nw_ref,
                  o_ref, s_out_ref, m_ref, *, n_heads, chunk):
    step = pl.program_id(1)
    n_batch = s0_ref.shape[0]
    rows = q_ref.shape[0] // n_batch
    dk = q_ref.shape[1] // n_heads
    dv = v_ref.shape[1] // n_heads
    n_chunks = rows // chunk
    seqs = [(b, h) for b in range(n_batch) for h in range(n_heads)]
    ri = lax.broadcasted_iota(jnp.int32, (chunk, chunk), 0)
    ci = lax.broadcasted_iota(jnp.int32, (chunk, chunk), 1)
    incl = (ci <= ri)[None]
    strict = (ci < ri)[None]

    @pl.when(step == 0)
    def _():
        for i, (b, h) in enumerate(seqs):
            m_ref[i] = s0_ref[b, h].T

    probs = [(c, b, h) for c in range(n_chunks) for b, h in seqs]
    rs = lambda c, b: slice(b * rows + c * chunk, b * rows + (c + 1) * chunk)
    stack = lambda f: jnp.stack([f(c, b, h) for c, b, h in probs])
    kb = stack(lambda c, b, h: k_ref[rs(c, b), h * dk:(h + 1) * dk])
    qb = stack(lambda c, b, h: q_ref[rs(c, b), h * dk:(h + 1) * dk])
    vf = stack(lambda c, b, h: v_ref[rs(c, b), h * dv:(h + 1) * dv]).astype(F32)
    gc = stack(lambda c, b, h: gcol_ref[rs(c, b), h:h + 1])
    gr = stack(lambda c, b, h: grow_ref[b, h:h + 1, rs(c, 0)])
    bc = stack(lambda c, b, h: beta_ref[rs(c, b), h:h + 1])
    kf = kb.astype(F32)
    eg = jnp.exp(gc)
    decay = jnp.where(incl, jnp.exp(jnp.where(incl, gc - gr, 0.0)), 0.0)
    kqk = _bdot_nt(jnp.concatenate([kb, qb], axis=1), kb)
    a = jnp.where(strict, bc * kqk[:, :chunk] * decay, 0.0)
    nil = -a
    pw = a
    span = 2
    while span < chunk:
        pwb = pw.astype(BF16)
        pw = _bdot(pwb, pwb)
        nil = nil + pw + _bdot(nil.astype(BF16), pw.astype(BF16))
        span *= 2
    rhs = jnp.concatenate([bc * vf, (bc * eg) * kf], axis=2)
    w = rhs + _bdot(nil.astype(BF16), rhs.astype(BF16))
    w_v = w[:, :, :dv]
    p = (kqk[:, chunk:] * decay).astype(BF16)
    g_last = gc[:, chunk - 1:chunk, :]
    lhs = jnp.concatenate([w[:, :, dv:], eg * qb.astype(F32)], axis=1).astype(BF16)
    k_dec = (jnp.exp(g_last - gc) * kf).astype(BF16)
    g_chunk = jnp.exp(g_last)

    for c in range(n_chunks):
        hs = slice(c * len(seqs), (c + 1) * len(seqs))
        m = m_ref[...]
        r = _bdot(lhs[hs], m.astype(BF16))
        ub = (w_v[hs] - r[:, :chunk]).astype(BF16)
        o = r[:, chunk:] + _bdot(p[hs], ub)
        m_ref[...] = g_chunk[hs] * m + _bdot_tn(k_dec[hs], ub)
        on = o * lax.rsqrt(jnp.mean(o * o, axis=-1, keepdims=True) + NORM_EPS) * nw_ref[...]
        for i, (b, h) in enumerate(seqs):
            vl = slice(h * dv, (h + 1) * dv)
            o_ref[rs(c, b), vl] = (on[i] * _silu(z_ref[rs(c, b), vl].astype(F32))).astype(BF16)

    @pl.when(step == pl.num_programs(1) - 1)
    def _():
        for i, (b, h) in enumerate(seqs):
            s_out_ref[b, h] = m_ref[i].T


def _delta_rule(q, k, v, z, gcol, beta, grow, s0, norm_w, *, batch, seq_len, n_heads, chunk, rows, n_batch):
    t = q.shape[0]
    steps = seq_len // rows
    dk = q.shape[1] // n_heads
    dv = v.shape[1] // n_heads
    assert n_batch == 1 or steps == 1
    row = lambda c: pl.BlockSpec((n_batch * rows, c), lambda b, s: (b * steps + s, 0))
    state = pl.BlockSpec((n_batch, n_heads, dv, dk), lambda b, s: (b, 0, 0, 0))
    return pl.pallas_call(
        functools.partial(_delta_kernel, n_heads=n_heads, chunk=chunk),
        grid=(batch // n_batch, steps),
        in_specs=[row(q.shape[1]), row(k.shape[1]), row(v.shape[1]), row(z.shape[1]),
                  row(gcol.shape[1]), row(beta.shape[1]),
                  pl.BlockSpec((n_batch, n_heads, rows), lambda b, s: (b, 0, s)),
                  state, pl.BlockSpec(norm_w.shape, lambda b, s: (0, 0))],
        out_specs=[row(v.shape[1]), state],
        out_shape=[jax.ShapeDtypeStruct((t, v.shape[1]), BF16),
                   jax.ShapeDtypeStruct((batch, n_heads, dv, dk), F32)],
        scratch_shapes=[pltpu.VMEM((n_batch * n_heads, dk, dv), F32)],
        compiler_params=_params("arbitrary", "arbitrary"),
        name="gated_delta_rule",
    )(q, k, v, z, gcol, beta, grow, s0, norm_w)


def _pad_lanes(a):
    return jnp.pad(a, ((0, 0), (0, LANES - a.shape[1])))


def _rows_to_heads(a, batch, seq_len):
    return a.reshape(batch, seq_len, a.shape[1]).transpose(0, 2, 1)


def _heads_to_rows(a):
    b, h, length = a.shape
    return a.transpose(0, 2, 1).reshape(b * length, h)


def _feature_major_pool(pool):
    n_l, n_p, page = pool.shape[:3]
    perm = (0, 1) + tuple(range(3, pool.ndim)) + (2,)
    return pool.transpose(perm).reshape(n_l * n_p, -1, page)


def _mixer_ab(x, past, w, i, gain, *, batch, seq_len, tm):
    t = x.shape[0]
    hd = w["a_head_dim"]
    nh = w["ab_b_f"].shape[1]
    aw = nh * hd
    w_in = w["ab_w_in"][i]
    bw = (w_in.shape[1] - 3 * aw - nh) // 3
    wb = w_in[:, 3 * aw + nh:].astype(BF16)
    wf = w_in[:, 3 * aw:3 * aw + nh]
    taps = w["ab_conv_w"][i]
    n_keep = taps.shape[0] - 1
    if past is None:
        wq = w_in[:, :aw].astype(BF16)
        wkvt = w_in[:, aw:3 * aw].T.astype(BF16)
        kt, vt, ktb, vtb, lft, qb, ob, tails = _ab_proj_prompt(
            x, gain, wq, wkvt, wf.T.astype(BF16), w["ab_b_f"][i][:, None], wb, taps,
            batch=batch, seq_len=seq_len, head_dim=hd, tm=tm)
        o_a = _fox_attention(qb, ktb, vtb, _forget_bias_rows(lft, hd), head_dim=hd,
                             tile=min(1024, seq_len), kblock=min(512, seq_len))
        tails = tails.reshape(batch, seq_len // tm, SUBLANES, bw)[:, -1]
        new_buf = tails[:, SUBLANES - n_keep:, :]
        to_tokens = lambda a: a.reshape(batch, nh, hd, seq_len).transpose(0, 3, 1, 2)
        state = (to_tokens(kt), to_tokens(vt), lft.transpose(0, 2, 1), new_buf)
    else:
        overrides = _conv_overrides(past["state_b_conv"][i], seq_len)
        q_s, k32, v32, logf, ob, gcu = _ab_proj_sample(
            x, gain, w_in[:, :3 * aw].astype(BF16), _pad_lanes(wf).astype(BF16),
            _pad_lanes(w["ab_b_f"][i][None, :]), wb, taps, overrides,
            n_heads=nh, head_dim=hd, seq_len=seq_len)
        n_phys, page = past["cache_a_k"].shape[1:3]
        pad_keys = lambda a: jnp.pad(a.reshape(batch, seq_len, -1).transpose(0, 2, 1),
                                     ((0, 0), (0, 0), (0, page - seq_len)))
        o_a = _paged_attention(
            q_s.reshape(batch, seq_len, aw), pad_keys(k32), pad_keys(v32), pad_keys(logf),
            _feature_major_pool(past["cache_a_k"]), _feature_major_pool(past["cache_a_v"]),
            _feature_major_pool(past["cache_a_logf"]), past["page_table"] + i * n_phys, head_dim=hd)
        o_a = o_a.reshape(t, aw)
        new_buf = gcu.reshape(batch, seq_len, bw)[:, seq_len - n_keep:, :]
        state = (k32.reshape(batch, seq_len, nh, hd), v32.reshape(batch, seq_len, nh, hd),
                 logf.reshape(batch, seq_len, nh), new_buf)
    return [o_a, ob], w["ab_w_out"][i].astype(BF16), state


def _mixer_c(x, past, w, i, gain, *, batch, seq_len, tm):
    t = x.shape[0]
    nh = w["c_a_log"].shape[1]
    dv = w["c_norm_w"].shape[1]
    w_in = w["c_w_in"][i]
    v_width = nh * dv
    conv_width = w_in.shape[1] - v_width - 2 * nh
    dk = (conv_width - v_width) // 2 // nh
    wqkv = w_in[:, :conv_width].astype(BF16)
    wz = w_in[:, conv_width:conv_width + v_width].astype(BF16)
    wba = _pad_lanes(w_in[:, conv_width + v_width:]).astype(BF16)
    zeros_h = jnp.zeros((1, nh), F32)
    alog = _pad_lanes(jnp.concatenate([zeros_h, w["c_a_log"][i][None, :]], axis=1))
    dtb = _pad_lanes(jnp.concatenate([zeros_h, w["c_dt_bias"][i][None, :]], axis=1))
    taps = w["c_conv_w"][i]
    n_keep = taps.shape[0] - 1
    tm_c = min(tm, 512)
    if past is None:
        overrides = None
        s0 = jnp.zeros((batch, nh, dv, dk), F32)
    else:
        overrides = _conv_overrides(past["state_c_conv"][i], seq_len)
        s0 = past["state_c_S"][i]
    qn, kn, vn, zb, beta, gdec, u_out = _c_proj(
        x, gain, wqkv, wz, wba, taps, alog, dtb, overrides,
        n_heads=nh, key_dim=dk, seq_len=seq_len, tm=tm_c)
    if past is None:
        tails = u_out.reshape(batch, seq_len // tm_c, SUBLANES, conv_width)[:, -1]
        new_buf = tails[:, SUBLANES - n_keep:, :]
        len_pad = seq_len
    else:
        new_buf = u_out.reshape(batch, seq_len, conv_width)[:, seq_len - n_keep:, :]
        len_pad = -(-seq_len // DELTA_CHUNK) * DELTA_CHUNK
        pad3 = lambda a: jnp.pad(a.reshape(batch, seq_len, a.shape[1]),
                                 ((0, 0), (0, len_pad - seq_len), (0, 0))).reshape(batch * len_pad, a.shape[1])
        qn, kn, vn, zb, beta, gdec = [pad3(a) for a in (qn, kn, vn, zb, beta, gdec)]
    n_batch = DELTA_BATCH if len_pad <= DELTA_ROWS and batch % DELTA_BATCH == 0 else 1
    scan_len = -(-len_pad // LANES) * LANES
    g_heads = jnp.pad(_rows_to_heads(gdec[:, nh:2 * nh], batch, len_pad), ((0, 0), (0, 0), (0, scan_len - len_pad)))
    grow = _segment_cumsum(g_heads, DELTA_CHUNK)[:, :, :len_pad]
    og, s_new = _delta_rule(qn, kn, vn, zb, _pad_lanes(_heads_to_rows(grow)), beta, grow, s0, w["c_norm_w"][i][None, :],
                            batch=batch, seq_len=len_pad, n_heads=nh, chunk=DELTA_CHUNK,
                            rows=min(DELTA_ROWS, len_pad), n_batch=n_batch)
    if len_pad != seq_len:
        og = og.reshape(batch, len_pad, v_width)[:, :seq_len].reshape(t, v_width)
    return [og], w["c_w_out"][i].astype(BF16), (new_buf, s_new)


def _run_trunk(x3, past, w, *, tm):
    batch, seq_len, d = x3.shape
    x = x3.reshape(batch * seq_len, d)
    ab_states, c_states = [], []
    ffn = lambda x, g0, g1, layer, j, mixer=None: _ffn_block(
        x, g0, g1, w["ffn_w_gate"], w["ffn_w_up"], w["ffn_w_down"], layer, j, tm=tm, mixer=mixer)
    for layer in range(w["norm_g"].shape[0]):
        gains = [w["norm_g"][layer, j][None, :] for j in range(6)]
        x = ffn(x, gains[0], gains[1], layer, 0)
        mixer, states = (_mixer_ab, ab_states) if layer % 2 == 0 else (_mixer_c, c_states)
        parts, w_out, state = mixer(x, past, w, layer // 2, gains[2], batch=batch, seq_len=seq_len, tm=tm)
        states.append(state)
        x = ffn(x, gains[4], gains[5], layer, 1, mixer=(parts, w_out, gains[3]))
    ab_new = [jnp.stack(ts) for ts in zip(*ab_states)]
    c_new = [jnp.stack(ts) for ts in zip(*c_states)]
    return x.reshape(batch, seq_len, d), ab_new, c_new


def kernel(x_prompt, x_sample, cache_a_k, cache_a_v, cache_a_logf, page_table, state_b_conv, state_c_conv, state_c_S, norm_g, ffn_w_gate, ffn_w_up, ffn_w_down, ab_w_in, ab_b_f, ab_conv_w, ab_w_out, c_w_in, c_conv_w, c_a_log, c_dt_bias, c_norm_w, c_w_out):
    w = dict(norm_g=norm_g,
             ffn_w_gate=ffn_w_gate, ffn_w_up=ffn_w_up, ffn_w_down=ffn_w_down,
             ab_w_in=ab_w_in, ab_b_f=ab_b_f, ab_conv_w=ab_conv_w, ab_w_out=ab_w_out,
             c_w_in=c_w_in, c_conv_w=c_conv_w, c_a_log=c_a_log, c_dt_bias=c_dt_bias,
             c_norm_w=c_norm_w, c_w_out=c_w_out, a_head_dim=cache_a_k.shape[-1])
    past = dict(cache_a_k=cache_a_k, cache_a_v=cache_a_v, cache_a_logf=cache_a_logf, page_table=page_table,
                state_b_conv=state_b_conv, state_c_conv=state_c_conv, state_c_S=state_c_S)
    prompt_rows = x_prompt.shape[0] * x_prompt.shape[1]
    sample_rows = x_sample.shape[0] * x_sample.shape[1]
    y_prompt, ab_p, c_p = _run_trunk(x_prompt, None, w, tm=min(512, prompt_rows))
    y_sample, ab_s, c_s = _run_trunk(x_sample, past, w, tm=min(512, sample_rows))
    a_k_prompt, a_v_prompt, a_logf_prompt, b_conv_prompt = ab_p
    a_k_sample, a_v_sample, a_logf_sample, b_conv_sample = ab_s
    c_conv_prompt, c_S_prompt = c_p
    c_conv_sample, c_S_sample = c_s
    return (y_prompt, y_sample,
            a_k_prompt, a_v_prompt, a_logf_prompt,
            a_k_sample, a_v_sample, a_logf_sample,
            b_conv_prompt, b_conv_sample,
            c_conv_prompt, c_conv_sample,
            c_S_prompt, c_S_sample)
```
